```python
import jax
import jax.numpy as jnp
from jax import lax
import numpy as np

D_MODEL = 2048
BATCH = 4
SEQ = 4096
DEPTH = 2

GRID_W = 64
CTX_LEN = 256
N_MIXERS = 2
NORM_EPS = 1e-6

RET_HEADS = 8
RET_DK = D_MODEL // RET_HEADS
RET_DV = 2 * RET_DK
RET_CHUNK = 128
RET_ROPE_BASE = 10000.0

ATT_HEAD_DIM = 128
ATT_Q_HEADS = D_MODEL // ATT_HEAD_DIM
ATT_KV_HEADS = 4
ATT_GROUP = ATT_Q_HEADS // ATT_KV_HEADS
WINDOW = 128
ATT_BLOCK = 128
ROPE_BASE = 10000.0

D_FF = 4 * D_MODEL

N_RET_LAYERS = (DEPTH + 1) // 2
N_ATT_LAYERS = DEPTH // 2
NEG_INF = -1e30

kernel_name = "hybrid_retention_swa_dit"


def rms_norm(x, g):
    xf = x.astype(jnp.float32)
    y = xf * lax.rsqrt(jnp.mean(xf * xf, axis=-1, keepdims=True) + NORM_EPS)
    return (y * g.astype(jnp.float32)).astype(x.dtype)


def modulate(h, shift, scale):
    return h * (1 + scale) + shift


def rope_angles(pos, dim, base):
    inv_freq = base ** (-jnp.arange(0, dim, 2, dtype=jnp.float32) / dim)
    return pos.astype(jnp.float32)[:, None] * inv_freq[None, :]


def apply_rope(x, ang):
    x1, x2 = jnp.split(x.astype(jnp.float32), 2, axis=-1)
    cos = jnp.cos(ang)[:, None, :]
    sin = jnp.sin(ang)[:, None, :]
    return jnp.concatenate([x1 * cos - x2 * sin, x2 * cos + x1 * sin], axis=-1).astype(x.dtype)


def axial_rope(x, rows, cols):
    half = x.shape[-1] // 2
    xr = apply_rope(x[..., :half], rope_angles(rows, half, ROPE_BASE))
    xc = apply_rope(x[..., half:], rope_angles(cols, half, ROPE_BASE))
    return jnp.concatenate([xr, xc], axis=-1)


def retention_scan(q, k, v, log_gamma, s0):
    B, L, H, _ = q.shape
    dv = v.shape[-1]
    C = RET_CHUNK
    n = L // C

    def to_chunks(t):
        return t.reshape(B, n, C, H, t.shape[-1]).transpose(1, 0, 3, 2, 4)

    idx = jnp.arange(C, dtype=jnp.float32)
    diff = idx[:, None] - idx[None, :]
    lg = log_gamma.astype(jnp.float32)
    decay_in = jnp.where(diff >= 0, jnp.exp(lg[:, None, None] * jnp.maximum(diff, 0.0)), 0.0)
    decay_q = jnp.exp(lg[:, None] * (idx + 1.0))[None, :, :, None]
    decay_k = jnp.exp(lg[:, None] * (C - 1.0 - idx))[None, :, :, None]
    decay_c = jnp.exp(lg * C)[None, :, None, None]

    def step(s, qkv):
        qc, kc, vc = (t.astype(jnp.float32) for t in qkv)
        scores = jnp.einsum("bhid,bhjd->bhij", qc, kc) * decay_in
        o = (jnp.einsum("bhij,bhjv->bhiv", scores, vc)
             + jnp.einsum("bhid,bhdv->bhiv", qc, s) * decay_q)
        s_new = s * decay_c + jnp.einsum("bhjd,bhjv->bhdv", kc * decay_k, vc)
        return s_new, o

    s_fin, o = lax.scan(step, s0, (to_chunks(q), to_chunks(k), to_chunks(v)))
    o = o.transpose(1, 0, 3, 2, 4).reshape(B, L, H, dv)
    return o, s_fin


def retention_mixer(h_ctx, h_lat, w_in, w_out, decay_fwd, decay_bwd, ang_lat, need_ctx_out):
    qk_w = RET_HEADS * RET_DK
    v_w = RET_HEADS * RET_DV

    def project(h):
        B, L, _ = h.shape
        q, k, v, g = jnp.split(h @ w_in, [qk_w, 2 * qk_w, 2 * qk_w + v_w], axis=-1)
        return (q.reshape(B, L, RET_HEADS, RET_DK),
                k.reshape(B, L, RET_HEADS, RET_DK) * (RET_DK ** -0.5),
                v.reshape(B, L, RET_HEADS, RET_DV),
                g)

    qc, kc, vc, gc = project(h_ctx)
    ql, kl, vl, gl = project(h_lat)
    ql = apply_rope(ql, ang_lat)
    kl = apply_rope(kl, ang_lat)
    lg_f = jax.nn.log_sigmoid(decay_fwd.astype(jnp.float32))
    lg_b = jax.nn.log_sigmoid(decay_bwd.astype(jnp.float32))
    B = h_lat.shape[0]
    s0 = jnp.zeros((B, RET_HEADS, RET_DK, RET_DV), jnp.float32)

    def flip(t):
        return jnp.flip(t, axis=1)

    oc_f, sc_f = retention_scan(qc, kc, vc, lg_f, s0)
    oc_b, sc_b = retention_scan(flip(qc), flip(kc), flip(vc), lg_b, s0)
    ol_f, _ = retention_scan(ql, kl, vl, lg_f, sc_f)
    ol_b, _ = retention_scan(flip(ql), flip(kl), flip(vl), lg_b, sc_b)

    def readout(o, g):
        Bo, Lo = o.shape[:2]
        y = o * lax.rsqrt(jnp.mean(o * o, axis=-1, keepdims=True) + NORM_EPS)
        y = y.reshape(Bo, Lo, RET_HEADS * RET_DV).astype(g.dtype)
        return (jax.nn.silu(g) * y) @ w_out

    out_lat = readout(ol_f + flip(ol_b), gl)
    out_ctx = readout(oc_f + flip(oc_b), gc) if need_ctx_out else None
    return out_ctx, out_lat


def window_attention_mixer(h_ctx, h_lat, w_in, w_out, sink, rows, cols, need_ctx_out):
    Hq, Hkv, G, dh = ATT_Q_HEADS, ATT_KV_HEADS, ATT_GROUP, ATT_HEAD_DIM
    scale = dh ** -0.5
    f32 = jnp.float32

    def project(h):
        B, L, _ = h.shape
        q, k, v = jnp.split(h @ w_in, [Hq * dh, (Hq + Hkv) * dh], axis=-1)
        return q.reshape(B, L, Hq, dh), k.reshape(B, L, Hkv, dh), v.reshape(B, L, Hkv, dh)

    qc, kc, vc = project(h_ctx)
    ql, kl, vl = project(h_lat)
    ql = axial_rope(ql, rows, cols)
    kl = axial_rope(kl, rows, cols)
    B, L = ql.shape[:2]
    nb = L // ATT_BLOCK
    sink_f = sink.astype(f32).reshape(Hkv, G)

    qb = ql.reshape(B, nb, ATT_BLOCK, Hkv, G, dh)

    def band(t):
        tp = jnp.pad(t, ((0, 0), (ATT_BLOCK, ATT_BLOCK), (0, 0), (0, 0)))
        tp = tp.reshape(B, nb + 2, ATT_BLOCK, Hkv, dh)
        return jnp.concatenate([tp[:, :-2], tp[:, 1:-1], tp[:, 2:]], axis=2)

    kb, vb = band(kl), band(vl)
    s_loc = jnp.einsum("bnqkgd,bnskd->bnkgqs", qb, kb, preferred_element_type=f32) * scale
    s_ctx = jnp.einsum("bnqkgd,bckd->bnkgqc", qb, kc, preferred_element_type=f32) * scale
    qi = jnp.arange(ATT_BLOCK)[:, None]
    kj = jnp.arange(3 * ATT_BLOCK)[None, :]
    kpos = (jnp.arange(nb)[:, None, None] - 1) * ATT_BLOCK + kj[None]
    mask = (jnp.abs(kj - ATT_BLOCK - qi)[None] <= WINDOW) & (kpos >= 0) & (kpos < L)
    s_loc = jnp.where(mask[None, :, None, None], s_loc, NEG_INF)
    sink_l = sink_f[None, None, :, :, None, None]
    m = jnp.maximum(jnp.maximum(s_loc.max(-1, keepdims=True), s_ctx.max(-1, keepdims=True)), sink_l)
    p_loc = jnp.exp(s_loc - m)
    p_ctx = jnp.exp(s_ctx - m)
    denom = p_loc.sum(-1, keepdims=True) + p_ctx.sum(-1, keepdims=True) + jnp.exp(sink_l - m)
    o = (jnp.einsum("bnkgqs,bnskd->bnqkgd", p_loc.astype(vl.dtype), vb, preferred_element_type=f32)
         + jnp.einsum("bnkgqc,bckd->bnqkgd", p_ctx.astype(vc.dtype), vc, preferred_element_type=f32))
    o = o / jnp.moveaxis(denom, 4, 2)
    out_lat = o.reshape(B, L, Hq * dh).astype(h_lat.dtype) @ w_out

    if need_ctx_out:
        Bc, Lc = qc.shape[:2]
        qcg = qc.reshape(Bc, Lc, Hkv, G, dh)
        s = jnp.einsum("bqkgd,bckd->bkgqc", qcg, kc, preferred_element_type=f32) * scale
        sink_c = sink_f[None, :, :, None, None]
        mc = jnp.maximum(s.max(-1, keepdims=True), sink_c)
        p = jnp.exp(s - mc)
        denom_c = p.sum(-1, keepdims=True) + jnp.exp(sink_c - mc)
        oc = jnp.einsum("bkgqc,bckd->bqkgd", p.astype(vc.dtype), vc, preferred_element_type=f32)
        oc = oc / jnp.moveaxis(denom_c, 3, 1)
        out_ctx = oc.reshape(Bc, Lc, Hq * dh).astype(h_ctx.dtype) @ w_out
    else:
        out_ctx = None
    return out_ctx, out_lat


def squared_relu_mlp(h, w1, w2):
    return jnp.square(jax.nn.relu(h @ w1)) @ w2


def setup_inputs(seed: int = 0) -> dict:
    key = jax.random.key(seed)
    ks = jax.random.split(key, 20)
    f32 = jnp.float32
    D = D_MODEL

    def nrm(k, shape, fan_in):
        return jax.random.normal(k, shape, f32) * (fan_in ** -0.5)

    ret_in_w = RET_HEADS * (2 * RET_DK + 2 * RET_DV)
    att_in_w = (ATT_Q_HEADS + 2 * ATT_KV_HEADS) * ATT_HEAD_DIM
    gamma = 1.0 - 2.0 ** (-5.0 - jnp.arange(RET_HEADS, dtype=f32))
    decay_logit = jnp.log(gamma) - jnp.log1p(-gamma)

    return {
        "x": jax.random.normal(ks[0], (BATCH, SEQ, D), f32),
        "c": jax.random.normal(ks[1], (BATCH, D), f32),
        "ctx": jax.random.normal(ks[2], (BATCH, CTX_LEN, D), f32),
        "c_ctx": jax.random.normal(ks[3], (D,), f32),
        "ada_w": nrm(ks[4], (DEPTH, D, 6 * D), D),
        "ada_b": 0.02 * jax.random.normal(ks[5], (DEPTH, 6 * D), f32),
        "norm_mix_g": 1.0 + 0.02 * jax.random.normal(ks[6], (DEPTH, D), f32),
        "norm_mlp_g": 1.0 + 0.02 * jax.random.normal(ks[7], (DEPTH, D), f32),
        "mlp_w1": nrm(ks[8], (DEPTH, D, D_FF), D),
        "mlp_w2": nrm(ks[9], (DEPTH, D_FF, D), D_FF),
        "ret_w_in": nrm(ks[10], (N_RET_LAYERS, D, ret_in_w), D),
        "ret_w_out": nrm(ks[11], (N_RET_LAYERS, RET_HEADS * RET_DV, D), RET_HEADS * RET_DV),
        "ret_decay_fwd": decay_logit[None, :] + 0.1 * jax.random.normal(ks[12], (N_RET_LAYERS, RET_HEADS), f32),
        "ret_decay_bwd": decay_logit[None, :] + 0.1 * jax.random.normal(ks[13], (N_RET_LAYERS, RET_HEADS), f32),
        "attn_w_in": nrm(ks[14], (N_ATT_LAYERS, D, att_in_w), D),
        "attn_w_out": nrm(ks[15], (N_ATT_LAYERS, ATT_Q_HEADS * ATT_HEAD_DIM, D), ATT_Q_HEADS * ATT_HEAD_DIM),
        "attn_sink": jax.random.normal(ks[16], (N_ATT_LAYERS, ATT_Q_HEADS), f32),
        "final_norm_g": 1.0 + 0.02 * jax.random.normal(ks[17], (D,), f32),
    }


def reference(x, c, ctx, c_ctx, ada_w, ada_b, norm_mix_g, norm_mlp_g, mlp_w1, mlp_w2,
              ret_w_in, ret_w_out, ret_decay_fwd, ret_decay_bwd,
              attn_w_in, attn_w_out, attn_sink, final_norm_g):
    L = x.shape[1]
    ROWS = L // GRID_W
    rows = jnp.repeat(jnp.arange(ROWS), GRID_W)
    cols = jnp.arange(ROWS * GRID_W) % GRID_W
    ret_ang = rope_angles(jnp.arange(L), RET_DK, RET_ROPE_BASE)
    cond_lat = jax.nn.silu(c)
    cond_ctx = jax.nn.silu(c_ctx)
    h_ctx = ctx

    for i in range(DEPTH):
        last = i == DEPTH - 1
        j = i // N_MIXERS
        mod_lat = (cond_lat @ ada_w[i] + ada_b[i])[:, None, :]
        mod_ctx = (cond_ctx @ ada_w[i] + ada_b[i])[None, None, :]
        sh1_l, sc1_l, g1_l, sh2_l, sc2_l, g2_l = jnp.split(mod_lat, 6, axis=-1)
        sh1_c, sc1_c, g1_c, sh2_c, sc2_c, g2_c = jnp.split(mod_ctx, 6, axis=-1)

        a_lat = modulate(rms_norm(x, norm_mix_g[i]), sh1_l, sc1_l)
        a_ctx = modulate(rms_norm(h_ctx, norm_mix_g[i]), sh1_c, sc1_c)
        if i % N_MIXERS == 0:
            o_ctx, o_lat = retention_mixer(a_ctx, a_lat, ret_w_in[j], ret_w_out[j],
                                           ret_decay_fwd[j], ret_decay_bwd[j], ret_ang, not last)
        else:
            o_ctx, o_lat = window_attention_mixer(a_ctx, a_lat, attn_w_in[j], attn_w_out[j],
                                                  attn_sink[j], rows, cols, not last)

        x = x + g1_l * o_lat
        x = x + g2_l * squared_relu_mlp(modulate(rms_norm(x, norm_mlp_g[i]), sh2_l, sc2_l),
                                        mlp_w1[i], mlp_w2[i])
        if not last:
            h_ctx = h_ctx + g1_c * o_ctx
            h_ctx = h_ctx + g2_c * squared_relu_mlp(modulate(rms_norm(h_ctx, norm_mlp_g[i]), sh2_c, sc2_c),
                                                    mlp_w1[i], mlp_w2[i])

    return rms_norm(x, final_norm_g)
```

```python
import functools

import jax
import jax.numpy as jnp
from jax import lax
from jax.experimental import pallas as pl
from jax.experimental.pallas import tpu as pltpu

F32 = jnp.float32
BF16 = jnp.bfloat16

NORM_EPS = 1e-6
NEG_INF = -1e30
GRID_W = 64
ROPE_BASE = 10000.0

RET_HEADS = 8
ATT_HEAD_DIM = 128
ATT_KV_HEADS = 4
WINDOW = 128

MOD_ROWS = 8
VMEM_LIMIT = 56 * 1024 * 1024


def _params(semantics, vmem=VMEM_LIMIT):
    return pltpu.CompilerParams(dimension_semantics=semantics, vmem_limit_bytes=vmem)


def _norm_mod(x, g, shift, scale):
    ms = jnp.mean(x * x, axis=-1, keepdims=True)
    y = x * lax.rsqrt(ms + NORM_EPS) * g
    return y * (1.0 + scale) + shift


def _ada_kernel(cond_ref, w_ref, b_ref, o_ref):
    c = cond_ref[...]
    s = c * jax.nn.sigmoid(c)
    o_ref[...] = jnp.dot(s.astype(BF16), w_ref[...].astype(BF16),
                         preferred_element_type=F32) + b_ref[...]


def _ada_call(cond, ada_w, ada_b, tn=1024):
    depth, d, n = ada_w.shape
    return pl.pallas_call(
        _ada_kernel,
        grid=(depth, n // tn),
        in_specs=[
            pl.BlockSpec((MOD_ROWS, d), lambda l, j: (0, 0)),
            pl.BlockSpec((None, d, tn), lambda l, j: (l, 0, j)),
            pl.BlockSpec((None, 1, tn), lambda l, j: (l, 0, j)),
        ],
        out_specs=pl.BlockSpec((None, MOD_ROWS, tn), lambda l, j: (l, 0, j)),
        out_shape=jax.ShapeDtypeStruct((depth, MOD_ROWS, n), F32),
        compiler_params=_params(("arbitrary", "arbitrary")),
        name="ada_mod",
    )(cond, ada_w, ada_b.reshape(depth, 1, n))


def _swap32(x):
    lane = lax.broadcasted_iota(jnp.int32, x.shape, 1)
    return jnp.where((lane & 32) == 0, pltpu.roll(x, 96, 1), pltpu.roll(x, 32, 1))


def _proj_kernel(*refs, rope, rope_tiles, k_tiles, k_scale):
    if rope is None:
        x_ref, g_ref, shift_ref, scale_ref, w_ref, o_ref, a_ref = refs
        cos_ref = sin_ref = None
    else:
        x_ref, g_ref, shift_ref, scale_ref, w_ref, cos_ref, sin_ref, o_ref, a_ref = refs
    j = pl.program_id(1)

    @pl.when(j == 0)
    def _():
        a = _norm_mod(x_ref[...], g_ref[...], shift_ref[...], scale_ref[...])
        a_ref[...] = a.astype(BF16)

    acc = jnp.dot(a_ref[...], w_ref[...], preferred_element_type=F32)
    if k_scale is not None:
        in_k = (j >= k_tiles[0]) & (j < k_tiles[1])
        acc = acc * jnp.where(in_k, k_scale, 1.0).astype(F32)
    tn = acc.shape[1]

    if rope is None:
        o_ref[...] = acc.astype(o_ref.dtype)
        return

    @pl.when(j < rope_tiles)
    def _():
        cos = cos_ref[...]
        sin = sin_ref[...]
        if rope == "half":
            for hh in range(tn // 256):
                x1 = acc[:, hh * 256:hh * 256 + 128]
                x2 = acc[:, hh * 256 + 128:(hh + 1) * 256]
                o_ref[:, hh * 256:hh * 256 + 128] = (x1 * cos - x2 * sin).astype(o_ref.dtype)
                o_ref[:, hh * 256 + 128:(hh + 1) * 256] = (x2 * cos + x1 * sin).astype(o_ref.dtype)
        else:
            for hh in range(tn // 128):
                xh = acc[:, hh * 128:(hh + 1) * 128]
                o_ref[:, hh * 128:(hh + 1) * 128] = (xh * cos + _swap32(xh) * sin).astype(o_ref.dtype)

    @pl.when(j >= rope_tiles)
    def _():
        o_ref[...] = acc.astype(o_ref.dtype)


def _proj_call(x2d, g, mod4, layer, w, *, tm, tn, rows_per_mod, mod_row0, n_off=0, n_cols=None,
               rope=None, cos=None, sin=None, rope_cols=0, k_cols=None, k_scale=None, name="proj"):
    t, d = x2d.shape
    n_cols = w.shape[1] if n_cols is None else n_cols
    nt = n_cols // tn
    tiles_per_mod = rows_per_mod // tm
    tiles_per_seq = None if rope is None else cos.shape[0] // tm

    def mod_spec(k):
        return pl.BlockSpec((None, None, 1, d),
                            lambda i, j: (layer, mod_row0 + i // tiles_per_mod, 0, k))

    in_specs = [
        pl.BlockSpec((tm, d), lambda i, j: (i, 0)),
        pl.BlockSpec((1, d), lambda i, j: (0, 0)),
        mod_spec(0), mod_spec(1),
        pl.BlockSpec((d, tn), lambda i, j: (0, n_off + j)),
    ]
    args = [x2d, g.reshape(1, d), mod4, mod4, w]
    if rope is not None:
        in_specs += [pl.BlockSpec((tm, 128), lambda i, j: (i % tiles_per_seq, 0))] * 2
        args += [cos, sin]
    body = functools.partial(
        _proj_kernel, rope=rope, rope_tiles=rope_cols // tn,
        k_tiles=None if k_cols is None else (k_cols[0] // tn, k_cols[1] // tn), k_scale=k_scale)
    return pl.pallas_call(
        body,
        grid=(t // tm, nt),
        in_specs=in_specs,
        out_specs=pl.BlockSpec((tm, tn), lambda i, j: (i, j)),
        out_shape=jax.ShapeDtypeStruct((t, n_cols), BF16),
        scratch_shapes=[pltpu.VMEM((tm, d), BF16)],
        compiler_params=_params(("parallel", "arbitrary")),
        name=name,
    )(*args)


def _out_kernel(y_ref, w_ref, x_ref, gate_ref, o_ref):
    acc = jnp.dot(y_ref[...], w_ref[...], preferred_element_type=F32)
    o_ref[...] = x_ref[...] + gate_ref[...] * acc


def _out_call(y, w, x2d, mod4, layer, *, tm, tn, rows_per_mod, mod_row0, name="out_proj"):
    t, k = y.shape
    d = w.shape[1]
    tiles_per_mod = rows_per_mod // tm
    gate_blk0 = 2 * d // tn
    return pl.pallas_call(
        _out_kernel,
        grid=(t // tm, d // tn),
        in_specs=[
            pl.BlockSpec((tm, k), lambda i, j: (i, 0)),
            pl.BlockSpec((k, tn), lambda i, j: (0, j)),
            pl.BlockSpec((tm, tn), lambda i, j: (i, j)),
            pl.BlockSpec((None, None, 1, tn),
                         lambda i, j: (layer, mod_row0 + i // tiles_per_mod, 0, gate_blk0 + j)),
        ],
        out_specs=pl.BlockSpec((tm, tn), lambda i, j: (i, j)),
        out_shape=jax.ShapeDtypeStruct((t, d), F32),
        compiler_params=_params(("parallel", "arbitrary")),
        name=name,
    )(y, w, x2d, mod4)


def _mlp_kernel(*refs, nf, final):
    if final:
        x_ref, g_ref, shift_ref, scale_ref, gate_ref, w1_ref, w2_ref, fg_ref, o_ref, a_ref = refs
    else:
        x_ref, g_ref, shift_ref, scale_ref, gate_ref, w1_ref, w2_ref, o_ref, a_ref = refs
        fg_ref = None
    f = pl.program_id(1)

    @pl.when(f == 0)
    def _():
        a = _norm_mod(x_ref[...], g_ref[...], shift_ref[...], scale_ref[...])
        a_ref[...] = a.astype(BF16)

    h = jnp.dot(a_ref[...], w1_ref[...], preferred_element_type=F32)
    h = jnp.square(jnp.maximum(h, 0.0)).astype(BF16)
    p = jnp.dot(h, w2_ref[...], preferred_element_type=F32)

    @pl.when(f == 0)
    def _():
        o_ref[...] = p

    @pl.when(f > 0)
    def _():
        o_ref[...] += p

    @pl.when(f == nf - 1)
    def _():
        r = x_ref[...] + gate_ref[...] * o_ref[...]
        if final:
            ms = jnp.mean(r * r, axis=-1, keepdims=True)
            r = r * lax.rsqrt(ms + NORM_EPS) * fg_ref[...]
        o_ref[...] = r


def _mlp_call(x2d, g, mod4, layer, w1, w2, *, tm, tf, rows_per_mod, mod_row0, final_g=None, name="mlp"):
    t, d = x2d.shape
    ff = w1.shape[1]
    nf = ff // tf
    tiles_per_mod = rows_per_mod // tm

    def mod_spec(k):
        return pl.BlockSpec((None, None, 1, d),
                            lambda i, f: (layer, mod_row0 + i // tiles_per_mod, 0, k))

    in_specs = [
        pl.BlockSpec((tm, d), lambda i, f: (i, 0)),
        pl.BlockSpec((1, d), lambda i, f: (0, 0)),
        mod_spec(3), mod_spec(4), mod_spec(5),
        pl.BlockSpec((d, tf), lambda i, f: (0, f)),
        pl.BlockSpec((tf, d), lambda i, f: (f, 0)),
    ]
    args = [x2d, g.reshape(1, d), mod4, mod4, mod4, w1, w2]
    if final_g is not None:
        in_specs.append(pl.BlockSpec((1, d), lambda i, f: (0, 0)))
        args.append(final_g.reshape(1, d))
    return pl.pallas_call(
        functools.partial(_mlp_kernel, nf=nf, final=final_g is not None),
        grid=(t // tm, nf),
        in_specs=in_specs,
        out_specs=pl.BlockSpec((tm, d), lambda i, f: (i, 0)),
        out_shape=jax.ShapeDtypeStruct((t, d), F32),
        scratch_shapes=[pltpu.VMEM((tm, d), BF16)],
        compiler_params=_params(("parallel", "arbitrary")),
        name=name,
    )(*args)


def _log_sigmoid(x):
    return jnp.minimum(x, 0.0) - jnp.log1p(jnp.exp(-jnp.abs(x)))


def _ret_kernel(df_ref, db_ref, q_ref, k_ref, v_ref, g_ref, s0f_ref, s0b_ref,
                y_ref, sff_ref, sfb_ref, of_ref, sf_ref, sb_ref, *, seq, ch):
    h = pl.program_id(1)
    n = seq // ch
    lgf = _log_sigmoid(jnp.full((ch, ch), df_ref[h], F32))
    lgb = _log_sigmoid(jnp.full((ch, ch), db_ref[h], F32))
    ii = lax.broadcasted_iota(jnp.int32, (ch, ch), 0)
    jj = lax.broadcasted_iota(jnp.int32, (ch, ch), 1)
    diff = (ii - jj).astype(F32)
    decay = (jnp.where(diff >= 0, jnp.exp(lgf * jnp.maximum(diff, 0.0)), 0.0)
             + jnp.where(diff <= 0, jnp.exp(lgb * jnp.maximum(-diff, 0.0)), 0.0))
    col = lax.broadcasted_iota(jnp.int32, (ch, 1), 0).astype(F32)
    lgf_c = _log_sigmoid(jnp.full((ch, 1), df_ref[h], F32))
    lgb_c = _log_sigmoid(jnp.full((ch, 1), db_ref[h], F32))
    dq_f = jnp.exp(lgf_c * (col + 1.0))
    dk_f = jnp.exp(lgf_c * (ch - 1.0 - col))
    dc_f = jnp.exp(_log_sigmoid(jnp.full((1, 1), df_ref[h], F32)) * float(ch))
    dq_b = jnp.exp(lgb_c * (ch - col))
    dk_b = jnp.exp(lgb_c * col)
    dc_b = jnp.exp(_log_sigmoid(jnp.full((1, 1), db_ref[h], F32)) * float(ch))
    tdims = (((0,), (0,)), ((), ()))
    ndims = (((1,), (1,)), ((), ()))

    sf_ref[...] = s0f_ref[...]

    def fwd(c, carry):
        r = pl.ds(pl.multiple_of(c * ch, ch), ch)
        q = q_ref[r, :]
        k = k_ref[r, :]
        v = v_ref[r, :]
        s = sf_ref[...]
        of_ref[r, :] = jnp.dot(q, s.astype(BF16), preferred_element_type=F32) * dq_f
        kd = (k.astype(F32) * dk_f).astype(BF16)
        sf_ref[...] = s * dc_f + lax.dot_general(kd, v, tdims, preferred_element_type=F32)
        return carry

    lax.fori_loop(0, n, fwd, 0)
    sff_ref[...] = sf_ref[...]
    sb_ref[...] = s0b_ref[...]

    def bwd(t, carry):
        c = n - 1 - t
        r = pl.ds(pl.multiple_of(c * ch, ch), ch)
        q = q_ref[r, :]
        k = k_ref[r, :]
        v = v_ref[r, :]
        s = sb_ref[...]
        scores = lax.dot_general(q, k, ndims, preferred_element_type=F32) * decay
        o = (jnp.dot(scores.astype(BF16), v, preferred_element_type=F32)
             + jnp.dot(q, s.astype(BF16), preferred_element_type=F32) * dq_b
             + of_ref[r, :])
        y = o * lax.rsqrt(jnp.mean(o * o, axis=-1, keepdims=True) + NORM_EPS)
        g = g_ref[r, :].astype(F32)
        y_ref[r, :] = (g * jax.nn.sigmoid(g) * y).astype(y_ref.dtype)
        kd = (k.astype(F32) * dk_b).astype(BF16)
        sb_ref[...] = s * dc_b + lax.dot_general(kd, v, tdims, preferred_element_type=F32)
        return carry

    lax.fori_loop(0, n, bwd, 0)
    sfb_ref[...] = sb_ref[...]


def _ret_call(qkvg, decay_f, decay_b, s0f, s0b, *, batch, seq, ch, name="retention"):
    heads = RET_HEADS
    dk = s0f.shape[2]
    dv = s0f.shape[3]
    smem = pl.BlockSpec(memory_space=pltpu.SMEM)
    state = pl.BlockSpec((None, None, dk, dv), lambda b, h: (b, h, 0, 0))
    return pl.pallas_call(
        functools.partial(_ret_kernel, seq=seq, ch=ch),
        grid=(batch, heads),
        in_specs=[
            smem, smem,
            pl.BlockSpec((seq, dk), lambda b, h: (b, h)),
            pl.BlockSpec((seq, dk), lambda b, h: (b, heads + h)),
            pl.BlockSpec((seq, dv), lambda b, h: (b, heads + h)),
            pl.BlockSpec((seq, dv), lambda b, h: (b, 2 * heads + h)),
            state, state,
        ],
        out_specs=[pl.BlockSpec((seq, dv), lambda b, h: (b, h)), state, state],
        out_shape=[
            jax.ShapeDtypeStruct((batch * seq, heads * dv), BF16),
            jax.ShapeDtypeStruct(s0f.shape, F32),
            jax.ShapeDtypeStruct(s0f.shape, F32),
        ],
        scratch_shapes=[
            pltpu.VMEM((seq, dv), F32),
            pltpu.VMEM((dk, dv), F32),
            pltpu.VMEM((dk, dv), F32),
        ],
        compiler_params=_params(("parallel", "arbitrary")),
        name=name,
    )(decay_f, decay_b, qkvg, qkvg, qkvg, qkvg, s0f, s0b)


def _attn_kernel(sink_ref, q_ref, kp_ref, kc_ref, kn_ref, vp_ref, vc_ref, vn_ref, kx_ref, vx_ref,
                 o_ref, *, nq, tq, group):
    n = pl.program_id(1)
    dh = ATT_HEAD_DIM
    n_ctx = kx_ref.shape[0]
    win = tq + 2 * WINDOW
    rows = group * tq
    nk = win + n_ctx
    scale = dh ** -0.5
    r = lax.broadcasted_iota(jnp.int32, (rows, nk), 0)
    c = lax.broadcasted_iota(jnp.int32, (rows, nk), 1)
    qpos = r % tq
    kpos = c - WINDOW
    ok = ((jnp.abs(kpos - qpos) <= WINDOW)
          & ((kpos >= 0) | (n > 0))
          & ((kpos < tq) | (n < nq - 1)))
    ok = ok | (c >= win)
    rcol = lax.broadcasted_iota(jnp.int32, (rows, 1), 0) // tq
    ndims = (((1,), (1,)), ((), ()))

    for kh in range(ATT_KV_HEADS):
        hs = slice(kh * dh, (kh + 1) * dh)
        qs = jnp.concatenate(
            [q_ref[:, (kh * group + g) * dh:(kh * group + g + 1) * dh] for g in range(group)], axis=0)
        k_all = jnp.concatenate([kp_ref[:, hs], kc_ref[:, hs], kn_ref[:, hs], kx_ref[:, hs]], axis=0)
        v_all = jnp.concatenate([vp_ref[:, hs], vc_ref[:, hs], vn_ref[:, hs], vx_ref[:, hs]], axis=0)
        s = lax.dot_general(qs, k_all, ndims, preferred_element_type=F32) * scale
        s = jnp.where(ok, s, NEG_INF)
        sink = jnp.zeros((rows, 1), F32)
        for g in range(group):
            sink = jnp.where(rcol == g, sink_ref[kh * group + g], sink)
        m = jnp.maximum(jnp.max(s, axis=-1, keepdims=True), sink)
        p = jnp.exp(s - m)
        denom = jnp.sum(p, axis=-1, keepdims=True) + jnp.exp(sink - m)
        o = jnp.dot(p.astype(BF16), v_all, preferred_element_type=F32) / denom
        for g in range(group):
            o_ref[:, (kh * group + g) * dh:(kh * group + g + 1) * dh] = (
                o[g * tq:(g + 1) * tq, :].astype(o_ref.dtype))


def _attn_call(qkv, kv_ctx, sink, *, batch, seq, n_ctx, tq, name="window_attn"):
    dh = ATT_HEAD_DIM
    kvw = ATT_KV_HEADS * dh
    qw = qkv.shape[1] - 2 * kvw
    group = qw // kvw
    nq = seq // tq
    wpb = seq // WINDOW
    step = tq // WINDOW
    k_blk = qw // kvw
    v_blk = k_blk + 1

    def prev_map(col):
        return lambda b, n: (b * wpb + jnp.maximum(n * step - 1, 0), col)

    def cur_map(col):
        return lambda b, n: (b * nq + n, col)

    def next_map(col):
        return lambda b, n: (b * wpb + jnp.minimum((n + 1) * step, wpb - 1), col)

    return pl.pallas_call(
        functools.partial(_attn_kernel, nq=nq, tq=tq, group=group),
        grid=(batch, nq),
        in_specs=[
            pl.BlockSpec(memory_space=pltpu.SMEM),
            pl.BlockSpec((tq, qw), lambda b, n: (b * nq + n, 0)),
            pl.BlockSpec((WINDOW, kvw), prev_map(k_blk)),
            pl.BlockSpec((tq, kvw), cur_map(k_blk)),
            pl.BlockSpec((WINDOW, kvw), next_map(k_blk)),
            pl.BlockSpec((WINDOW, kvw), prev_map(v_blk)),
            pl.BlockSpec((tq, kvw), cur_map(v_blk)),
            pl.BlockSpec((WINDOW, kvw), next_map(v_blk)),
            pl.BlockSpec((n_ctx, kvw), lambda b, n: (b, 0)),
            pl.BlockSpec((n_ctx, kvw), lambda b, n: (b, 1)),
        ],
        out_specs=pl.BlockSpec((tq, qw), lambda b, n: (b * nq + n, 0)),
        out_shape=jax.ShapeDtypeStruct((batch * seq, qw), BF16),
        compiler_params=_params(("parallel", "arbitrary")),
        name=name,
    )(sink, qkv, qkv, qkv, qkv, qkv, qkv, qkv, kv_ctx, kv_ctx)


def _rope_angles(pos, dim, base):
    inv_freq = base ** (-jnp.arange(0, dim, 2, dtype=F32) / dim)
    return pos.astype(F32)[:, None] * inv_freq[None, :]


def _axial_tables(seq):
    rows = jnp.arange(seq) // GRID_W
    cols = jnp.arange(seq) % GRID_W
    half = ATT_HEAD_DIM // 2
    ar = _rope_angles(rows, half, ROPE_BASE)
    ac = _rope_angles(cols, half, ROPE_BASE)
    cos = jnp.concatenate([jnp.cos(ar), jnp.cos(ar), jnp.cos(ac), jnp.cos(ac)], axis=-1)
    sin = jnp.concatenate([-jnp.sin(ar), jnp.sin(ar), -jnp.sin(ac), jnp.sin(ac)], axis=-1)
    return cos, sin


def kernel(x, c, ctx, c_ctx, ada_w, ada_b, norm_mix_g, norm_mlp_g, mlp_w1, mlp_w2, ret_w_in, ret_w_out,
           ret_decay_fwd, ret_decay_bwd, attn_w_in, attn_w_out, attn_sink, final_norm_g):
    batch, seq, d = x.shape
    n_ctx = ctx.shape[1]
    assert ada_w.shape[0] == 2 and ret_w_in.shape[0] == 1 and attn_w_in.shape[0] == 1
    assert batch + 1 <= MOD_ROWS
    ret_dk = d // RET_HEADS
    ret_dv = 2 * ret_dk
    qk_w = RET_HEADS * ret_dk
    att_qw = attn_w_out.shape[1]
    att_kvw = ATT_KV_HEADS * ATT_HEAD_DIM
    ctx_row = batch

    cond = jnp.concatenate([c, c_ctx[None, :], jnp.zeros((MOD_ROWS - batch - 1, d), F32)], axis=0)
    mod = _ada_call(cond, ada_w, ada_b)
    mod4 = mod.reshape(mod.shape[0], MOD_ROWS, 1, mod.shape[2])

    xl = x.reshape(batch * seq, d)
    xc = ctx.reshape(batch * n_ctx, d)
    t_ctx = batch * n_ctx
    lat = dict(rows_per_mod=seq, mod_row0=0)
    cx = dict(rows_per_mod=t_ctx, mod_row0=ctx_row)

    ret_ang = _rope_angles(jnp.arange(seq), ret_dk, ROPE_BASE)
    ret_cos, ret_sin = jnp.cos(ret_ang), jnp.sin(ret_ang)
    att_cos, att_sin = _axial_tables(seq)

    w_in = ret_w_in[0].astype(BF16)
    w_out = ret_w_out[0].astype(BF16)
    w1 = mlp_w1[0].astype(BF16)
    w2 = mlp_w2[0].astype(BF16)
    k_cols = (qk_w, 2 * qk_w)
    k_scale = ret_dk ** -0.5
    p_ctx = _proj_call(xc, norm_mix_g[0], mod4, 0, w_in, tm=t_ctx, tn=1024, k_cols=k_cols, k_scale=k_scale,
                       name="ret_proj_ctx", **cx)
    p_lat = _proj_call(xl, norm_mix_g[0], mod4, 0, w_in, tm=1024, tn=1024, rope="half", cos=ret_cos,
                       sin=ret_sin, rope_cols=2 * qk_w, k_cols=k_cols, k_scale=k_scale,
                       name="ret_proj_lat", **lat)
    zeros = jnp.zeros((batch, RET_HEADS, ret_dk, ret_dv), F32)
    y_ctx, sc_f, sc_b = _ret_call(p_ctx, ret_decay_fwd[0], ret_decay_bwd[0], zeros, zeros,
                                  batch=batch, seq=n_ctx, ch=min(256, n_ctx), name="retention_ctx")
    y_lat, _, _ = _ret_call(p_lat, ret_decay_fwd[0], ret_decay_bwd[0], sc_f, sc_b,
                            batch=batch, seq=seq, ch=256, name="retention_lat")
    xl = _out_call(y_lat, w_out, xl, mod4, 0, tm=1024, tn=512, name="ret_out_lat", **lat)
    xc = _out_call(y_ctx, w_out, xc, mod4, 0, tm=t_ctx, tn=512, name="ret_out_ctx", **cx)
    xl = _mlp_call(xl, norm_mlp_g[0], mod4, 0, w1, w2, tm=512, tf=1024, name="mlp0_lat", **lat)
    xc = _mlp_call(xc, norm_mlp_g[0], mod4, 0, w1, w2, tm=512, tf=1024, name="mlp0_ctx", **cx)

    w_in = attn_w_in[0].astype(BF16)
    w_out = attn_w_out[0].astype(BF16)
    w1 = mlp_w1[1].astype(BF16)
    w2 = mlp_w2[1].astype(BF16)
    kv_ctx = _proj_call(xc, norm_mix_g[1], mod4, 1, w_in, tm=t_ctx, tn=512, n_off=att_qw // 512,
                        n_cols=2 * att_kvw, name="attn_proj_ctx", **cx)
    qkv = _proj_call(xl, norm_mix_g[1], mod4, 1, w_in, tm=1024, tn=512, rope="axial", cos=att_cos,
                     sin=att_sin, rope_cols=att_qw + att_kvw, name="attn_proj_lat", **lat)
    o = _attn_call(qkv, kv_ctx, attn_sink[0], batch=batch, seq=seq, n_ctx=n_ctx, tq=128)
    xl = _out_call(o, w_out, xl, mod4, 1, tm=1024, tn=512, name="attn_out_lat", **lat)
    xl = _mlp_call(xl, norm_mlp_g[1], mod4, 1, w1, w2, tm=512, tf=1024, final_g=final_norm_g,
                   name="mlp1_lat", **lat)
    return xl.reshape(batch, seq, d)
```

```python
import functools

import jax
import jax.numpy as jnp
from jax import lax
from jax.experimental import pallas as pl
from jax.experimental.pallas import tpu as pltpu

F32 = jnp.float32
BF16 = jnp.bfloat16

NORM_EPS = 1e-6
NEG_INF = -1e30
GRID_W = 64
ROPE_BASE = 10000.0

RET_HEADS = 8
ATT_HEAD_DIM = 128
ATT_KV_HEADS = 4
WINDOW = 128

MOD_ROWS = 8
VMEM_LIMIT = 56 * 1024 * 1024


def _params(semantics, vmem=VMEM_LIMIT):
    return pltpu.CompilerParams(dimension_semantics=semantics, vmem_limit_bytes=vmem)


def _norm_mod(x, g, shift, scale):
    ms = jnp.mean(x * x, axis=-1, keepdims=True)
    y = x * lax.rsqrt(ms + NORM_EPS) * g
    return y * (1.0 + scale) + shift


def _ada_kernel(cond_ref, w_ref, b_ref, o_ref):
    c = cond_ref[...]
    s = c * jax.nn.sigmoid(c)
    o_ref[...] = jnp.dot(s.astype(BF16), w_ref[...].astype(BF16),
                         preferred_element_type=F32) + b_ref[...]


def _ada_call(cond, ada_w, ada_b, tn=1024):
    depth, d, n = ada_w.shape
    return pl.pallas_call(
        _ada_kernel,
        grid=(depth, n // tn),
        in_specs=[
            pl.BlockSpec((MOD_ROWS, d), lambda l, j: (0, 0)),
            pl.BlockSpec((None, d, tn), lambda l, j: (l, 0, j)),
            pl.BlockSpec((None, 1, tn), lambda l, j: (l, 0, j)),
        ],
        out_specs=pl.BlockSpec((None, MOD_ROWS, tn), lambda l, j: (l, 0, j)),
        out_shape=jax.ShapeDtypeStruct((depth, MOD_ROWS, n), F32),
        compiler_params=_params(("arbitrary", "arbitrary")),
        name="ada_mod",
    )(cond, ada_w, ada_b.reshape(depth, 1, n))


def _swap32(x):
    lane = lax.broadcasted_iota(jnp.int32, x.shape, 1)
    return jnp.where((lane & 32) == 0, pltpu.roll(x, 96, 1), pltpu.roll(x, 32, 1))


def _proj_kernel(*refs, rope):
    if rope is None:
        x_ref, g_ref, shift_ref, scale_ref, w_ref, o_ref, a_ref = refs
        cos_ref = sin_ref = None
    else:
        x_ref, g_ref, shift_ref, scale_ref, w_ref, cos_ref, sin_ref, o_ref, a_ref = refs

    @pl.when(pl.program_id(1) == 0)
    def _():
        a = _norm_mod(x_ref[...], g_ref[...], shift_ref[...], scale_ref[...])
        a_ref[...] = a.astype(BF16)

    acc = jnp.dot(a_ref[...], w_ref[...], preferred_element_type=F32)
    tn = acc.shape[1]
    if rope is None:
        o_ref[...] = acc.astype(o_ref.dtype)
        return
    cos = cos_ref[...]
    sin = sin_ref[...]
    if rope == "half":
        for hh in range(tn // 256):
            x1 = acc[:, hh * 256:hh * 256 + 128]
            x2 = acc[:, hh * 256 + 128:(hh + 1) * 256]
            o_ref[:, hh * 256:hh * 256 + 128] = (x1 * cos - x2 * sin).astype(o_ref.dtype)
            o_ref[:, hh * 256 + 128:(hh + 1) * 256] = (x2 * cos + x1 * sin).astype(o_ref.dtype)
    else:
        for hh in range(tn // 128):
            xh = acc[:, hh * 128:(hh + 1) * 128]
            o_ref[:, hh * 128:(hh + 1) * 128] = (xh * cos + _swap32(xh) * sin).astype(o_ref.dtype)


def _proj_call(x2d, g, mod4, layer, w, *, tm, tn, rows_per_mod, mod_row0, n_off=0, n_cols=None,
               rope=None, cos=None, sin=None, tab_fn=None, name="proj"):
    t, d = x2d.shape
    n_cols = w.shape[1] if n_cols is None else n_cols
    nt = n_cols // tn
    tiles_per_mod = rows_per_mod // tm

    def mod_spec(k):
        return pl.BlockSpec((None, None, 1, d),
                            lambda i, j: (layer, mod_row0 + i // tiles_per_mod, 0, k))

    in_specs = [
        pl.BlockSpec((tm, d), lambda i, j: (i, 0)),
        pl.BlockSpec((1, d), lambda i, j: (0, 0)),
        mod_spec(0), mod_spec(1),
        pl.BlockSpec((d, tn), lambda i, j: (0, n_off + j)),
    ]
    args = [x2d, g.reshape(1, d), mod4, mod4, w]
    if rope is not None:
        in_specs += [pl.BlockSpec((tm, 128), lambda i, j: (tab_fn(i, j), 0))] * 2
        args += [cos, sin]
    return pl.pallas_call(
        functools.partial(_proj_kernel, rope=rope),
        grid=(t // tm, nt),
        in_specs=in_specs,
        out_specs=pl.BlockSpec((tm, tn), lambda i, j: (i, j)),
        out_shape=jax.ShapeDtypeStruct((t, n_cols), BF16),
        scratch_shapes=[pltpu.VMEM((tm, d), BF16)],
        compiler_params=_params(("parallel", "arbitrary")),
        name=name,
    )(*args)


def _out_kernel(y_ref, w_ref, x_ref, gate_ref, o_ref):
    acc = jnp.dot(y_ref[...], w_ref[...], preferred_element_type=F32)
    o_ref[...] = x_ref[...] + gate_ref[...] * acc


def _out_call(y, w, x2d, mod4, layer, *, tm, tn, rows_per_mod, mod_row0, name="out_proj"):
    t, k = y.shape
    d = w.shape[1]
    tiles_per_mod = rows_per_mod // tm
    gate_blk0 = 2 * d // tn
    return pl.pallas_call(
        _out_kernel,
        grid=(t // tm, d // tn),
        in_specs=[
            pl.BlockSpec((tm, k), lambda i, j: (i, 0)),
            pl.BlockSpec((k, tn), lambda i, j: (0, j)),
            pl.BlockSpec((tm, tn), lambda i, j: (i, j)),
            pl.BlockSpec((None, None, 1, tn),
                         lambda i, j: (layer, mod_row0 + i // tiles_per_mod, 0, gate_blk0 + j)),
        ],
        out_specs=pl.BlockSpec((tm, tn), lambda i, j: (i, j)),
        out_shape=jax.ShapeDtypeStruct((t, d), F32),
        compiler_params=_params(("parallel", "arbitrary")),
        name=name,
    )(y, w, x2d, mod4)


def _mlp_kernel(*refs, nf, final):
    if final:
        x_ref, g_ref, shift_ref, scale_ref, gate_ref, w1_ref, w2_ref, fg_ref, o_ref, a_ref = refs
    else:
        x_ref, g_ref, shift_ref, scale_ref, gate_ref, w1_ref, w2_ref, o_ref, a_ref = refs
        fg_ref = None
    f = pl.program_id(1)

    @pl.when(f == 0)
    def _():
        a = _norm_mod(x_ref[...], g_ref[...], shift_ref[...], scale_ref[...])
        a_ref[...] = a.astype(BF16)
        o_ref[...] = jnp.zeros_like(o_ref)

    h = jnp.dot(a_ref[...], w1_ref[...], preferred_element_type=F32)
    h = jnp.square(jnp.maximum(h, 0.0)).astype(BF16)
    o_ref[...] += jnp.dot(h, w2_ref[...], preferred_element_type=F32)

    @pl.when(f == nf - 1)
    def _():
        r = x_ref[...] + gate_ref[...] * o_ref[...]
        if final:
            ms = jnp.mean(r * r, axis=-1, keepdims=True)
            r = r * lax.rsqrt(ms + NORM_EPS) * fg_ref[...]
        o_ref[...] = r


def _mlp_call(x2d, g, mod4, layer, w1, w2, *, tm, tf, rows_per_mod, mod_row0, final_g=None, name="mlp"):
    t, d = x2d.shape
    ff = w1.shape[1]
    nf = ff // tf
    tiles_per_mod = rows_per_mod // tm

    def mod_spec(k):
        return pl.BlockSpec((None, None, 1, d),
                            lambda i, f: (layer, mod_row0 + i // tiles_per_mod, 0, k))

    in_specs = [
        pl.BlockSpec((tm, d), lambda i, f: (i, 0)),
        pl.BlockSpec((1, d), lambda i, f: (0, 0)),
        mod_spec(3), mod_spec(4), mod_spec(5),
        pl.BlockSpec((d, tf), lambda i, f: (0, f)),
        pl.BlockSpec((tf, d), lambda i, f: (f, 0)),
    ]
    args = [x2d, g.reshape(1, d), mod4, mod4, mod4, w1, w2]
    if final_g is not None:
        in_specs.append(pl.BlockSpec((1, d), lambda i, f: (0, 0)))
        args.append(final_g.reshape(1, d))
    return pl.pallas_call(
        functools.partial(_mlp_kernel, nf=nf, final=final_g is not None),
        grid=(t // tm, nf),
        in_specs=in_specs,
        out_specs=pl.BlockSpec((tm, d), lambda i, f: (i, 0)),
        out_shape=jax.ShapeDtypeStruct((t, d), F32),
        scratch_shapes=[pltpu.VMEM((tm, d), BF16)],
        compiler_params=_params(("parallel", "arbitrary")),
        name=name,
    )(*args)


def _log_sigmoid(x):
    return jnp.minimum(x, 0.0) - jnp.log1p(jnp.exp(-jnp.abs(x)))


def _ret_kernel(df_ref, db_ref, q_ref, k_ref, v_ref, g_ref, s0f_ref, s0b_ref,
                y_ref, sff_ref, sfb_ref, of_ref, sf_ref, sb_ref, *, seq, ch, unroll):
    h = pl.program_id(1)
    n = seq // ch
    lgf = _log_sigmoid(jnp.full((ch, ch), df_ref[h], F32))
    lgb = _log_sigmoid(jnp.full((ch, ch), db_ref[h], F32))
    ii = lax.broadcasted_iota(jnp.int32, (ch, ch), 0)
    jj = lax.broadcasted_iota(jnp.int32, (ch, ch), 1)
    diff = (ii - jj).astype(F32)
    decay = (jnp.where(diff >= 0, jnp.exp(lgf * jnp.maximum(diff, 0.0)), 0.0)
             + jnp.where(diff <= 0, jnp.exp(lgb * jnp.maximum(-diff, 0.0)), 0.0))
    col = lax.broadcasted_iota(jnp.int32, (ch, 1), 0).astype(F32)
    lgf_c = _log_sigmoid(jnp.full((ch, 1), df_ref[h], F32))
    lgb_c = _log_sigmoid(jnp.full((ch, 1), db_ref[h], F32))
    dq_f = jnp.exp(lgf_c * (col + 1.0))
    dk_f = jnp.exp(lgf_c * (ch - 1.0 - col))
    dc_f = jnp.exp(_log_sigmoid(jnp.full((1, 1), df_ref[h], F32)) * float(ch))
    dq_b = jnp.exp(lgb_c * (ch - col))
    dk_b = jnp.exp(lgb_c * col)
    dc_b = jnp.exp(_log_sigmoid(jnp.full((1, 1), db_ref[h], F32)) * float(ch))
    tdims = (((0,), (0,)), ((), ()))
    ndims = (((1,), (1,)), ((), ()))

    sf_ref[...] = s0f_ref[...]

    def fwd(c, carry):
        r = pl.ds(pl.multiple_of(c * ch, ch), ch)
        q = q_ref[r, :]
        k = k_ref[r, :]
        v = v_ref[r, :]
        s = sf_ref[...]
        of_ref[r, :] = jnp.dot(q, s.astype(BF16), preferred_element_type=F32) * dq_f
        kd = (k.astype(F32) * dk_f).astype(BF16)
        sf_ref[...] = s * dc_f + lax.dot_general(kd, v, tdims, preferred_element_type=F32)
        return carry

    lax.fori_loop(0, n, fwd, 0, unroll=min(unroll, n))
    sff_ref[...] = sf_ref[...]
    sb_ref[...] = s0b_ref[...]

    def bwd(t, carry):
        c = n - 1 - t
        r = pl.ds(pl.multiple_of(c * ch, ch), ch)
        q = q_ref[r, :]
        k = k_ref[r, :]
        v = v_ref[r, :]
        s = sb_ref[...]
        scores = lax.dot_general(q, k, ndims, preferred_element_type=F32) * decay
        o = (jnp.dot(scores.astype(BF16), v, preferred_element_type=F32)
             + jnp.dot(q, s.astype(BF16), preferred_element_type=F32) * dq_b
             + of_ref[r, :])
        y = o * lax.rsqrt(jnp.mean(o * o, axis=-1, keepdims=True) + NORM_EPS)
        g = g_ref[r, :].astype(F32)
        y_ref[r, :] = (g * jax.nn.sigmoid(g) * y).astype(y_ref.dtype)
        kd = (k.astype(F32) * dk_b).astype(BF16)
        sb_ref[...] = s * dc_b + lax.dot_general(kd, v, tdims, preferred_element_type=F32)
        return carry

    lax.fori_loop(0, n, bwd, 0, unroll=min(unroll, n))
    sfb_ref[...] = sb_ref[...]


def _ret_call(qkvg, decay_f, decay_b, s0f, s0b, *, batch, seq, ch, name="retention"):
    heads = RET_HEADS
    dk = s0f.shape[2]
    dv = s0f.shape[3]
    smem = pl.BlockSpec(memory_space=pltpu.SMEM)
    state = pl.BlockSpec((None, None, dk, dv), lambda b, h: (b, h, 0, 0))
    return pl.pallas_call(
        functools.partial(_ret_kernel, seq=seq, ch=ch, unroll=2),
        grid=(batch, heads),
        in_specs=[
            smem, smem,
            pl.BlockSpec((seq, dk), lambda b, h: (b, h)),
            pl.BlockSpec((seq, dk), lambda b, h: (b, heads + h)),
            pl.BlockSpec((seq, dv), lambda b, h: (b, heads + h)),
            pl.BlockSpec((seq, dv), lambda b, h: (b, 2 * heads + h)),
            state, state,
        ],
        out_specs=[pl.BlockSpec((seq, dv), lambda b, h: (b, h)), state, state],
        out_shape=[
            jax.ShapeDtypeStruct((batch * seq, heads * dv), BF16),
            jax.ShapeDtypeStruct(s0f.shape, F32),
            jax.ShapeDtypeStruct(s0f.shape, F32),
        ],
        scratch_shapes=[
            pltpu.VMEM((seq, dv), F32),
            pltpu.VMEM((dk, dv), F32),
            pltpu.VMEM((dk, dv), F32),
        ],
        compiler_params=_params(("parallel", "arbitrary")),
        name=name,
    )(decay_f, decay_b, qkvg, qkvg, qkvg, qkvg, s0f, s0b)


def _attn_kernel(sink_ref, q_ref, kp_ref, kc_ref, kn_ref, vp_ref, vc_ref, vn_ref, kx_ref, vx_ref,
                 o_ref, *, nq, tq, group):
    n = pl.program_id(1)
    dh = ATT_HEAD_DIM
    n_ctx = kx_ref.shape[0]
    win = tq + 2 * WINDOW
    rows = group * tq
    nk = win + n_ctx
    scale = dh ** -0.5
    r = lax.broadcasted_iota(jnp.int32, (rows, nk), 0)
    c = lax.broadcasted_iota(jnp.int32, (rows, nk), 1)
    qpos = r % tq
    kpos = c - WINDOW
    ok = ((jnp.abs(kpos - qpos) <= WINDOW)
          & ((kpos >= 0) | (n > 0))
          & ((kpos < tq) | (n < nq - 1)))
    ok = ok | (c >= win)
    rcol = lax.broadcasted_iota(jnp.int32, (rows, 1), 0) // tq
    ndims = (((1,), (1,)), ((), ()))

    for kh in range(ATT_KV_HEADS):
        hs = slice(kh * dh, (kh + 1) * dh)
        qs = jnp.concatenate(
            [q_ref[:, (kh * group + g) * dh:(kh * group + g + 1) * dh] for g in range(group)], axis=0)
        k_all = jnp.concatenate([kp_ref[:, hs], kc_ref[:, hs], kn_ref[:, hs], kx_ref[:, hs]], axis=0)
        v_all = jnp.concatenate([vp_ref[:, hs], vc_ref[:, hs], vn_ref[:, hs], vx_ref[:, hs]], axis=0)
        s = lax.dot_general(qs, k_all, ndims, preferred_element_type=F32) * scale
        s = jnp.where(ok, s, NEG_INF)
        sink = jnp.zeros((rows, 1), F32)
        for g in range(group):
            sink = jnp.where(rcol == g, sink_ref[kh * group + g], sink)
        m = jnp.maximum(jnp.max(s, axis=-1, keepdims=True), sink)
        p = jnp.exp(s - m)
        denom = jnp.sum(p, axis=-1, keepdims=True) + jnp.exp(sink - m)
        o = jnp.dot(p.astype(BF16), v_all, preferred_element_type=F32) / denom
        for g in range(group):
            o_ref[:, (kh * group + g) * dh:(kh * group + g + 1) * dh] = (
                o[g * tq:(g + 1) * tq, :].astype(o_ref.dtype))


def _attn_call(qkv, kv_ctx, sink, *, batch, seq, n_ctx, tq, name="window_attn"):
    dh = ATT_HEAD_DIM
    kvw = ATT_KV_HEADS * dh
    qw = qkv.shape[1] - 2 * kvw
    group = qw // kvw
    nq = seq // tq
    wpb = seq // WINDOW
    step = tq // WINDOW
    k_blk = qw // kvw
    v_blk = k_blk + 1

    def prev_map(col):
        return lambda b, n: (b * wpb + jnp.maximum(n * step - 1, 0), col)

    def cur_map(col):
        return lambda b, n: (b * nq + n, col)

    def next_map(col):
        return lambda b, n: (b * wpb + jnp.minimum((n + 1) * step, wpb - 1), col)

    return pl.pallas_call(
        functools.partial(_attn_kernel, nq=nq, tq=tq, group=group),
        grid=(batch, nq),
        in_specs=[
            pl.BlockSpec(memory_space=pltpu.SMEM),
            pl.BlockSpec((tq, qw), lambda b, n: (b * nq + n, 0)),
            pl.BlockSpec((WINDOW, kvw), prev_map(k_blk)),
            pl.BlockSpec((tq, kvw), cur_map(k_blk)),
            pl.BlockSpec((WINDOW, kvw), next_map(k_blk)),
            pl.BlockSpec((WINDOW, kvw), prev_map(v_blk)),
            pl.BlockSpec((tq, kvw), cur_map(v_blk)),
            pl.BlockSpec((WINDOW, kvw), next_map(v_blk)),
            pl.BlockSpec((n_ctx, kvw), lambda b, n: (b, 0)),
            pl.BlockSpec((n_ctx, kvw), lambda b, n: (b, 1)),
        ],
        out_specs=pl.BlockSpec((tq, qw), lambda b, n: (b * nq + n, 0)),
        out_shape=jax.ShapeDtypeStruct((batch * seq, qw), BF16),
        compiler_params=_params(("parallel", "arbitrary")),
        name=name,
    )(sink, qkv, qkv, qkv, qkv, qkv, qkv, qkv, kv_ctx, kv_ctx)


def _rope_angles(pos, dim, base):
    inv_freq = base ** (-jnp.arange(0, dim, 2, dtype=F32) / dim)
    return pos.astype(F32)[:, None] * inv_freq[None, :]


def _axial_tables(seq):
    rows = jnp.arange(seq) // GRID_W
    cols = jnp.arange(seq) % GRID_W
    half = ATT_HEAD_DIM // 2
    ar = _rope_angles(rows, half, ROPE_BASE)
    ac = _rope_angles(cols, half, ROPE_BASE)
    cos = jnp.concatenate([jnp.cos(ar), jnp.cos(ar), jnp.cos(ac), jnp.cos(ac)], axis=-1)
    sin = jnp.concatenate([-jnp.sin(ar), jnp.sin(ar), -jnp.sin(ac), jnp.sin(ac)], axis=-1)
    return cos, sin


def kernel(x, c, ctx, c_ctx, ada_w, ada_b, norm_mix_g, norm_mlp_g, mlp_w1, mlp_w2, ret_w_in, ret_w_out,
           ret_decay_fwd, ret_decay_bwd, attn_w_in, attn_w_out, attn_sink, final_norm_g):
    batch, seq, d = x.shape
    n_ctx = ctx.shape[1]
    assert ada_w.shape[0] == 2 and ret_w_in.shape[0] == 1 and attn_w_in.shape[0] == 1
    assert batch + 1 <= MOD_ROWS
    ret_dk = d // RET_HEADS
    ret_dv = 2 * ret_dk
    qk_w = RET_HEADS * ret_dk
    att_qw = attn_w_out.shape[1]
    att_kvw = ATT_KV_HEADS * ATT_HEAD_DIM
    ctx_row = batch

    cond = jnp.concatenate([c, c_ctx[None, :], jnp.zeros((MOD_ROWS - batch - 1, d), F32)], axis=0)
    mod = _ada_call(cond, ada_w, ada_b)
    mod4 = mod.reshape(mod.shape[0], MOD_ROWS, 1, mod.shape[2])

    xl = x.reshape(batch * seq, d)
    xc = ctx.reshape(batch * n_ctx, d)
    t_ctx = batch * n_ctx
    lat = dict(rows_per_mod=seq, mod_row0=0)
    cx = dict(rows_per_mod=t_ctx, mod_row0=ctx_row)

    ret_ang = _rope_angles(jnp.arange(seq), ret_dk, ROPE_BASE)
    ret_cos, ret_sin = jnp.cos(ret_ang), jnp.sin(ret_ang)
    att_cos, att_sin = _axial_tables(seq)

    w_in = ret_w_in[0].astype(BF16)
    w_out = ret_w_out[0].astype(BF16)
    w1 = mlp_w1[0].astype(BF16)
    w2 = mlp_w2[0].astype(BF16)
    tm, tn = 1024, 1024
    k_scale = ret_dk ** -0.5
    q_tiles = qk_w // tn
    tps = seq // tm
    one = jnp.ones((tm, 128), F32)
    zero = jnp.zeros((tm, 128), F32)

    def ret_tab_lat(i, j):
        return jnp.where(j < q_tiles, i % tps, jnp.where(j < 2 * q_tiles, tps + i % tps, 2 * tps))

    def ret_tab_ctx(i, j):
        return jnp.where((j >= q_tiles) & (j < 2 * q_tiles), 1, 0)

    assert t_ctx == tm
    p_ctx = _proj_call(xc, norm_mix_g[0], mod4, 0, w_in, tm=tm, tn=tn, rope="half",
                       cos=jnp.concatenate([one, one * k_scale]), sin=jnp.concatenate([zero, zero]),
                       tab_fn=ret_tab_ctx, name="ret_proj_ctx", **cx)
    p_lat = _proj_call(xl, norm_mix_g[0], mod4, 0, w_in, tm=tm, tn=tn, rope="half",
                       cos=jnp.concatenate([ret_cos, ret_cos * k_scale, one]),
                       sin=jnp.concatenate([ret_sin, ret_sin * k_scale, zero]),
                       tab_fn=ret_tab_lat, name="ret_proj_lat", **lat)
    zeros = jnp.zeros((batch, RET_HEADS, ret_dk, ret_dv), F32)
    y_ctx, sc_f, sc_b = _ret_call(p_ctx, ret_decay_fwd[0], ret_decay_bwd[0], zeros, zeros,
                                  batch=batch, seq=n_ctx, ch=min(256, n_ctx), name="retention_ctx")
    y_lat, _, _ = _ret_call(p_lat, ret_decay_fwd[0], ret_decay_bwd[0], sc_f, sc_b,
                            batch=batch, seq=seq, ch=256, name="retention_lat")
    xl = _out_call(y_lat, w_out, xl, mod4, 0, tm=1024, tn=512, name="ret_out_lat", **lat)
    xc = _out_call(y_ctx, w_out, xc, mod4, 0, tm=t_ctx, tn=512, name="ret_out_ctx", **cx)
    xl = _mlp_call(xl, norm_mlp_g[0], mod4, 0, w1, w2, tm=512, tf=1024, name="mlp0_lat", **lat)
    xc = _mlp_call(xc, norm_mlp_g[0], mod4, 0, w1, w2, tm=512, tf=1024, name="mlp0_ctx", **cx)

    w_in = attn_w_in[0].astype(BF16)
    w_out = attn_w_out[0].astype(BF16)
    w1 = mlp_w1[1].astype(BF16)
    w2 = mlp_w2[1].astype(BF16)
    kv_ctx = _proj_call(xc, norm_mix_g[1], mod4, 1, w_in, tm=t_ctx, tn=512, n_off=att_qw // 512,
                        n_cols=2 * att_kvw, name="attn_proj_ctx", **cx)
    rope_tiles = (att_qw + att_kvw) // 512

    def att_tab(i, j):
        return jnp.where(j < rope_tiles, i % tps, tps)

    qkv = _proj_call(xl, norm_mix_g[1], mod4, 1, w_in, tm=tm, tn=512, rope="axial",
                     cos=jnp.concatenate([att_cos, one]), sin=jnp.concatenate([att_sin, zero]),
                     tab_fn=att_tab, name="attn_proj_lat", **lat)
    o = _attn_call(qkv, kv_ctx, attn_sink[0], batch=batch, seq=seq, n_ctx=n_ctx, tq=128)
    xl = _out_call(o, w_out, xl, mod4, 1, tm=1024, tn=512, name="attn_out_lat", **lat)
    xl = _mlp_call(xl, norm_mlp_g[1], mod4, 1, w1, w2, tm=512, tf=1024, final_g=final_norm_g,
                   name="mlp1_lat", **lat)
    return xl.reshape(batch, seq, d)
```

```python
import functools

import jax
import jax.numpy as jnp
from jax import lax
from jax.experimental import pallas as pl
from jax.experimental.pallas import tpu as pltpu

F32 = jnp.float32
BF16 = jnp.bfloat16

NORM_EPS = 1e-6
NEG_INF = -1e30
GRID_W = 64
ROPE_BASE = 10000.0

RET_HEADS = 8
ATT_HEAD_DIM = 128
ATT_KV_HEADS = 4
WINDOW = 128

MOD_ROWS = 8
VMEM_LIMIT = 56 * 1024 * 1024


def _params(semantics, vmem=VMEM_LIMIT):
    return pltpu.CompilerParams(dimension_semantics=semantics, vmem_limit_bytes=vmem)


def _norm_mod(x, g, shift, scale):
    ms = jnp.mean(x * x, axis=-1, keepdims=True)
    y = x * lax.rsqrt(ms + NORM_EPS) * g
    return y * (1.0 + scale) + shift


def _ada_kernel(cond_ref, w_ref, b_ref, o_ref):
    c = cond_ref[...]
    s = c * jax.nn.sigmoid(c)
    o_ref[...] = jnp.dot(s.astype(BF16), w_ref[...].astype(BF16),
                         preferred_element_type=F32) + b_ref[...]


def _ada_call(cond, ada_w, ada_b, tn=1024):
    depth, d, n = ada_w.shape
    return pl.pallas_call(
        _ada_kernel,
        grid=(depth, n // tn),
        in_specs=[
            pl.BlockSpec((MOD_ROWS, d), lambda l, j: (0, 0)),
            pl.BlockSpec((None, d, tn), lambda l, j: (l, 0, j)),
            pl.BlockSpec((None, 1, tn), lambda l, j: (l, 0, j)),
        ],
        out_specs=pl.BlockSpec((None, MOD_ROWS, tn), lambda l, j: (l, 0, j)),
        out_shape=jax.ShapeDtypeStruct((depth, MOD_ROWS, n), F32),
        compiler_params=_params(("arbitrary", "arbitrary")),
        name="ada_mod",
    )(cond, ada_w, ada_b.reshape(depth, 1, n))


def _swap32(x):
    lane = lax.broadcasted_iota(jnp.int32, x.shape, 1)
    return jnp.where((lane & 32) == 0, pltpu.roll(x, 96, 1), pltpu.roll(x, 32, 1))


def _cast_plumbing(jobs, n_steps, step_of):
    in_specs, args, out_specs, out_shapes = [], [], [], []
    for arr, layer, rows in jobs:
        r, c = arr.shape[-2:]
        nblk = r // rows
        assert n_steps >= nblk and r % rows == 0
        rep = n_steps // nblk

        def blk(i, j, rep=rep, nblk=nblk):
            return jnp.minimum(step_of(i, j) // rep, nblk - 1)

        in_specs.append(pl.BlockSpec((None, rows, c), lambda i, j, blk=blk, layer=layer: (layer, blk(i, j), 0)))
        args.append(arr)
        out_specs.append(pl.BlockSpec((rows, c), lambda i, j, blk=blk: (blk(i, j), 0)))
        out_shapes.append(jax.ShapeDtypeStruct((r, c), BF16))
    return in_specs, args, out_specs, out_shapes


def _rotate_store(acc, o_ref, rope, cos_ref, sin_ref):
    tn = acc.shape[1]
    if rope is None:
        o_ref[...] = acc.astype(o_ref.dtype)
        return
    cos = cos_ref[...]
    sin = sin_ref[...]
    if rope == "half":
        for hh in range(tn // 256):
            x1 = acc[:, hh * 256:hh * 256 + 128]
            x2 = acc[:, hh * 256 + 128:(hh + 1) * 256]
            o_ref[:, hh * 256:hh * 256 + 128] = (x1 * cos - x2 * sin).astype(o_ref.dtype)
            o_ref[:, hh * 256 + 128:(hh + 1) * 256] = (x2 * cos + x1 * sin).astype(o_ref.dtype)
    else:
        for hh in range(tn // 128):
            xh = acc[:, hh * 128:(hh + 1) * 128]
            o_ref[:, hh * 128:(hh + 1) * 128] = (xh * cos + _swap32(xh) * sin).astype(o_ref.dtype)


def _proj_kernel(*refs, rope, n_cast, emit_w, ns, nm, mod_row0, tiles_per_mod):
    refs = list(refs)
    xs_ref = refs.pop(0) if ns else None
    x_ref, g_ref, shift_ref, scale_ref, w_ref = refs[:5]
    refs = refs[5:]
    cos_ref = sin_ref = None
    if rope is not None:
        cos_ref, sin_ref = refs[:2]
        refs = refs[2:]
    cast_src, refs = refs[:n_cast], refs[n_cast:]
    o_ref = refs.pop(0)
    wout_ref = refs.pop(0) if emit_w else None
    cast_dst, a_refs = refs[:n_cast], refs[n_cast:]
    i = pl.program_id(0)
    j = pl.program_id(1)

    def prologue(x, tile):
        b = mod_row0 + tile // tiles_per_mod
        return _norm_mod(x, g_ref[...], shift_ref[b], scale_ref[b]).astype(BF16)

    def body(a_cur, a_nxt):
        w = w_ref[...]
        if emit_w:
            w = w.astype(BF16)
            wout_ref[...] = w
        for src, dst in zip(cast_src, cast_dst):
            dst[...] = src[...].astype(BF16)
        if a_nxt is not None:
            rs = xs_ref.shape[0]
            rows = pl.ds(pl.multiple_of(jnp.minimum(j, ns - 1) * rs, rs), rs)
            a_nxt[rows, :] = prologue(xs_ref[...], jnp.minimum(i + 1, nm - 1))
        acc = jnp.dot(a_cur[...], w, preferred_element_type=F32)
        _rotate_store(acc, o_ref, rope, cos_ref, sin_ref)

    if ns == 0:
        @pl.when(j == 0)
        def _():
            a_refs[0][...] = prologue(x_ref[...], i)

        body(a_refs[0], None)
        return

    @pl.when((i == 0) & (j == 0))
    def _():
        a_refs[0][...] = prologue(x_ref[...], 0)

    @pl.when(i % 2 == 0)
    def _():
        body(a_refs[0], a_refs[1])

    @pl.when(i % 2 == 1)
    def _():
        body(a_refs[1], a_refs[0])


def _proj_call(x2d, g, mod4, layer, w, *, tm, tn, rows_per_mod, mod_row0, n_off=0, n_cols=None,
               rope=None, cos=None, sin=None, tab_fn=None, ns=0, emit_w=False, casts=(), name="proj"):
    t, d = x2d.shape
    n_cols = w.shape[-1] if n_cols is None else n_cols
    nt = n_cols // tn
    nm = t // tm
    assert t % tm == 0 and n_cols % tn == 0 and rows_per_mod % tm == 0
    assert not emit_w or nm == 1
    assert ns == 0 or (tm % ns == 0 and ns <= nt and nm > 1)

    def mod_spec(k):
        return pl.BlockSpec((None, MOD_ROWS, 1, d), lambda i, j: (layer, 0, 0, k))

    in_specs, args = [], []
    if ns:
        in_specs.append(pl.BlockSpec(
            (tm // ns, d), lambda i, j: (jnp.minimum(i + 1, nm - 1) * ns + jnp.minimum(j, ns - 1), 0)))
        in_specs.append(pl.BlockSpec((tm, d), lambda i, j: (0, 0), pipeline_mode=pl.Buffered(1)))
        args += [x2d, x2d]
    else:
        in_specs.append(pl.BlockSpec((tm, d), lambda i, j: (i, 0)))
        args.append(x2d)
    in_specs += [pl.BlockSpec((1, d), lambda i, j: (0, 0)), mod_spec(0), mod_spec(1)]
    args += [g.reshape(1, d), mod4, mod4]
    if emit_w:
        in_specs.append(pl.BlockSpec((None, d, tn), lambda i, j: (0, 0, n_off + j)))
    else:
        in_specs.append(pl.BlockSpec((d, tn), lambda i, j: (0, n_off + j)))
    args.append(w)
    if rope is not None:
        in_specs += [pl.BlockSpec((tm, 128), lambda i, j: (tab_fn(i, j), 0))] * 2
        args += [cos, sin]
    c_in, c_args, c_out, c_shapes = _cast_plumbing(casts, nm * nt, lambda i, j: i * nt + j)
    out_specs = [pl.BlockSpec((tm, tn), lambda i, j: (i, j))]
    out_shapes = [jax.ShapeDtypeStruct((t, n_cols), BF16)]
    if emit_w:
        out_specs.append(pl.BlockSpec((d, tn), lambda i, j: (0, j)))
        out_shapes.append(jax.ShapeDtypeStruct((d, n_cols), BF16))
    return pl.pallas_call(
        functools.partial(_proj_kernel, rope=rope, n_cast=len(casts), emit_w=emit_w, ns=ns, nm=nm,
                          mod_row0=mod_row0, tiles_per_mod=rows_per_mod // tm),
        grid=(nm, nt),
        in_specs=in_specs + c_in,
        out_specs=out_specs + c_out,
        out_shape=out_shapes + c_shapes,
        scratch_shapes=[pltpu.VMEM((tm, d), BF16)] * (2 if ns else 1),
        compiler_params=_params(("arbitrary", "arbitrary")),
        name=name,
    )(*args, *c_args)


def _out_kernel(y_ref, w_ref, x_ref, gate_ref, o_ref):
    acc = jnp.dot(y_ref[...], w_ref[...], preferred_element_type=F32)
    o_ref[...] = x_ref[...] + gate_ref[...] * acc


def _out_call(y, w, x2d, mod4, layer, *, tm, tn, rows_per_mod, mod_row0, name="out_proj"):
    t, k = y.shape
    d = w.shape[1]
    tiles_per_mod = rows_per_mod // tm
    gate_blk0 = 2 * d // tn
    return pl.pallas_call(
        _out_kernel,
        grid=(t // tm, d // tn),
        in_specs=[
            pl.BlockSpec((tm, k), lambda i, j: (i, 0)),
            pl.BlockSpec((k, tn), lambda i, j: (0, j)),
            pl.BlockSpec((tm, tn), lambda i, j: (i, j)),
            pl.BlockSpec((None, None, 1, tn),
                         lambda i, j: (layer, mod_row0 + i // tiles_per_mod, 0, gate_blk0 + j)),
        ],
        out_specs=pl.BlockSpec((tm, tn), lambda i, j: (i, j)),
        out_shape=jax.ShapeDtypeStruct((t, d), F32),
        compiler_params=_params(("parallel", "arbitrary")),
        name=name,
    )(y, w, x2d, mod4)


def _mlp_kernel(*refs, nf, final, n_cast, ns, nm, mod_row0, tiles_per_mod):
    refs = list(refs)
    xs_ref = refs.pop(0) if ns else None
    x_ref, g_ref, shift_ref, scale_ref, gate_ref, w1_ref, w2_ref = refs[:7]
    refs = refs[7:]
    fg_ref = refs.pop(0) if final else None
    cast_src, refs = refs[:n_cast], refs[n_cast:]
    o_ref = refs.pop(0)
    cast_dst, a_refs = refs[:n_cast], refs[n_cast:]
    i = pl.program_id(0)
    f = pl.program_id(1)

    def mod_row(tile):
        return mod_row0 + tile // tiles_per_mod

    def prologue(x, tile):
        b = mod_row(tile)
        return _norm_mod(x, g_ref[...], shift_ref[b], scale_ref[b]).astype(BF16)

    def body(a_cur, a_nxt):
        @pl.when(f == 0)
        def _():
            o_ref[...] = jnp.zeros_like(o_ref)

        h = jnp.dot(a_cur[...], w1_ref[...], preferred_element_type=F32)
        h = jnp.square(jnp.maximum(h, 0.0)).astype(BF16)
        o_ref[...] += jnp.dot(h, w2_ref[...], preferred_element_type=F32)
        for src, dst in zip(cast_src, cast_dst):
            dst[...] = src[...].astype(BF16)
        if a_nxt is not None:
            rs = xs_ref.shape[0]
            rows = pl.ds(pl.multiple_of(jnp.minimum(f, ns - 1) * rs, rs), rs)
            a_nxt[rows, :] = prologue(xs_ref[...], jnp.minimum(i + 1, nm - 1))

        @pl.when(f == nf - 1)
        def _():
            r = x_ref[...] + gate_ref[mod_row(i)] * o_ref[...]
            if final:
                ms = jnp.mean(r * r, axis=-1, keepdims=True)
                r = r * lax.rsqrt(ms + NORM_EPS) * fg_ref[...]
            o_ref[...] = r

    if ns == 0:
        @pl.when(f == 0)
        def _():
            a_refs[0][...] = prologue(x_ref[...], i)

        body(a_refs[0], None)
        return

    @pl.when((i == 0) & (f == 0))
    def _():
        a_refs[0][...] = prologue(x_ref[...], 0)

    @pl.when(i % 2 == 0)
    def _():
        body(a_refs[0], a_refs[1])

    @pl.when(i % 2 == 1)
    def _():
        body(a_refs[1], a_refs[0])


def _mlp_call(x2d, g, mod4, layer, w1, w2, *, tm, tf, rows_per_mod, mod_row0, final_g=None, ns=0, casts=(),
              name="mlp"):
    t, d = x2d.shape
    ff = w1.shape[1]
    nf = ff // tf
    nm = t // tm
    assert t % tm == 0 and ff % tf == 0 and rows_per_mod % tm == 0
    assert ns == 0 or (tm % ns == 0 and ns <= nf and nm > 1)

    def mod_spec(k):
        return pl.BlockSpec((None, MOD_ROWS, 1, d), lambda i, f: (layer, 0, 0, k))

    in_specs, args = [], []
    if ns:
        in_specs.append(pl.BlockSpec(
            (tm // ns, d), lambda i, f: (jnp.minimum(i + 1, nm - 1) * ns + jnp.minimum(f, ns - 1), 0)))
        args.append(x2d)
    in_specs += [
        pl.BlockSpec((tm, d), lambda i, f: (i, 0)),
        pl.BlockSpec((1, d), lambda i, f: (0, 0)),
        mod_spec(3), mod_spec(4), mod_spec(5),
        pl.BlockSpec((d, tf), lambda i, f: (0, f)),
        pl.BlockSpec((tf, d), lambda i, f: (f, 0)),
    ]
    args += [x2d, g.reshape(1, d), mod4, mod4, mod4, w1, w2]
    if final_g is not None:
        in_specs.append(pl.BlockSpec((1, d), lambda i, f: (0, 0)))
        args.append(final_g.reshape(1, d))
    c_in, c_args, c_out, c_shapes = _cast_plumbing(casts, nm * nf, lambda i, f: i * nf + f)
    return pl.pallas_call(
        functools.partial(_mlp_kernel, nf=nf, final=final_g is not None, n_cast=len(casts), ns=ns, nm=nm,
                          mod_row0=mod_row0, tiles_per_mod=rows_per_mod // tm),
        grid=(nm, nf),
        in_specs=in_specs + c_in,
        out_specs=[pl.BlockSpec((tm, d), lambda i, f: (i, 0))] + c_out,
        out_shape=[jax.ShapeDtypeStruct((t, d), F32)] + c_shapes,
        scratch_shapes=[pltpu.VMEM((tm, d), BF16)] * (2 if ns else 1),
        compiler_params=_params(("arbitrary", "arbitrary")),
        name=name,
    )(*args, *c_args)


def _log_sigmoid(x):
    return jnp.minimum(x, 0.0) - jnp.log1p(jnp.exp(-jnp.abs(x)))


def _ret_kernel(df_ref, db_ref, q_ref, k_ref, v_ref, g_ref, s0f_ref, s0b_ref,
                y_ref, sff_ref, sfb_ref, of_ref, sf_ref, sb_ref, *, seq, ch, unroll):
    h = pl.program_id(1)
    n = seq // ch
    lgf = _log_sigmoid(jnp.full((ch, ch), df_ref[h], F32))
    lgb = _log_sigmoid(jnp.full((ch, ch), db_ref[h], F32))
    ii = lax.broadcasted_iota(jnp.int32, (ch, ch), 0)
    jj = lax.broadcasted_iota(jnp.int32, (ch, ch), 1)
    diff = (ii - jj).astype(F32)
    decay = (jnp.where(diff >= 0, jnp.exp(lgf * jnp.maximum(diff, 0.0)), 0.0)
             + jnp.where(diff <= 0, jnp.exp(lgb * jnp.maximum(-diff, 0.0)), 0.0))
    col = lax.broadcasted_iota(jnp.int32, (ch, 1), 0).astype(F32)
    lgf_c = _log_sigmoid(jnp.full((ch, 1), df_ref[h], F32))
    lgb_c = _log_sigmoid(jnp.full((ch, 1), db_ref[h], F32))
    dq_f = jnp.exp(lgf_c * (col + 1.0))
    dk_f = jnp.exp(lgf_c * (ch - 1.0 - col))
    dc_f = jnp.exp(_log_sigmoid(jnp.full((1, 1), df_ref[h], F32)) * float(ch))
    dq_b = jnp.exp(lgb_c * (ch - col))
    dk_b = jnp.exp(lgb_c * col)
    dc_b = jnp.exp(_log_sigmoid(jnp.full((1, 1), db_ref[h], F32)) * float(ch))
    tdims = (((0,), (0,)), ((), ()))
    ndims = (((1,), (1,)), ((), ()))

    sf_ref[...] = s0f_ref[...]

    def fwd(c, carry):
        r = pl.ds(pl.multiple_of(c * ch, ch), ch)
        q = q_ref[r, :]
        k = k_ref[r, :]
        v = v_ref[r, :]
        s = sf_ref[...]
        of_ref[r, :] = jnp.dot(q, s.astype(BF16), preferred_element_type=F32) * dq_f
        kd = (k.astype(F32) * dk_f).astype(BF16)
        sf_ref[...] = s * dc_f + lax.dot_general(kd, v, tdims, preferred_element_type=F32)
        return carry

    lax.fori_loop(0, n, fwd, 0, unroll=min(unroll, n))
    sff_ref[...] = sf_ref[...]
    sb_ref[...] = s0b_ref[...]

    def bwd(t, carry):
        c = n - 1 - t
        r = pl.ds(pl.multiple_of(c * ch, ch), ch)
        q = q_ref[r, :]
        k = k_ref[r, :]
        v = v_ref[r, :]
        s = sb_ref[...]
        scores = lax.dot_general(q, k, ndims, preferred_element_type=F32) * decay
        o = (jnp.dot(scores.astype(BF16), v, preferred_element_type=F32)
             + jnp.dot(q, s.astype(BF16), preferred_element_type=F32) * dq_b
             + of_ref[r, :])
        y = o * lax.rsqrt(jnp.mean(o * o, axis=-1, keepdims=True) + NORM_EPS)
        g = g_ref[r, :].astype(F32)
        y_ref[r, :] = (g * jax.nn.sigmoid(g) * y).astype(y_ref.dtype)
        kd = (k.astype(F32) * dk_b).astype(BF16)
        sb_ref[...] = s * dc_b + lax.dot_general(kd, v, tdims, preferred_element_type=F32)
        return carry

    lax.fori_loop(0, n, bwd, 0, unroll=min(unroll, n))
    sfb_ref[...] = sb_ref[...]


def _ret_call(qkvg, decay_f, decay_b, s0f, s0b, *, batch, seq, ch, name="retention"):
    heads = RET_HEADS
    dk = s0f.shape[2]
    dv = s0f.shape[3]
    smem = pl.BlockSpec(memory_space=pltpu.SMEM)
    state = pl.BlockSpec((None, None, dk, dv), lambda b, h: (b, h, 0, 0))
    return pl.pallas_call(
        functools.partial(_ret_kernel, seq=seq, ch=ch, unroll=2),
        grid=(batch, heads),
        in_specs=[
            smem, smem,
            pl.BlockSpec((seq, dk), lambda b, h: (b, h)),
            pl.BlockSpec((seq, dk), lambda b, h: (b, heads + h)),
            pl.BlockSpec((seq, dv), lambda b, h: (b, heads + h)),
            pl.BlockSpec((seq, dv), lambda b, h: (b, 2 * heads + h)),
            state, state,
        ],
        out_specs=[pl.BlockSpec((seq, dv), lambda b, h: (b, h)), state, state],
        out_shape=[
            jax.ShapeDtypeStruct((batch * seq, heads * dv), BF16),
            jax.ShapeDtypeStruct(s0f.shape, F32),
            jax.ShapeDtypeStruct(s0f.shape, F32),
        ],
        scratch_shapes=[
            pltpu.VMEM((seq, dv), F32),
            pltpu.VMEM((dk, dv), F32),
            pltpu.VMEM((dk, dv), F32),
        ],
        compiler_params=_params(("parallel", "arbitrary")),
        name=name,
    )(decay_f, decay_b, qkvg, qkvg, qkvg, qkvg, s0f, s0b)


def _attn_kernel(sink_ref, q_ref, kp_ref, kc_ref, kn_ref, vp_ref, vc_ref, vn_ref, kx_ref, vx_ref,
                 o_ref, *, nq, tq, group):
    n = pl.program_id(1)
    dh = ATT_HEAD_DIM
    n_ctx = kx_ref.shape[0]
    win = tq + 2 * WINDOW
    rows = group * tq
    nk = win + n_ctx
    scale = dh ** -0.5
    r = lax.broadcasted_iota(jnp.int32, (rows, nk), 0)
    c = lax.broadcasted_iota(jnp.int32, (rows, nk), 1)
    qpos = r % tq
    kpos = c - WINDOW
    ok = ((jnp.abs(kpos - qpos) <= WINDOW)
          & ((kpos >= 0) | (n > 0))
          & ((kpos < tq) | (n < nq - 1)))
    ok = ok | (c >= win)
    rcol = lax.broadcasted_iota(jnp.int32, (rows, 1), 0) // tq
    ndims = (((1,), (1,)), ((), ()))

    for kh in range(ATT_KV_HEADS):
        hs = slice(kh * dh, (kh + 1) * dh)
        qs = jnp.concatenate(
            [q_ref[:, (kh * group + g) * dh:(kh * group + g + 1) * dh] for g in range(group)], axis=0)
        k_all = jnp.concatenate([kp_ref[:, hs], kc_ref[:, hs], kn_ref[:, hs], kx_ref[:, hs]], axis=0)
        v_all = jnp.concatenate([vp_ref[:, hs], vc_ref[:, hs], vn_ref[:, hs], vx_ref[:, hs]], axis=0)
        s = lax.dot_general(qs, k_all, ndims, preferred_element_type=F32) * scale
        s = jnp.where(ok, s, NEG_INF)
        sink = jnp.zeros((rows, 1), F32)
        for g in range(group):
            sink = jnp.where(rcol == g, sink_ref[kh * group + g], sink)
        m = jnp.maximum(jnp.max(s, axis=-1, keepdims=True), sink)
        p = jnp.exp(s - m)
        denom = jnp.sum(p, axis=-1, keepdims=True) + jnp.exp(sink - m)
        o = jnp.dot(p.astype(BF16), v_all, preferred_element_type=F32) / denom
        for g in range(group):
            o_ref[:, (kh * group + g) * dh:(kh * group + g + 1) * dh] = (
                o[g * tq:(g + 1) * tq, :].astype(o_ref.dtype))


def _attn_call(qkv, kv_ctx, sink, *, batch, seq, n_ctx, tq, name="window_attn"):
    dh = ATT_HEAD_DIM
    kvw = ATT_KV_HEADS * dh
    qw = qkv.shape[1] - 2 * kvw
    group = qw // kvw
    nq = seq // tq
    wpb = seq // WINDOW
    step = tq // WINDOW
    k_blk = qw // kvw
    v_blk = k_blk + 1

    def prev_map(col):
        return lambda b, n: (b * wpb + jnp.maximum(n * step - 1, 0), col)

    def cur_map(col):
        return lambda b, n: (b * nq + n, col)

    def next_map(col):
        return lambda b, n: (b * wpb + jnp.minimum((n + 1) * step, wpb - 1), col)

    return pl.pallas_call(
        functools.partial(_attn_kernel, nq=nq, tq=tq, group=group),
        grid=(batch, nq),
        in_specs=[
            pl.BlockSpec(memory_space=pltpu.SMEM),
            pl.BlockSpec((tq, qw), lambda b, n: (b * nq + n, 0)),
            pl.BlockSpec((WINDOW, kvw), prev_map(k_blk)),
            pl.BlockSpec((tq, kvw), cur_map(k_blk)),
            pl.BlockSpec((WINDOW, kvw), next_map(k_blk)),
            pl.BlockSpec((WINDOW, kvw), prev_map(v_blk)),
            pl.BlockSpec((tq, kvw), cur_map(v_blk)),
            pl.BlockSpec((WINDOW, kvw), next_map(v_blk)),
            pl.BlockSpec((n_ctx, kvw), lambda b, n: (b, 0)),
            pl.BlockSpec((n_ctx, kvw), lambda b, n: (b, 1)),
        ],
        out_specs=pl.BlockSpec((tq, qw), lambda b, n: (b * nq + n, 0)),
        out_shape=jax.ShapeDtypeStruct((batch * seq, qw), BF16),
        compiler_params=_params(("parallel", "arbitrary")),
        name=name,
    )(sink, qkv, qkv, qkv, qkv, qkv, qkv, qkv, kv_ctx, kv_ctx)


def _rope_angles(pos, dim, base):
    inv_freq = base ** (-jnp.arange(0, dim, 2, dtype=F32) / dim)
    return pos.astype(F32)[:, None] * inv_freq[None, :]


def _axial_tables(seq):
    rows = jnp.arange(seq) // GRID_W
    cols = jnp.arange(seq) % GRID_W
    half = ATT_HEAD_DIM // 2
    ar = _rope_angles(rows, half, ROPE_BASE)
    ac = _rope_angles(cols, half, ROPE_BASE)
    cos = jnp.concatenate([jnp.cos(ar), jnp.cos(ar), jnp.cos(ac), jnp.cos(ac)], axis=-1)
    sin = jnp.concatenate([-jnp.sin(ar), jnp.sin(ar), -jnp.sin(ac), jnp.sin(ac)], axis=-1)
    return cos, sin


def kernel(x, c, ctx, c_ctx, ada_w, ada_b, norm_mix_g, norm_mlp_g, mlp_w1, mlp_w2, ret_w_in, ret_w_out,
           ret_decay_fwd, ret_decay_bwd, attn_w_in, attn_w_out, attn_sink, final_norm_g):
    batch, seq, d = x.shape
    n_ctx = ctx.shape[1]
    assert ada_w.shape[0] == 2 and ret_w_in.shape[0] == 1 and attn_w_in.shape[0] == 1
    assert batch + 1 <= MOD_ROWS
    ret_dk = d // RET_HEADS
    ret_dv = 2 * ret_dk
    qk_w = RET_HEADS * ret_dk
    att_qw = attn_w_out.shape[1]
    att_kvw = ATT_KV_HEADS * ATT_HEAD_DIM
    ctx_row = batch

    cond = jnp.concatenate([c, c_ctx[None, :], jnp.zeros((MOD_ROWS - batch - 1, d), F32)], axis=0)
    mod = _ada_call(cond, ada_w, ada_b)
    mod4 = mod.reshape(mod.shape[0], MOD_ROWS, 1, mod.shape[2])

    xl = x.reshape(batch * seq, d)
    xc = ctx.reshape(batch * n_ctx, d)
    t_ctx = batch * n_ctx
    lat = dict(rows_per_mod=seq, mod_row0=0)
    cx = dict(rows_per_mod=t_ctx, mod_row0=ctx_row)

    ret_ang = _rope_angles(jnp.arange(seq), ret_dk, ROPE_BASE)
    ret_cos, ret_sin = jnp.cos(ret_ang), jnp.sin(ret_ang)
    att_cos, att_sin = _axial_tables(seq)

    tm, tn, tn_ctx = 1024, 1024, 512
    k_scale = ret_dk ** -0.5
    tps = seq // tm
    one = jnp.ones((tm, 128), F32)
    zero = jnp.zeros((tm, 128), F32)

    def ret_tab_lat(i, j):
        q_tiles = qk_w // tn
        return jnp.where(j < q_tiles, i % tps, jnp.where(j < 2 * q_tiles, tps + i % tps, 2 * tps))

    def ret_tab_ctx(i, j):
        q_tiles = qk_w // tn_ctx
        return jnp.where((j >= q_tiles) & (j < 2 * q_tiles), 1, 0)

    assert t_ctx == tm
    p_ctx, w_in = _proj_call(xc, norm_mix_g[0], mod4, 0, ret_w_in, tm=tm, tn=tn_ctx, rope="half",
                             cos=jnp.concatenate([one, one * k_scale]), sin=jnp.concatenate([zero, zero]),
                             tab_fn=ret_tab_ctx, emit_w=True, name="ret_proj_ctx", **cx)
    casts = [(ret_w_out, 0, 32), (mlp_w1, 0, 16), (mlp_w2, 0, 64), (attn_w_in, 0, 16), (attn_w_out, 0, 16)]
    p_lat, w_out, w1, w2, att_w_in, att_w_out = _proj_call(
        xl, norm_mix_g[0], mod4, 0, w_in, tm=tm, tn=tn, rope="half",
        cos=jnp.concatenate([ret_cos, ret_cos * k_scale, one]),
        sin=jnp.concatenate([ret_sin, ret_sin * k_scale, zero]),
        tab_fn=ret_tab_lat, ns=8, casts=casts, name="ret_proj_lat", **lat)
    zeros = jnp.zeros((batch, RET_HEADS, ret_dk, ret_dv), F32)
    y_ctx, sc_f, sc_b = _ret_call(p_ctx, ret_decay_fwd[0], ret_decay_bwd[0], zeros, zeros,
                                  batch=batch, seq=n_ctx, ch=min(256, n_ctx), name="retention_ctx")
    y_lat, _, _ = _ret_call(p_lat, ret_decay_fwd[0], ret_decay_bwd[0], sc_f, sc_b,
                            batch=batch, seq=seq, ch=256, name="retention_lat")
    xl = _out_call(y_lat, w_out, xl, mod4, 0, tm=1024, tn=512, name="ret_out_lat", **lat)
    xc = _out_call(y_ctx, w_out, xc, mod4, 0, tm=t_ctx, tn=512, name="ret_out_ctx", **cx)
    xl, w1_next, w2_next = _mlp_call(xl, norm_mlp_g[0], mod4, 0, w1, w2, tm=512, tf=1024, ns=8,
                                     casts=[(mlp_w1, 1, 16), (mlp_w2, 1, 64)], name="mlp0_lat", **lat)
    xc, = _mlp_call(xc, norm_mlp_g[0], mod4, 0, w1, w2, tm=512, tf=1024, name="mlp0_ctx", **cx)

    kv_ctx, = _proj_call(xc, norm_mix_g[1], mod4, 1, att_w_in, tm=t_ctx, tn=512, n_off=att_qw // 512,
                         n_cols=2 * att_kvw, name="attn_proj_ctx", **cx)
    rope_tiles = (att_qw + att_kvw) // 512

    def att_tab(i, j):
        return jnp.where(j < rope_tiles, i % tps, tps)

    qkv, = _proj_call(xl, norm_mix_g[1], mod4, 1, att_w_in, tm=tm, tn=512, rope="axial",
                      cos=jnp.concatenate([att_cos, one]), sin=jnp.concatenate([att_sin, zero]),
                      tab_fn=att_tab, ns=4, name="attn_proj_lat", **lat)
    o = _attn_call(qkv, kv_ctx, attn_sink[0], batch=batch, seq=seq, n_ctx=n_ctx, tq=128)
    xl = _out_call(o, att_w_out, xl, mod4, 1, tm=1024, tn=512, name="attn_out_lat", **lat)
    xl, = _mlp_call(xl, norm_mlp_g[1], mod4, 1, w1_next, w2_next, tm=512, tf=1024, final_g=final_norm_g,
                    ns=8, name="mlp1_lat", **lat)
    return xl.reshape(batch, seq, d)
```

```python
import functools

import jax
import jax.numpy as jnp
from jax import lax
from jax.experimental import pallas as pl
from jax.experimental.pallas import tpu as pltpu

F32 = jnp.float32
BF16 = jnp.bfloat16

NORM_EPS = 1e-6
NEG_INF = -1e30
GRID_W = 64
ROPE_BASE = 10000.0

RET_HEADS = 8
ATT_HEAD_DIM = 128
ATT_KV_HEADS = 4
WINDOW = 128

MOD_ROWS = 8
VMEM_LIMIT = 56 * 1024 * 1024


def _params(semantics, vmem=VMEM_LIMIT):
    return pltpu.CompilerParams(dimension_semantics=semantics, vmem_limit_bytes=vmem)


def _norm_mod(x, g, shift, scale):
    ms = jnp.mean(x * x, axis=-1, keepdims=True)
    y = x * lax.rsqrt(ms + NORM_EPS) * g
    return y * (1.0 + scale) + shift


def _ada_kernel(cond_ref, w_ref, b_ref, o_ref):
    c = cond_ref[...]
    s = c * jax.nn.sigmoid(c)
    o_ref[...] = jnp.dot(s.astype(BF16), w_ref[...].astype(BF16),
                         preferred_element_type=F32) + b_ref[...]


def _ada_call(cond, ada_w, ada_b, tn=1024):
    depth, d, n = ada_w.shape
    return pl.pallas_call(
        _ada_kernel,
        grid=(depth, n // tn),
        in_specs=[
            pl.BlockSpec((MOD_ROWS, d), lambda l, j: (0, 0)),
            pl.BlockSpec((None, d, tn), lambda l, j: (l, 0, j)),
            pl.BlockSpec((None, 1, tn), lambda l, j: (l, 0, j)),
        ],
        out_specs=pl.BlockSpec((None, MOD_ROWS, tn), lambda l, j: (l, 0, j)),
        out_shape=jax.ShapeDtypeStruct((depth, MOD_ROWS, n), F32),
        compiler_params=_params(("arbitrary", "arbitrary")),
        name="ada_mod",
    )(cond, ada_w, ada_b.reshape(depth, 1, n))


def _swap32(x):
    lane = lax.broadcasted_iota(jnp.int32, x.shape, 1)
    return jnp.where((lane & 32) == 0, pltpu.roll(x, 96, 1), pltpu.roll(x, 32, 1))


def _cast_plumbing(jobs, n_steps, step_of):
    in_specs, args, out_specs, out_shapes = [], [], [], []
    for arr, layer, rows in jobs:
        r, c = arr.shape[-2:]
        nblk = r // rows
        assert n_steps >= nblk and r % rows == 0
        rep = n_steps // nblk

        def blk(i, j, rep=rep, nblk=nblk):
            return jnp.minimum(step_of(i, j) // rep, nblk - 1)

        in_specs.append(pl.BlockSpec((None, rows, c), lambda i, j, blk=blk, layer=layer: (layer, blk(i, j), 0)))
        args.append(arr)
        out_specs.append(pl.BlockSpec((rows, c), lambda i, j, blk=blk: (blk(i, j), 0)))
        out_shapes.append(jax.ShapeDtypeStruct((r, c), BF16))
    return in_specs, args, out_specs, out_shapes


def _zero_after(vals):
    if not vals:
        return None
    t = None
    for v in vals:
        u = pltpu.bitcast(v, jnp.uint32)
        for k in range(u.shape[1] // 128):
            blk = u[:, k * 128:(k + 1) * 128]
            for r in range(blk.shape[0] // 8):
                piece = blk[r * 8:(r + 1) * 8, :]
                t = piece if t is None else t | piece
    z = lax.shift_right_logical(lax.shift_right_logical(t, jnp.uint32(16)), jnp.uint32(16))
    return pltpu.bitcast(z, F32)[:1, :]


def _rotate_store(acc, o_ref, rope, tabs, zero=None):
    tn = acc.shape[1]
    if rope is None:
        o_ref[...] = acc.astype(o_ref.dtype)
        return
    hd = 256 if rope == "half" else 128
    heads = tn // hd
    for hh in range(heads):
        cos_ref, sin_ref = tabs[hh * len(tabs) // heads]
        cos = cos_ref[...]
        sin = sin_ref[...]
        if rope == "half":
            x1 = acc[:, hh * 256:hh * 256 + 128]
            x2 = acc[:, hh * 256 + 128:(hh + 1) * 256]
            if zero is not None and hh == 0:
                x1 = x1 + zero
            o_ref[:, hh * 256:hh * 256 + 128] = (x1 * cos - x2 * sin).astype(o_ref.dtype)
            o_ref[:, hh * 256 + 128:(hh + 1) * 256] = (x2 * cos + x1 * sin).astype(o_ref.dtype)
        else:
            xh = acc[:, hh * 128:(hh + 1) * 128]
            if zero is not None and hh == 0:
                xh = xh + zero
            o_ref[:, hh * 128:(hh + 1) * 128] = (xh * cos + _swap32(xh) * sin).astype(o_ref.dtype)


def _proj_kernel(*refs, rope, n_tab, n_cast, emit_w, ns, nm, mod_row0, tiles_per_mod):
    refs = list(refs)
    xs_ref = refs.pop(0) if ns else None
    x_ref, g_ref, shift_ref, scale_ref, w_ref = refs[:5]
    refs = refs[5:]
    tabs = [(refs[2 * k], refs[2 * k + 1]) for k in range(n_tab)]
    refs = refs[2 * n_tab:]
    cast_src, refs = refs[:n_cast], refs[n_cast:]
    o_ref = refs.pop(0)
    wout_ref = refs.pop(0) if emit_w else None
    cast_dst, a_refs = refs[:n_cast], refs[n_cast:]
    i = pl.program_id(0)
    j = pl.program_id(1)

    def prologue(x, tile):
        b = mod_row0 + tile // tiles_per_mod
        return _norm_mod(x, g_ref[...], shift_ref[b], scale_ref[b]).astype(BF16)

    def body(a_cur, a_nxt):
        w = w_ref[...]
        if emit_w:
            w = w.astype(BF16)
            wout_ref[...] = w
        side = []
        for src, dst in zip(cast_src, cast_dst):
            v = src[...].astype(BF16)
            dst[...] = v
            side.append(v)
        if a_nxt is not None:
            rs = xs_ref.shape[0]
            rows = pl.ds(pl.multiple_of(jnp.minimum(j, ns - 1) * rs, rs), rs)
            v = prologue(xs_ref[...], jnp.minimum(i + 1, nm - 1))
            a_nxt[rows, :] = v
            side.append(v)
        acc = jnp.dot(a_cur[...], w, preferred_element_type=F32)
        _rotate_store(acc, o_ref, rope, tabs, _zero_after(side))

    if ns == 0:
        @pl.when(j == 0)
        def _():
            a_refs[0][...] = prologue(x_ref[...], i)

        body(a_refs[0], None)
        return

    @pl.when((i == 0) & (j == 0))
    def _():
        a_refs[0][...] = prologue(x_ref[...], 0)

    @pl.when(i % 2 == 0)
    def _():
        body(a_refs[0], a_refs[1])

    @pl.when(i % 2 == 1)
    def _():
        body(a_refs[1], a_refs[0])


def _proj_call(x2d, g, mod4, layer, w, *, tm, tn, rows_per_mod, mod_row0, n_off=0, n_cols=None,
               rope=None, tables=(), ns=0, emit_w=False, casts=(), name="proj"):
    t, d = x2d.shape
    n_cols = w.shape[-1] if n_cols is None else n_cols
    nt = n_cols // tn
    nm = t // tm
    assert t % tm == 0 and n_cols % tn == 0 and rows_per_mod % tm == 0
    assert not emit_w or nm == 1
    assert ns == 0 or (tm % ns == 0 and ns <= nt and nm > 1)

    def mod_spec(k):
        return pl.BlockSpec((None, MOD_ROWS, 1, d), lambda i, j: (layer, 0, 0, k))

    in_specs, args = [], []
    if ns:
        in_specs.append(pl.BlockSpec(
            (tm // ns, d), lambda i, j: (jnp.minimum(i + 1, nm - 1) * ns + jnp.minimum(j, ns - 1), 0)))
        in_specs.append(pl.BlockSpec((tm, d), lambda i, j: (0, 0), pipeline_mode=pl.Buffered(1)))
        args += [x2d, x2d]
    else:
        in_specs.append(pl.BlockSpec((tm, d), lambda i, j: (i, 0)))
        args.append(x2d)
    in_specs += [pl.BlockSpec((1, d), lambda i, j: (0, 0)), mod_spec(0), mod_spec(1)]
    args += [g.reshape(1, d), mod4, mod4]
    if emit_w:
        in_specs.append(pl.BlockSpec((None, d, tn), lambda i, j: (0, 0, n_off + j)))
    else:
        in_specs.append(pl.BlockSpec((d, tn), lambda i, j: (0, n_off + j)))
    args.append(w)
    for cos, sin, tab_fn in tables:
        in_specs += [pl.BlockSpec((tm, 128), lambda i, j, tab_fn=tab_fn: (tab_fn(i, j), 0))] * 2
        args += [cos, sin]
    c_in, c_args, c_out, c_shapes = _cast_plumbing(casts, nm * nt, lambda i, j: i * nt + j)
    out_specs = [pl.BlockSpec((tm, tn), lambda i, j: (i, j))]
    out_shapes = [jax.ShapeDtypeStruct((t, n_cols), BF16)]
    if emit_w:
        out_specs.append(pl.BlockSpec((d, tn), lambda i, j: (0, j)))
        out_shapes.append(jax.ShapeDtypeStruct((d, n_cols), BF16))
    return pl.pallas_call(
        functools.partial(_proj_kernel, rope=rope, n_tab=len(tables), n_cast=len(casts), emit_w=emit_w, ns=ns, nm=nm,
                          mod_row0=mod_row0, tiles_per_mod=rows_per_mod // tm),
        grid=(nm, nt),
        in_specs=in_specs + c_in,
        out_specs=out_specs + c_out,
        out_shape=out_shapes + c_shapes,
        scratch_shapes=[pltpu.VMEM((tm, d), BF16)] * (2 if ns else 1),
        compiler_params=_params(("arbitrary", "arbitrary")),
        name=name,
    )(*args, *c_args)


def _out_kernel(y_ref, w_ref, x_ref, gate_ref, o_ref):
    acc = jnp.dot(y_ref[...], w_ref[...], preferred_element_type=F32)
    o_ref[...] = x_ref[...] + gate_ref[...] * acc


def _out_call(y, w, x2d, mod4, layer, *, tm, tn, rows_per_mod, mod_row0, name="out_proj"):
    t, k = y.shape
    d = w.shape[1]
    tiles_per_mod = rows_per_mod // tm
    gate_blk0 = 2 * d // tn
    return pl.pallas_call(
        _out_kernel,
        grid=(t // tm, d // tn),
        in_specs=[
            pl.BlockSpec((tm, k), lambda i, j: (i, 0)),
            pl.BlockSpec((k, tn), lambda i, j: (0, j)),
            pl.BlockSpec((tm, tn), lambda i, j: (i, j)),
            pl.BlockSpec((None, None, 1, tn),
                         lambda i, j: (layer, mod_row0 + i // tiles_per_mod, 0, gate_blk0 + j)),
        ],
        out_specs=pl.BlockSpec((tm, tn), lambda i, j: (i, j)),
        out_shape=jax.ShapeDtypeStruct((t, d), F32),
        compiler_params=_params(("parallel", "arbitrary")),
        name=name,
    )(y, w, x2d, mod4)


def _mlp_kernel(*refs, nf, final, n_cast, ns, nm, mod_row0, tiles_per_mod):
    refs = list(refs)
    xs_ref = refs.pop(0) if ns else None
    x_ref, g_ref, shift_ref, scale_ref, gate_ref, w1_ref, w2_ref = refs[:7]
    refs = refs[7:]
    fg_ref = refs.pop(0) if final else None
    cast_src, refs = refs[:n_cast], refs[n_cast:]
    o_ref = refs.pop(0)
    cast_dst, a_refs = refs[:n_cast], refs[n_cast:]
    i = pl.program_id(0)
    f = pl.program_id(1)

    def mod_row(tile):
        return mod_row0 + tile // tiles_per_mod

    def prologue(x, tile):
        b = mod_row(tile)
        return _norm_mod(x, g_ref[...], shift_ref[b], scale_ref[b]).astype(BF16)

    def body(a_cur, a_nxt):
        @pl.when(f == 0)
        def _():
            o_ref[...] = jnp.zeros_like(o_ref)

        side = []
        for src, dst in zip(cast_src, cast_dst):
            v = src[...].astype(BF16)
            dst[...] = v
            side.append(v)
        if a_nxt is not None:
            rs = xs_ref.shape[0]
            rows = pl.ds(pl.multiple_of(jnp.minimum(f, ns - 1) * rs, rs), rs)
            v = prologue(xs_ref[...], jnp.minimum(i + 1, nm - 1))
            a_nxt[rows, :] = v
            side.append(v)
        zero = _zero_after(side)
        h = jnp.dot(a_cur[...], w1_ref[...], preferred_element_type=F32)
        h = jnp.maximum(h, 0.0)
        if zero is not None:
            h = jnp.concatenate([h[:, :-128], h[:, -128:] + zero], axis=1)
        h = jnp.square(h).astype(BF16)
        o_ref[...] += jnp.dot(h, w2_ref[...], preferred_element_type=F32)

        @pl.when(f == nf - 1)
        def _():
            r = x_ref[...] + gate_ref[mod_row(i)] * o_ref[...]
            if final:
                ms = jnp.mean(r * r, axis=-1, keepdims=True)
                r = r * lax.rsqrt(ms + NORM_EPS) * fg_ref[...]
            o_ref[...] = r

    if ns == 0:
        @pl.when(f == 0)
        def _():
            a_refs[0][...] = prologue(x_ref[...], i)

        body(a_refs[0], None)
        return

    @pl.when((i == 0) & (f == 0))
    def _():
        a_refs[0][...] = prologue(x_ref[...], 0)

    @pl.when(i % 2 == 0)
    def _():
        body(a_refs[0], a_refs[1])

    @pl.when(i % 2 == 1)
    def _():
        body(a_refs[1], a_refs[0])


def _mlp_call(x2d, g, mod4, layer, w1, w2, *, tm, tf, rows_per_mod, mod_row0, final_g=None, ns=0, casts=(),
              name="mlp"):
    t, d = x2d.shape
    ff = w1.shape[1]
    nf = ff // tf
    nm = t // tm
    assert t % tm == 0 and ff % tf == 0 and rows_per_mod % tm == 0
    assert ns == 0 or (tm % ns == 0 and ns <= nf and nm > 1)

    def mod_spec(k):
        return pl.BlockSpec((None, MOD_ROWS, 1, d), lambda i, f: (layer, 0, 0, k))

    in_specs, args = [], []
    if ns:
        in_specs.append(pl.BlockSpec(
            (tm // ns, d), lambda i, f: (jnp.minimum(i + 1, nm - 1) * ns + jnp.minimum(f, ns - 1), 0)))
        args.append(x2d)
    in_specs += [
        pl.BlockSpec((tm, d), lambda i, f: (i, 0)),
        pl.BlockSpec((1, d), lambda i, f: (0, 0)),
        mod_spec(3), mod_spec(4), mod_spec(5),
        pl.BlockSpec((d, tf), lambda i, f: (0, f)),
        pl.BlockSpec((tf, d), lambda i, f: (f, 0)),
    ]
    args += [x2d, g.reshape(1, d), mod4, mod4, mod4, w1, w2]
    if final_g is not None:
        in_specs.append(pl.BlockSpec((1, d), lambda i, f: (0, 0)))
        args.append(final_g.reshape(1, d))
    c_in, c_args, c_out, c_shapes = _cast_plumbing(casts, nm * nf, lambda i, f: i * nf + f)
    return pl.pallas_call(
        functools.partial(_mlp_kernel, nf=nf, final=final_g is not None, n_cast=len(casts), ns=ns, nm=nm,
                          mod_row0=mod_row0, tiles_per_mod=rows_per_mod // tm),
        grid=(nm, nf),
        in_specs=in_specs + c_in,
        out_specs=[pl.BlockSpec((tm, d), lambda i, f: (i, 0))] + c_out,
        out_shape=[jax.ShapeDtypeStruct((t, d), F32)] + c_shapes,
        scratch_shapes=[pltpu.VMEM((tm, d), BF16)] * (2 if ns else 1),
        compiler_params=_params(("arbitrary", "arbitrary")),
        name=name,
    )(*args, *c_args)


def _log_sigmoid(x):
    return jnp.minimum(x, 0.0) - jnp.log1p(jnp.exp(-jnp.abs(x)))


def _ret_kernel(df_ref, db_ref, q_ref, k_ref, v_ref, g_ref, s0f_ref, s0b_ref,
                y_ref, sff_ref, sfb_ref, of_ref, sf_ref, sb_ref, *, seq, ch, unroll):
    h = pl.program_id(1)
    n = seq // ch
    lgf = _log_sigmoid(jnp.full((ch, ch), df_ref[h], F32))
    lgb = _log_sigmoid(jnp.full((ch, ch), db_ref[h], F32))
    ii = lax.broadcasted_iota(jnp.int32, (ch, ch), 0)
    jj = lax.broadcasted_iota(jnp.int32, (ch, ch), 1)
    diff = (ii - jj).astype(F32)
    decay = (jnp.where(diff >= 0, jnp.exp(lgf * jnp.maximum(diff, 0.0)), 0.0)
             + jnp.where(diff <= 0, jnp.exp(lgb * jnp.maximum(-diff, 0.0)), 0.0))
    col = lax.broadcasted_iota(jnp.int32, (ch, 1), 0).astype(F32)
    lgf_c = _log_sigmoid(jnp.full((ch, 1), df_ref[h], F32))
    lgb_c = _log_sigmoid(jnp.full((ch, 1), db_ref[h], F32))
    dq_f = jnp.exp(lgf_c * (col + 1.0))
    dk_f = jnp.exp(lgf_c * (ch - 1.0 - col))
    dc_f = jnp.exp(_log_sigmoid(jnp.full((1, 1), df_ref[h], F32)) * float(ch))
    dq_b = jnp.exp(lgb_c * (ch - col))
    dk_b = jnp.exp(lgb_c * col)
    dc_b = jnp.exp(_log_sigmoid(jnp.full((1, 1), db_ref[h], F32)) * float(ch))
    tdims = (((0,), (0,)), ((), ()))
    ndims = (((1,), (1,)), ((), ()))

    sf_ref[...] = s0f_ref[...]

    def fwd(c, carry):
        r = pl.ds(pl.multiple_of(c * ch, ch), ch)
        q = q_ref[r, :]
        k = k_ref[r, :]
        v = v_ref[r, :]
        s = sf_ref[...]
        of_ref[r, :] = jnp.dot(q, s.astype(BF16), preferred_element_type=F32) * dq_f
        kd = (k.astype(F32) * dk_f).astype(BF16)
        sf_ref[...] = s * dc_f + lax.dot_general(kd, v, tdims, preferred_element_type=F32)
        return carry

    lax.fori_loop(0, n, fwd, 0, unroll=min(unroll, n))
    sff_ref[...] = sf_ref[...]
    sb_ref[...] = s0b_ref[...]

    def bwd(t, carry):
        c = n - 1 - t
        r = pl.ds(pl.multiple_of(c * ch, ch), ch)
        q = q_ref[r, :]
        k = k_ref[r, :]
        v = v_ref[r, :]
        s = sb_ref[...]
        scores = lax.dot_general(q, k, ndims, preferred_element_type=F32) * decay
        o = (jnp.dot(scores.astype(BF16), v, preferred_element_type=F32)
             + jnp.dot(q, s.astype(BF16), preferred_element_type=F32) * dq_b
             + of_ref[r, :])
        y = o * lax.rsqrt(jnp.mean(o * o, axis=-1, keepdims=True) + NORM_EPS)
        g = g_ref[r, :].astype(F32)
        y_ref[r, :] = (g * jax.nn.sigmoid(g) * y).astype(y_ref.dtype)
        kd = (k.astype(F32) * dk_b).astype(BF16)
        sb_ref[...] = s * dc_b + lax.dot_general(kd, v, tdims, preferred_element_type=F32)
        return carry

    lax.fori_loop(0, n, bwd, 0, unroll=min(unroll, n))
    sfb_ref[...] = sb_ref[...]


def _ret_call(qkvg, decay_f, decay_b, s0f, s0b, *, batch, seq, ch, name="retention"):
    heads = RET_HEADS
    dk = s0f.shape[2]
    dv = s0f.shape[3]
    smem = pl.BlockSpec(memory_space=pltpu.SMEM)
    state = pl.BlockSpec((None, None, dk, dv), lambda b, h: (b, h, 0, 0))
    return pl.pallas_call(
        functools.partial(_ret_kernel, seq=seq, ch=ch, unroll=2),
        grid=(batch, heads),
        in_specs=[
            smem, smem,
            pl.BlockSpec((seq, dk), lambda b, h: (b, h)),
            pl.BlockSpec((seq, dk), lambda b, h: (b, heads + h)),
            pl.BlockSpec((seq, dv), lambda b, h: (b, heads + h)),
            pl.BlockSpec((seq, dv), lambda b, h: (b, 2 * heads + h)),
            state, state,
        ],
        out_specs=[pl.BlockSpec((seq, dv), lambda b, h: (b, h)), state, state],
        out_shape=[
            jax.ShapeDtypeStruct((batch * seq, heads * dv), BF16),
            jax.ShapeDtypeStruct(s0f.shape, F32),
            jax.ShapeDtypeStruct(s0f.shape, F32),
        ],
        scratch_shapes=[
            pltpu.VMEM((seq, dv), F32),
            pltpu.VMEM((dk, dv), F32),
            pltpu.VMEM((dk, dv), F32),
        ],
        compiler_params=_params(("parallel", "arbitrary")),
        name=name,
    )(decay_f, decay_b, qkvg, qkvg, qkvg, qkvg, s0f, s0b)


def _attn_kernel(sink_ref, q_ref, kp_ref, kc_ref, kn_ref, vp_ref, vc_ref, vn_ref, kx_ref, vx_ref,
                 o_ref, *, nq, tq, group):
    n = pl.program_id(1)
    dh = ATT_HEAD_DIM
    n_ctx = kx_ref.shape[0]
    win = tq + 2 * WINDOW
    rows = group * tq
    nk = win + n_ctx
    scale = dh ** -0.5
    r = lax.broadcasted_iota(jnp.int32, (rows, nk), 0)
    c = lax.broadcasted_iota(jnp.int32, (rows, nk), 1)
    qpos = r % tq
    kpos = c - WINDOW
    ok = ((jnp.abs(kpos - qpos) <= WINDOW)
          & ((kpos >= 0) | (n > 0))
          & ((kpos < tq) | (n < nq - 1)))
    ok = ok | (c >= win)
    rcol = lax.broadcasted_iota(jnp.int32, (rows, 1), 0) // tq
    ndims = (((1,), (1,)), ((), ()))

    for kh in range(ATT_KV_HEADS):
        hs = slice(kh * dh, (kh + 1) * dh)
        qs = jnp.concatenate(
            [q_ref[:, (kh * group + g) * dh:(kh * group + g + 1) * dh] for g in range(group)], axis=0)
        k_all = jnp.concatenate([kp_ref[:, hs], kc_ref[:, hs], kn_ref[:, hs], kx_ref[:, hs]], axis=0)
        v_all = jnp.concatenate([vp_ref[:, hs], vc_ref[:, hs], vn_ref[:, hs], vx_ref[:, hs]], axis=0)
        s = lax.dot_general(qs, k_all, ndims, preferred_element_type=F32) * scale
        s = jnp.where(ok, s, NEG_INF)
        sink = jnp.zeros((rows, 1), F32)
        for g in range(group):
            sink = jnp.where(rcol == g, sink_ref[kh * group + g], sink)
        m = jnp.maximum(jnp.max(s, axis=-1, keepdims=True), sink)
        p = jnp.exp(s - m)
        denom = jnp.sum(p, axis=-1, keepdims=True) + jnp.exp(sink - m)
        o = jnp.dot(p.astype(BF16), v_all, preferred_element_type=F32) / denom
        for g in range(group):
            o_ref[:, (kh * group + g) * dh:(kh * group + g + 1) * dh] = (
                o[g * tq:(g + 1) * tq, :].astype(o_ref.dtype))


def _attn_call(qkv, kv_ctx, sink, *, batch, seq, n_ctx, tq, name="window_attn"):
    dh = ATT_HEAD_DIM
    kvw = ATT_KV_HEADS * dh
    qw = qkv.shape[1] - 2 * kvw
    group = qw // kvw
    nq = seq // tq
    wpb = seq // WINDOW
    step = tq // WINDOW
    k_blk = qw // kvw
    v_blk = k_blk + 1

    def prev_map(col):
        return lambda b, n: (b * wpb + jnp.maximum(n * step - 1, 0), col)

    def cur_map(col):
        return lambda b, n: (b * nq + n, col)

    def next_map(col):
        return lambda b, n: (b * wpb + jnp.minimum((n + 1) * step, wpb - 1), col)

    return pl.pallas_call(
        functools.partial(_attn_kernel, nq=nq, tq=tq, group=group),
        grid=(batch, nq),
        in_specs=[
            pl.BlockSpec(memory_space=pltpu.SMEM),
            pl.BlockSpec((tq, qw), lambda b, n: (b * nq + n, 0)),
            pl.BlockSpec((WINDOW, kvw), prev_map(k_blk)),
            pl.BlockSpec((tq, kvw), cur_map(k_blk)),
            pl.BlockSpec((WINDOW, kvw), next_map(k_blk)),
            pl.BlockSpec((WINDOW, kvw), prev_map(v_blk)),
            pl.BlockSpec((tq, kvw), cur_map(v_blk)),
            pl.BlockSpec((WINDOW, kvw), next_map(v_blk)),
            pl.BlockSpec((n_ctx, kvw), lambda b, n: (b, 0)),
            pl.BlockSpec((n_ctx, kvw), lambda b, n: (b, 1)),
        ],
        out_specs=pl.BlockSpec((tq, qw), lambda b, n: (b * nq + n, 0)),
        out_shape=jax.ShapeDtypeStruct((batch * seq, qw), BF16),
        compiler_params=_params(("parallel", "arbitrary")),
        name=name,
    )(sink, qkv, qkv, qkv, qkv, qkv, qkv, qkv, kv_ctx, kv_ctx)


def _rope_angles(pos, dim, base):
    inv_freq = base ** (-jnp.arange(0, dim, 2, dtype=F32) / dim)
    return pos.astype(F32)[:, None] * inv_freq[None, :]


def _axial_tables(seq):
    rows = jnp.arange(seq) // GRID_W
    cols = jnp.arange(seq) % GRID_W
    half = ATT_HEAD_DIM // 2
    ar = _rope_angles(rows, half, ROPE_BASE)
    ac = _rope_angles(cols, half, ROPE_BASE)
    cos = jnp.concatenate([jnp.cos(ar), jnp.cos(ar), jnp.cos(ac), jnp.cos(ac)], axis=-1)
    sin = jnp.concatenate([-jnp.sin(ar), jnp.sin(ar), -jnp.sin(ac), jnp.sin(ac)], axis=-1)
    return cos, sin


def kernel(x, c, ctx, c_ctx, ada_w, ada_b, norm_mix_g, norm_mlp_g, mlp_w1, mlp_w2, ret_w_in, ret_w_out,
           ret_decay_fwd, ret_decay_bwd, attn_w_in, attn_w_out, attn_sink, final_norm_g):
    batch, seq, d = x.shape
    n_ctx = ctx.shape[1]
    assert ada_w.shape[0] == 2 and ret_w_in.shape[0] == 1 and attn_w_in.shape[0] == 1
    assert batch + 1 <= MOD_ROWS
    ret_dk = d // RET_HEADS
    ret_dv = 2 * ret_dk
    qk_w = RET_HEADS * ret_dk
    att_qw = attn_w_out.shape[1]
    att_kvw = ATT_KV_HEADS * ATT_HEAD_DIM
    ctx_row = batch

    cond = jnp.concatenate([c, c_ctx[None, :], jnp.zeros((MOD_ROWS - batch - 1, d), F32)], axis=0)
    mod = _ada_call(cond, ada_w, ada_b)
    mod4 = mod.reshape(mod.shape[0], MOD_ROWS, 1, mod.shape[2])

    xl = x.reshape(batch * seq, d)
    xc = ctx.reshape(batch * n_ctx, d)
    t_ctx = batch * n_ctx
    lat = dict(rows_per_mod=seq, mod_row0=0)
    cx = dict(rows_per_mod=t_ctx, mod_row0=ctx_row)

    ret_ang = _rope_angles(jnp.arange(seq), ret_dk, ROPE_BASE)
    ret_cos, ret_sin = jnp.cos(ret_ang), jnp.sin(ret_ang)
    att_cos, att_sin = _axial_tables(seq)

    tm, tn, tn_ctx = 1024, 1024, 512
    k_scale = ret_dk ** -0.5
    tps = seq // tm
    one = jnp.ones((tm, 128), F32)
    zero = jnp.zeros((tm, 128), F32)

    def ret_tab_lat(i, j):
        q_tiles = qk_w // tn
        return jnp.where(j < q_tiles, i % tps, jnp.where(j < 2 * q_tiles, tps + i % tps, 2 * tps))

    def ret_tab_ctx(i, j):
        q_tiles = qk_w // tn_ctx
        return jnp.where((j >= q_tiles) & (j < 2 * q_tiles), 1, 0)

    assert t_ctx == tm
    ctx_tab = (jnp.concatenate([one, one * k_scale]), jnp.concatenate([zero, zero]), ret_tab_ctx)
    p_ctx, w_in = _proj_call(xc, norm_mix_g[0], mod4, 0, ret_w_in, tm=tm, tn=tn_ctx, rope="half",
                             tables=[ctx_tab], emit_w=True, name="ret_proj_ctx", **cx)
    casts = [(ret_w_out, 0, 32), (mlp_w1, 0, 16), (mlp_w2, 0, 64), (attn_w_in, 0, 16), (attn_w_out, 0, 16)]
    lat_tab = (jnp.concatenate([ret_cos, ret_cos * k_scale, one]),
               jnp.concatenate([ret_sin, ret_sin * k_scale, zero]), ret_tab_lat)
    p_lat, w_out, w1, w2, att_w_in, att_w_out = _proj_call(
        xl, norm_mix_g[0], mod4, 0, w_in, tm=tm, tn=tn, rope="half", tables=[lat_tab],
        ns=8, casts=casts, name="ret_proj_lat", **lat)
    zeros = jnp.zeros((batch, RET_HEADS, ret_dk, ret_dv), F32)
    y_ctx, sc_f, sc_b = _ret_call(p_ctx, ret_decay_fwd[0], ret_decay_bwd[0], zeros, zeros,
                                  batch=batch, seq=n_ctx, ch=min(256, n_ctx), name="retention_ctx")
    y_lat, _, _ = _ret_call(p_lat, ret_decay_fwd[0], ret_decay_bwd[0], sc_f, sc_b,
                            batch=batch, seq=seq, ch=256, name="retention_lat")
    xl = _out_call(y_lat, w_out, xl, mod4, 0, tm=1024, tn=1024, name="ret_out_lat", **lat)
    xc = _out_call(y_ctx, w_out, xc, mod4, 0, tm=t_ctx, tn=512, name="ret_out_ctx", **cx)
    xl, w1_next, w2_next = _mlp_call(xl, norm_mlp_g[0], mod4, 0, w1, w2, tm=512, tf=1024, ns=8,
                                     casts=[(mlp_w1, 1, 16), (mlp_w2, 1, 64)], name="mlp0_lat", **lat)
    xc, = _mlp_call(xc, norm_mlp_g[0], mod4, 0, w1, w2, tm=512, tf=1024, name="mlp0_ctx", **cx)

    kv_ctx, = _proj_call(xc, norm_mix_g[1], mod4, 1, att_w_in, tm=t_ctx, tn=512, n_off=att_qw // 512,
                         n_cols=2 * att_kvw, name="attn_proj_ctx", **cx)
    att_tn = 2 * att_kvw
    assert att_qw % att_tn == 0
    q_tiles = att_qw // att_tn
    att_c = jnp.concatenate([att_cos, one])
    att_s = jnp.concatenate([att_sin, zero])
    qkv, = _proj_call(xl, norm_mix_g[1], mod4, 1, att_w_in, tm=tm, tn=att_tn, rope="axial",
                      tables=[(att_c, att_s, lambda i, j: i % tps),
                              (att_c, att_s, lambda i, j: jnp.where(j < q_tiles, i % tps, tps))],
                      ns=2, name="attn_proj_lat", **lat)
    o = _attn_call(qkv, kv_ctx, attn_sink[0], batch=batch, seq=seq, n_ctx=n_ctx, tq=128)
    xl = _out_call(o, att_w_out, xl, mod4, 1, tm=1024, tn=1024, name="attn_out_lat", **lat)
    xl, = _mlp_call(xl, norm_mlp_g[1], mod4, 1, w1_next, w2_next, tm=512, tf=1024, final_g=final_norm_g,
                    ns=8, name="mlp1_lat", **lat)
    return xl.reshape(batch, seq, d)
```

```python
import functools

import jax
import jax.numpy as jnp
from jax import lax
from jax.experimental import pallas as pl
from jax.experimental.pallas import tpu as pltpu

F32 = jnp.float32
BF16 = jnp.bfloat16

NORM_EPS = 1e-6
NEG_INF = -1e30
GRID_W = 64
ROPE_BASE = 10000.0

RET_HEADS = 8
ATT_HEAD_DIM = 128
ATT_KV_HEADS = 4
WINDOW = 128

MOD_ROWS = 8
VMEM_LIMIT = 56 * 1024 * 1024
X_LEAD = 2


def _params(semantics, vmem=VMEM_LIMIT):
    return pltpu.CompilerParams(dimension_semantics=semantics, vmem_limit_bytes=vmem)


def _norm_mod(x, g, shift, scale):
    ms = jnp.mean(x * x, axis=-1, keepdims=True)
    y = x * lax.rsqrt(ms + NORM_EPS) * g
    return y * (1.0 + scale) + shift


def _ada_kernel(cond_ref, w_ref, b_ref, o_ref):
    c = cond_ref[...]
    s = c * jax.nn.sigmoid(c)
    o_ref[...] = jnp.dot(s.astype(BF16), w_ref[...].astype(BF16),
                         preferred_element_type=F32) + b_ref[...]


def _ada_call(cond, ada_w, ada_b, tn=1024):
    depth, d, n = ada_w.shape
    return pl.pallas_call(
        _ada_kernel,
        grid=(depth, n // tn),
        in_specs=[
            pl.BlockSpec((MOD_ROWS, d), lambda l, j: (0, 0)),
            pl.BlockSpec((None, d, tn), lambda l, j: (l, 0, j)),
            pl.BlockSpec((None, 1, tn), lambda l, j: (l, 0, j)),
        ],
        out_specs=pl.BlockSpec((None, MOD_ROWS, tn), lambda l, j: (l, 0, j)),
        out_shape=jax.ShapeDtypeStruct((depth, MOD_ROWS, n), F32),
        compiler_params=_params(("arbitrary", "arbitrary")),
        name="ada_mod",
    )(cond, ada_w, ada_b.reshape(depth, 1, n))


def _swap32(x):
    lane = lax.broadcasted_iota(jnp.int32, x.shape, 1)
    return jnp.where((lane & 32) == 0, pltpu.roll(x, 96, 1), pltpu.roll(x, 32, 1))


def _cast_plumbing(jobs, n_steps, step_of):
    in_specs, args, out_specs, out_shapes = [], [], [], []
    for arr, layer, rows in jobs:
        r, c = arr.shape[-2:]
        nblk = r // rows
        assert n_steps >= nblk and r % rows == 0
        rep = n_steps // nblk

        def blk(i, j, rep=rep, nblk=nblk):
            return jnp.minimum(step_of(i, j) // rep, nblk - 1)

        in_specs.append(pl.BlockSpec((None, rows, c), lambda i, j, blk=blk, layer=layer: (layer, blk(i, j), 0)))
        args.append(arr)
        out_specs.append(pl.BlockSpec((rows, c), lambda i, j, blk=blk: (blk(i, j), 0)))
        out_shapes.append(jax.ShapeDtypeStruct((r, c), BF16))
    return in_specs, args, out_specs, out_shapes


def _zero_after(vals):
    if not vals:
        return None
    t = None
    for v in vals:
        u = pltpu.bitcast(v, jnp.uint32)
        for k in range(u.shape[1] // 128):
            blk = u[:, k * 128:(k + 1) * 128]
            for r in range(blk.shape[0] // 8):
                piece = blk[r * 8:(r + 1) * 8, :]
                t = piece if t is None else t | piece
    z = lax.shift_right_logical(lax.shift_right_logical(t, jnp.uint32(16)), jnp.uint32(16))
    return pltpu.bitcast(z, F32)[:1, :]


def _rotate_store(acc, o_ref, rope, tabs, zero=None):
    tn = acc.shape[1]
    if rope is None:
        o_ref[...] = acc.astype(o_ref.dtype)
        return
    hd = 256 if rope == "half" else 128
    heads = tn // hd
    for hh in range(heads):
        cos_ref, sin_ref = tabs[hh * len(tabs) // heads]
        cos = cos_ref[...]
        sin = sin_ref[...]
        if rope == "half":
            x1 = acc[:, hh * 256:hh * 256 + 128]
            x2 = acc[:, hh * 256 + 128:(hh + 1) * 256]
            if zero is not None and hh == 0:
                x1 = x1 + zero
            o_ref[:, hh * 256:hh * 256 + 128] = (x1 * cos - x2 * sin).astype(o_ref.dtype)
            o_ref[:, hh * 256 + 128:(hh + 1) * 256] = (x2 * cos + x1 * sin).astype(o_ref.dtype)
        else:
            xh = acc[:, hh * 128:(hh + 1) * 128]
            if zero is not None and hh == 0:
                xh = xh + zero
            o_ref[:, hh * 128:(hh + 1) * 128] = (xh * cos + _swap32(xh) * sin).astype(o_ref.dtype)


def _proj_kernel(*refs, rope, n_tab, n_cast, emit_w, ns, nm, mod_row0, tiles_per_mod):
    refs = list(refs)
    xs_ref = refs.pop(0) if ns else None
    x_ref, g_ref, shift_ref, scale_ref, w_ref = refs[:5]
    refs = refs[5:]
    tabs = [(refs[2 * k], refs[2 * k + 1]) for k in range(n_tab)]
    refs = refs[2 * n_tab:]
    cast_src, refs = refs[:n_cast], refs[n_cast:]
    o_ref = refs.pop(0)
    wout_ref = refs.pop(0) if emit_w else None
    cast_dst, a_refs = refs[:n_cast], refs[n_cast:]
    i = pl.program_id(0)
    j = pl.program_id(1)

    def prologue(x, tile):
        b = mod_row0 + tile // tiles_per_mod
        return _norm_mod(x, g_ref[...], shift_ref[b], scale_ref[b]).astype(BF16)

    def body(a_cur, a_nxt):
        w = w_ref[...]
        if emit_w:
            w = w.astype(BF16)
            wout_ref[...] = w
        side = []
        for src, dst in zip(cast_src, cast_dst):
            v = src[...].astype(BF16)
            dst[...] = v
            side.append(v)
        if a_nxt is not None:
            rs = xs_ref.shape[0]
            rows = pl.ds(pl.multiple_of(jnp.minimum(j, ns - 1) * rs, rs), rs)
            v = prologue(xs_ref[...], jnp.minimum(i + 1, nm - 1))
            a_nxt[rows, :] = v
            side.append(v)
        acc = jnp.dot(a_cur[...], w, preferred_element_type=F32)
        _rotate_store(acc, o_ref, rope, tabs, _zero_after(side))

    if ns == 0:
        @pl.when(j == 0)
        def _():
            a_refs[0][...] = prologue(x_ref[...], i)

        body(a_refs[0], None)
        return

    @pl.when((i == 0) & (j == 0))
    def _():
        a_refs[0][...] = prologue(x_ref[...], 0)

    @pl.when(i % 2 == 0)
    def _():
        body(a_refs[0], a_refs[1])

    @pl.when(i % 2 == 1)
    def _():
        body(a_refs[1], a_refs[0])


def _proj_call(x2d, g, mod4, layer, w, *, tm, tn, rows_per_mod, mod_row0, n_off=0, n_cols=None,
               rope=None, tables=(), ns=0, emit_w=False, casts=(), name="proj"):
    t, d = x2d.shape
    n_cols = w.shape[-1] if n_cols is None else n_cols
    nt = n_cols // tn
    nm = t // tm
    assert t % tm == 0 and n_cols % tn == 0 and rows_per_mod % tm == 0
    assert not emit_w or nm == 1
    assert ns == 0 or (tm % ns == 0 and ns <= nt and nm > 1)

    def mod_spec(k):
        return pl.BlockSpec((None, MOD_ROWS, 1, d), lambda i, j: (layer, 0, 0, k))

    in_specs, args = [], []
    if ns:
        in_specs.append(pl.BlockSpec(
            (tm // ns, d), lambda i, j: (jnp.minimum(i + 1, nm - 1) * ns + jnp.minimum(j, ns - 1), 0)))
        in_specs.append(pl.BlockSpec((tm, d), lambda i, j: (0, 0), pipeline_mode=pl.Buffered(1)))
        args += [x2d, x2d]
    else:
        in_specs.append(pl.BlockSpec((tm, d), lambda i, j: (i, 0)))
        args.append(x2d)
    in_specs += [pl.BlockSpec((1, d), lambda i, j: (0, 0)), mod_spec(0), mod_spec(1)]
    args += [g.reshape(1, d), mod4, mod4]
    if emit_w:
        in_specs.append(pl.BlockSpec((None, d, tn), lambda i, j: (0, 0, n_off + j)))
    else:
        in_specs.append(pl.BlockSpec((d, tn), lambda i, j: (0, n_off + j)))
    args.append(w)
    for cos, sin, tab_fn in tables:
        in_specs += [pl.BlockSpec((tm, 128), lambda i, j, tab_fn=tab_fn: (tab_fn(i, j), 0))] * 2
        args += [cos, sin]
    c_in, c_args, c_out, c_shapes = _cast_plumbing(casts, nm * nt, lambda i, j: i * nt + j)
    out_specs = [pl.BlockSpec((tm, tn), lambda i, j: (i, j))]
    out_shapes = [jax.ShapeDtypeStruct((t, n_cols), BF16)]
    if emit_w:
        out_specs.append(pl.BlockSpec((d, tn), lambda i, j: (0, j)))
        out_shapes.append(jax.ShapeDtypeStruct((d, n_cols), BF16))
    return pl.pallas_call(
        functools.partial(_proj_kernel, rope=rope, n_tab=len(tables), n_cast=len(casts), emit_w=emit_w, ns=ns, nm=nm,
                          mod_row0=mod_row0, tiles_per_mod=rows_per_mod // tm),
        grid=(nm, nt),
        in_specs=in_specs + c_in,
        out_specs=out_specs + c_out,
        out_shape=out_shapes + c_shapes,
        scratch_shapes=[pltpu.VMEM((tm, d), BF16)] * (2 if ns else 1),
        compiler_params=_params(("arbitrary", "arbitrary")),
        name=name,
    )(*args, *c_args)


def _out_kernel(y_ref, w_ref, x_ref, gate_ref, o_ref):
    acc = jnp.dot(y_ref[...], w_ref[...], preferred_element_type=F32)
    o_ref[...] = x_ref[...] + gate_ref[...] * acc


def _out_call(y, w, x2d, mod4, layer, *, tm, tn, rows_per_mod, mod_row0, name="out_proj"):
    t, k = y.shape
    d = w.shape[1]
    tiles_per_mod = rows_per_mod // tm
    gate_blk0 = 2 * d // tn
    return pl.pallas_call(
        _out_kernel,
        grid=(t // tm, d // tn),
        in_specs=[
            pl.BlockSpec((tm, k), lambda i, j: (i, 0)),
            pl.BlockSpec((k, tn), lambda i, j: (0, j)),
            pl.BlockSpec((tm, tn), lambda i, j: (i, j)),
            pl.BlockSpec((None, None, 1, tn),
                         lambda i, j: (layer, mod_row0 + i // tiles_per_mod, 0, gate_blk0 + j)),
        ],
        out_specs=pl.BlockSpec((tm, tn), lambda i, j: (i, j)),
        out_shape=jax.ShapeDtypeStruct((t, d), F32),
        compiler_params=_params(("parallel", "arbitrary")),
        name=name,
    )(y, w, x2d, mod4)


def _mlp_kernel(*refs, nf, final, n_cast, nm, mod_row0, tiles_per_mod):
    refs = list(refs)
    xs_ref, x_hbm, g_ref, shift_ref, scale_ref, gate_ref, w1_ref, w2_ref = refs[:8]
    refs = refs[8:]
    fg_ref = refs.pop(0) if final else None
    cast_src, refs = refs[:n_cast], refs[n_cast:]
    o_ref = refs.pop(0)
    cast_dst, refs = refs[:n_cast], refs[n_cast:]
    a_even, a_odd, x_buf, x_sem = refs
    i = pl.program_id(0)
    f = pl.program_id(1)
    tm = x_buf.shape[0]
    rs = xs_ref.shape[0]

    def x_copy(tile):
        return pltpu.make_async_copy(x_hbm.at[pl.ds(pl.multiple_of(tile * tm, tm), tm), :], x_buf, x_sem)

    def mod_row(tile):
        return mod_row0 + tile // tiles_per_mod

    def prologue(x, tile):
        b = mod_row(tile)
        return _norm_mod(x, g_ref[...], shift_ref[b], scale_ref[b]).astype(BF16)

    @pl.when((i == 0) & (f == 0))
    def _():
        first = x_copy(0)
        first.start()
        first.wait()
        a_even[...] = prologue(x_buf[...], 0)

    @pl.when(f == nf - 1 - X_LEAD)
    def _():
        x_copy(i).start()

    def body(a_cur, a_nxt):
        @pl.when(f == 0)
        def _():
            o_ref[...] = jnp.zeros_like(o_ref)

        side = []
        for src, dst in zip(cast_src, cast_dst):
            v = src[...].astype(BF16)
            dst[...] = v
            side.append(v)
        v = prologue(xs_ref[...], jnp.minimum(i + 1, nm - 1))
        a_nxt[pl.ds(pl.multiple_of(f * rs, rs), rs), :] = v
        side.append(v)
        h = jnp.dot(a_cur[...], w1_ref[...], preferred_element_type=F32)
        h = jnp.maximum(h, 0.0)
        h = jnp.concatenate([h[:, :-128], h[:, -128:] + _zero_after(side)], axis=1)
        h = jnp.square(h).astype(BF16)
        o_ref[...] += jnp.dot(h, w2_ref[...], preferred_element_type=F32)

        @pl.when(f == nf - 1)
        def _():
            x_copy(i).wait()
            r = x_buf[...] + gate_ref[mod_row(i)] * o_ref[...]
            if final:
                ms = jnp.mean(r * r, axis=-1, keepdims=True)
                r = r * lax.rsqrt(ms + NORM_EPS) * fg_ref[...]
            o_ref[...] = r

    @pl.when(i % 2 == 0)
    def _():
        body(a_even, a_odd)

    @pl.when(i % 2 == 1)
    def _():
        body(a_odd, a_even)


def _mlp_call(x2d, g, mod4, layer, w1, w2, *, tm, tf, rows_per_mod, mod_row0, final_g=None, casts=(),
              name="mlp"):
    t, d = x2d.shape
    ff = w1.shape[1]
    nf = ff // tf
    nm = t // tm
    assert t % tm == 0 and ff % tf == 0 and rows_per_mod % tm == 0
    assert tm % nf == 0 and (tm // nf) % 16 == 0 and nf > X_LEAD

    def mod_spec(k):
        return pl.BlockSpec((None, MOD_ROWS, 1, d), lambda i, f: (layer, 0, 0, k))

    in_specs = [
        pl.BlockSpec((tm // nf, d), lambda i, f: (jnp.minimum(i + 1, nm - 1) * nf + f, 0)),
        pl.BlockSpec(memory_space=pl.ANY),
        pl.BlockSpec((1, d), lambda i, f: (0, 0)),
        mod_spec(3), mod_spec(4), mod_spec(5),
        pl.BlockSpec((d, tf), lambda i, f: (0, f)),
        pl.BlockSpec((tf, d), lambda i, f: (f, 0)),
    ]
    args = [x2d, x2d, g.reshape(1, d), mod4, mod4, mod4, w1, w2]
    if final_g is not None:
        in_specs.append(pl.BlockSpec((1, d), lambda i, f: (0, 0)))
        args.append(final_g.reshape(1, d))
    c_in, c_args, c_out, c_shapes = _cast_plumbing(casts, nm * nf, lambda i, f: i * nf + f)
    return pl.pallas_call(
        functools.partial(_mlp_kernel, nf=nf, final=final_g is not None, n_cast=len(casts), nm=nm,
                          mod_row0=mod_row0, tiles_per_mod=rows_per_mod // tm),
        grid=(nm, nf),
        in_specs=in_specs + c_in,
        out_specs=[pl.BlockSpec((tm, d), lambda i, f: (i, 0))] + c_out,
        out_shape=[jax.ShapeDtypeStruct((t, d), F32)] + c_shapes,
        scratch_shapes=[pltpu.VMEM((tm, d), BF16), pltpu.VMEM((tm, d), BF16), pltpu.VMEM((tm, d), F32),
                        pltpu.SemaphoreType.DMA(())],
        compiler_params=_params(("arbitrary", "arbitrary")),
        name=name,
    )(*args, *c_args)


def _log_sigmoid(x):
    return jnp.minimum(x, 0.0) - jnp.log1p(jnp.exp(-jnp.abs(x)))


def _ret_kernel(df_ref, db_ref, q_ref, k_ref, v_ref, g_ref, s0f_ref, s0b_ref,
                y_ref, sff_ref, sfb_ref, of_ref, sf_ref, sb_ref, *, seq, ch, unroll):
    h = pl.program_id(1)
    n = seq // ch
    lgf = _log_sigmoid(jnp.full((ch, ch), df_ref[h], F32))
    lgb = _log_sigmoid(jnp.full((ch, ch), db_ref[h], F32))
    ii = lax.broadcasted_iota(jnp.int32, (ch, ch), 0)
    jj = lax.broadcasted_iota(jnp.int32, (ch, ch), 1)
    diff = (ii - jj).astype(F32)
    decay = (jnp.where(diff >= 0, jnp.exp(lgf * jnp.maximum(diff, 0.0)), 0.0)
             + jnp.where(diff <= 0, jnp.exp(lgb * jnp.maximum(-diff, 0.0)), 0.0))
    col = lax.broadcasted_iota(jnp.int32, (ch, 1), 0).astype(F32)
    lgf_c = _log_sigmoid(jnp.full((ch, 1), df_ref[h], F32))
    lgb_c = _log_sigmoid(jnp.full((ch, 1), db_ref[h], F32))
    dq_f = jnp.exp(lgf_c * (col + 1.0))
    dk_f = jnp.exp(lgf_c * (ch - 1.0 - col))
    dc_f = jnp.exp(_log_sigmoid(jnp.full((1, 1), df_ref[h], F32)) * float(ch))
    dq_b = jnp.exp(lgb_c * (ch - col))
    dk_b = jnp.exp(lgb_c * col)
    dc_b = jnp.exp(_log_sigmoid(jnp.full((1, 1), db_ref[h], F32)) * float(ch))
    tdims = (((0,), (0,)), ((), ()))
    ndims = (((1,), (1,)), ((), ()))

    sf_ref[...] = s0f_ref[...]

    def fwd(c, carry):
        r = pl.ds(pl.multiple_of(c * ch, ch), ch)
        q = q_ref[r, :]
        k = k_ref[r, :]
        v = v_ref[r, :]
        s = sf_ref[...]
        of_ref[r, :] = jnp.dot(q, s.astype(BF16), preferred_element_type=F32) * dq_f
        kd = (k.astype(F32) * dk_f).astype(BF16)
        sf_ref[...] = s * dc_f + lax.dot_general(kd, v, tdims, preferred_element_type=F32)
        return carry

    lax.fori_loop(0, n, fwd, 0, unroll=min(unroll, n))
    sff_ref[...] = sf_ref[...]
    sb_ref[...] = s0b_ref[...]

    def bwd(t, carry):
        c = n - 1 - t
        r = pl.ds(pl.multiple_of(c * ch, ch), ch)
        q = q_ref[r, :]
        k = k_ref[r, :]
        v = v_ref[r, :]
        s = sb_ref[...]
        scores = lax.dot_general(q, k, ndims, preferred_element_type=F32) * decay
        o = (jnp.dot(scores.astype(BF16), v, preferred_element_type=F32)
             + jnp.dot(q, s.astype(BF16), preferred_element_type=F32) * dq_b
             + of_ref[r, :])
        y = o * lax.rsqrt(jnp.mean(o * o, axis=-1, keepdims=True) + NORM_EPS)
        g = g_ref[r, :].astype(F32)
        y_ref[r, :] = (g * jax.nn.sigmoid(g) * y).astype(y_ref.dtype)
        kd = (k.astype(F32) * dk_b).astype(BF16)
        sb_ref[...] = s * dc_b + lax.dot_general(kd, v, tdims, preferred_element_type=F32)
        return carry

    lax.fori_loop(0, n, bwd, 0, unroll=min(unroll, n))
    sfb_ref[...] = sb_ref[...]


def _ret_call(qkvg, decay_f, decay_b, s0f, s0b, *, batch, seq, ch, name="retention"):
    heads = RET_HEADS
    dk = s0f.shape[2]
    dv = s0f.shape[3]
    smem = pl.BlockSpec(memory_space=pltpu.SMEM)
    state = pl.BlockSpec((None, None, dk, dv), lambda b, h: (b, h, 0, 0))
    return pl.pallas_call(
        functools.partial(_ret_kernel, seq=seq, ch=ch, unroll=2),
        grid=(batch, heads),
        in_specs=[
            smem, smem,
            pl.BlockSpec((seq, dk), lambda b, h: (b, h)),
            pl.BlockSpec((seq, dk), lambda b, h: (b, heads + h)),
            pl.BlockSpec((seq, dv), lambda b, h: (b, heads + h)),
            pl.BlockSpec((seq, dv), lambda b, h: (b, 2 * heads + h)),
            state, state,
        ],
        out_specs=[pl.BlockSpec((seq, dv), lambda b, h: (b, h)), state, state],
        out_shape=[
            jax.ShapeDtypeStruct((batch * seq, heads * dv), BF16),
            jax.ShapeDtypeStruct(s0f.shape, F32),
            jax.ShapeDtypeStruct(s0f.shape, F32),
        ],
        scratch_shapes=[
            pltpu.VMEM((seq, dv), F32),
            pltpu.VMEM((dk, dv), F32),
            pltpu.VMEM((dk, dv), F32),
        ],
        compiler_params=_params(("parallel", "arbitrary")),
        name=name,
    )(decay_f, decay_b, qkvg, qkvg, qkvg, qkvg, s0f, s0b)


def _attn_kernel(sink_ref, q_ref, kp_ref, kc_ref, kn_ref, vp_ref, vc_ref, vn_ref, kx_ref, vx_ref,
                 o_ref, *, nq, tq, group):
    n = pl.program_id(1)
    dh = ATT_HEAD_DIM
    n_ctx = kx_ref.shape[0]
    win = tq + 2 * WINDOW
    rows = group * tq
    nk = win + n_ctx
    scale = dh ** -0.5
    assert tq == WINDOW
    rq = lax.broadcasted_iota(jnp.int32, (rows, WINDOW), 0) % tq
    ck = lax.broadcasted_iota(jnp.int32, (rows, WINDOW), 1)
    ok_prev = (ck >= rq) & (n > 0)
    ok_next = (ck <= rq) & (n < nq - 1)
    rcol = lax.broadcasted_iota(jnp.int32, (rows, 1), 0) // tq
    ndims = (((1,), (1,)), ((), ()))

    for kh in range(ATT_KV_HEADS):
        hs = slice(kh * dh, (kh + 1) * dh)
        qs = jnp.concatenate(
            [q_ref[:, (kh * group + g) * dh:(kh * group + g + 1) * dh] for g in range(group)], axis=0)
        k_all = jnp.concatenate([kp_ref[:, hs], kc_ref[:, hs], kn_ref[:, hs], kx_ref[:, hs]], axis=0)
        v_all = jnp.concatenate([vp_ref[:, hs], vc_ref[:, hs], vn_ref[:, hs], vx_ref[:, hs]], axis=0)
        s = lax.dot_general(qs, k_all, ndims, preferred_element_type=F32) * scale
        s = jnp.concatenate([jnp.where(ok_prev, s[:, :WINDOW], NEG_INF), s[:, WINDOW:WINDOW + tq],
                             jnp.where(ok_next, s[:, WINDOW + tq:win], NEG_INF), s[:, win:]], axis=1)
        sink = jnp.zeros((rows, 1), F32)
        for g in range(group):
            sink = jnp.where(rcol == g, sink_ref[kh * group + g], sink)
        m = jnp.maximum(jnp.max(s, axis=-1, keepdims=True), sink)
        p = jnp.exp(s - m)
        denom = jnp.sum(p, axis=-1, keepdims=True) + jnp.exp(sink - m)
        o = jnp.dot(p.astype(BF16), v_all, preferred_element_type=F32) / denom
        for g in range(group):
            o_ref[:, (kh * group + g) * dh:(kh * group + g + 1) * dh] = (
                o[g * tq:(g + 1) * tq, :].astype(o_ref.dtype))


def _attn_call(qkv, kv_ctx, sink, *, batch, seq, n_ctx, tq, name="window_attn"):
    dh = ATT_HEAD_DIM
    kvw = ATT_KV_HEADS * dh
    qw = qkv.shape[1] - 2 * kvw
    group = qw // kvw
    nq = seq // tq
    wpb = seq // WINDOW
    step = tq // WINDOW
    k_blk = qw // kvw
    v_blk = k_blk + 1

    def prev_map(col):
        return lambda b, n: (b * wpb + jnp.maximum(n * step - 1, 0), col)

    def cur_map(col):
        return lambda b, n: (b * nq + n, col)

    def next_map(col):
        return lambda b, n: (b * wpb + jnp.minimum((n + 1) * step, wpb - 1), col)

    return pl.pallas_call(
        functools.partial(_attn_kernel, nq=nq, tq=tq, group=group),
        grid=(batch, nq),
        in_specs=[
            pl.BlockSpec(memory_space=pltpu.SMEM),
            pl.BlockSpec((tq, qw), lambda b, n: (b * nq + n, 0)),
            pl.BlockSpec((WINDOW, kvw), prev_map(k_blk)),
            pl.BlockSpec((tq, kvw), cur_map(k_blk)),
            pl.BlockSpec((WINDOW, kvw), next_map(k_blk)),
            pl.BlockSpec((WINDOW, kvw), prev_map(v_blk)),
            pl.BlockSpec((tq, kvw), cur_map(v_blk)),
            pl.BlockSpec((WINDOW, kvw), next_map(v_blk)),
            pl.BlockSpec((n_ctx, kvw), lambda b, n: (b, 0)),
            pl.BlockSpec((n_ctx, kvw), lambda b, n: (b, 1)),
        ],
        out_specs=pl.BlockSpec((tq, qw), lambda b, n: (b * nq + n, 0)),
        out_shape=jax.ShapeDtypeStruct((batch * seq, qw), BF16),
        compiler_params=_params(("parallel", "arbitrary")),
        name=name,
    )(sink, qkv, qkv, qkv, qkv, qkv, qkv, qkv, kv_ctx, kv_ctx)


def _rope_angles(pos, dim, base):
    inv_freq = base ** (-jnp.arange(0, dim, 2, dtype=F32) / dim)
    return pos.astype(F32)[:, None] * inv_freq[None, :]


def _axial_tables(seq):
    rows = jnp.arange(seq) // GRID_W
    cols = jnp.arange(seq) % GRID_W
    half = ATT_HEAD_DIM // 2
    ar = _rope_angles(rows, half, ROPE_BASE)
    ac = _rope_angles(cols, half, ROPE_BASE)
    cos = jnp.concatenate([jnp.cos(ar), jnp.cos(ar), jnp.cos(ac), jnp.cos(ac)], axis=-1)
    sin = jnp.concatenate([-jnp.sin(ar), jnp.sin(ar), -jnp.sin(ac), jnp.sin(ac)], axis=-1)
    return cos, sin


def kernel(x, c, ctx, c_ctx, ada_w, ada_b, norm_mix_g, norm_mlp_g, mlp_w1, mlp_w2, ret_w_in, ret_w_out,
           ret_decay_fwd, ret_decay_bwd, attn_w_in, attn_w_out, attn_sink, final_norm_g):
    batch, seq, d = x.shape
    n_ctx = ctx.shape[1]
    assert ada_w.shape[0] == 2 and ret_w_in.shape[0] == 1 and attn_w_in.shape[0] == 1
    assert batch + 1 <= MOD_ROWS
    ret_dk = d // RET_HEADS
    ret_dv = 2 * ret_dk
    qk_w = RET_HEADS * ret_dk
    att_qw = attn_w_out.shape[1]
    att_kvw = ATT_KV_HEADS * ATT_HEAD_DIM
    ctx_row = batch

    cond = jnp.concatenate([c, c_ctx[None, :], jnp.zeros((MOD_ROWS - batch - 1, d), F32)], axis=0)
    mod = _ada_call(cond, ada_w, ada_b)
    mod4 = mod.reshape(mod.shape[0], MOD_ROWS, 1, mod.shape[2])

    xl = x.reshape(batch * seq, d)
    xc = ctx.reshape(batch * n_ctx, d)
    t_ctx = batch * n_ctx
    lat = dict(rows_per_mod=seq, mod_row0=0)
    cx = dict(rows_per_mod=t_ctx, mod_row0=ctx_row)

    ret_ang = _rope_angles(jnp.arange(seq), ret_dk, ROPE_BASE)
    ret_cos, ret_sin = jnp.cos(ret_ang), jnp.sin(ret_ang)
    att_cos, att_sin = _axial_tables(seq)

    tm, tn, tn_ctx = 1024, 1024, 512
    k_scale = ret_dk ** -0.5
    tps = seq // tm
    one = jnp.ones((tm, 128), F32)
    zero = jnp.zeros((tm, 128), F32)

    def ret_tab_lat(i, j):
        q_tiles = qk_w // tn
        return jnp.where(j < q_tiles, i % tps, jnp.where(j < 2 * q_tiles, tps + i % tps, 2 * tps))

    def ret_tab_ctx(i, j):
        q_tiles = qk_w // tn_ctx
        return jnp.where((j >= q_tiles) & (j < 2 * q_tiles), 1, 0)

    assert t_ctx == tm
    ctx_tab = (jnp.concatenate([one, one * k_scale]), jnp.concatenate([zero, zero]), ret_tab_ctx)
    p_ctx, w_in = _proj_call(xc, norm_mix_g[0], mod4, 0, ret_w_in, tm=tm, tn=tn_ctx, rope="half",
                             tables=[ctx_tab], emit_w=True, name="ret_proj_ctx", **cx)
    casts = [(ret_w_out, 0, 32), (mlp_w1, 0, 16), (mlp_w2, 0, 64), (attn_w_in, 0, 16), (attn_w_out, 0, 16)]
    lat_tab = (jnp.concatenate([ret_cos, ret_cos * k_scale, one]),
               jnp.concatenate([ret_sin, ret_sin * k_scale, zero]), ret_tab_lat)
    p_lat, w_out, w1, w2, att_w_in, att_w_out = _proj_call(
        xl, norm_mix_g[0], mod4, 0, w_in, tm=tm, tn=tn, rope="half", tables=[lat_tab],
        ns=8, casts=casts, name="ret_proj_lat", **lat)
    zeros = jnp.zeros((batch, RET_HEADS, ret_dk, ret_dv), F32)
    y_ctx, sc_f, sc_b = _ret_call(p_ctx, ret_decay_fwd[0], ret_decay_bwd[0], zeros, zeros,
                                  batch=batch, seq=n_ctx, ch=min(256, n_ctx), name="retention_ctx")
    y_lat, _, _ = _ret_call(p_lat, ret_decay_fwd[0], ret_decay_bwd[0], sc_f, sc_b,
                            batch=batch, seq=seq, ch=256, name="retention_lat")
    xl = _out_call(y_lat, w_out, xl, mod4, 0, tm=1024, tn=1024, name="ret_out_lat", **lat)
    xc = _out_call(y_ctx, w_out, xc, mod4, 0, tm=t_ctx, tn=512, name="ret_out_ctx", **cx)
    xl, w1_next, w2_next = _mlp_call(xl, norm_mlp_g[0], mod4, 0, w1, w2, tm=1024, tf=512,
                                     casts=[(mlp_w1, 1, 16), (mlp_w2, 1, 64)], name="mlp0_lat", **lat)
    xc, = _mlp_call(xc, norm_mlp_g[0], mod4, 0, w1, w2, tm=t_ctx, tf=512, name="mlp0_ctx", **cx)

    kv_ctx, = _proj_call(xc, norm_mix_g[1], mod4, 1, att_w_in, tm=t_ctx, tn=512, n_off=att_qw // 512,
                         n_cols=2 * att_kvw, name="attn_proj_ctx", **cx)
    att_tn = 2 * att_kvw
    assert att_qw % att_tn == 0
    q_tiles = att_qw // att_tn
    att_c = jnp.concatenate([att_cos, one])
    att_s = jnp.concatenate([att_sin, zero])
    qkv, = _proj_call(xl, norm_mix_g[1], mod4, 1, att_w_in, tm=tm, tn=att_tn, rope="axial",
                      tables=[(att_c, att_s, lambda i, j: i % tps),
                              (att_c, att_s, lambda i, j: jnp.where(j < q_tiles, i % tps, tps))],
                      ns=2, name="attn_proj_lat", **lat)
    o = _attn_call(qkv, kv_ctx, attn_sink[0], batch=batch, seq=seq, n_ctx=n_ctx, tq=128)
    xl = _out_call(o, att_w_out, xl, mod4, 1, tm=1024, tn=1024, name="attn_out_lat", **lat)
    xl, = _mlp_call(xl, norm_mlp_g[1], mod4, 1, w1_next, w2_next, tm=1024, tf=512, final_g=final_norm_g,
                    name="mlp1_lat", **lat)
    return xl.reshape(batch, seq, d)
```

```python
import functools

import jax
import jax.numpy as jnp
from jax import lax
from jax.experimental import pallas as pl
from jax.experimental.pallas import tpu as pltpu

F32 = jnp.float32
BF16 = jnp.bfloat16

NORM_EPS = 1e-6
NEG_INF = -1e30
GRID_W = 64
ROPE_BASE = 10000.0

RET_HEADS = 8
ATT_HEAD_DIM = 128
ATT_KV_HEADS = 4
WINDOW = 128

MOD_ROWS = 8
VMEM_LIMIT = 56 * 1024 * 1024


def _params(semantics, vmem=VMEM_LIMIT):
    return pltpu.CompilerParams(dimension_semantics=semantics, vmem_limit_bytes=vmem)


def _norm_mod(x, g, shift, scale):
    ms = jnp.mean(x * x, axis=-1, keepdims=True)
    y = x * lax.rsqrt(ms + NORM_EPS) * g
    return y * (1.0 + scale) + shift


def _ada_kernel(cond_ref, w_ref, b_ref, o_ref):
    c = cond_ref[...]
    s = c * jax.nn.sigmoid(c)
    o_ref[...] = jnp.dot(s.astype(BF16), w_ref[...].astype(BF16),
                         preferred_element_type=F32) + b_ref[...]


def _ada_call(cond, ada_w, ada_b, tn=1024):
    depth, d, n = ada_w.shape
    return pl.pallas_call(
        _ada_kernel,
        grid=(depth, n // tn),
        in_specs=[
            pl.BlockSpec((MOD_ROWS, d), lambda l, j: (0, 0)),
            pl.BlockSpec((None, d, tn), lambda l, j: (l, 0, j)),
            pl.BlockSpec((None, 1, tn), lambda l, j: (l, 0, j)),
        ],
        out_specs=pl.BlockSpec((None, MOD_ROWS, tn), lambda l, j: (l, 0, j)),
        out_shape=jax.ShapeDtypeStruct((depth, MOD_ROWS, n), F32),
        compiler_params=_params(("arbitrary", "arbitrary")),
        name="ada_mod",
    )(cond, ada_w, ada_b.reshape(depth, 1, n))


def _swap32(x):
    lane = lax.broadcasted_iota(jnp.int32, x.shape, 1)
    return jnp.where((lane & 32) == 0, pltpu.roll(x, 96, 1), pltpu.roll(x, 32, 1))


def _cast_plumbing(jobs, n_steps, step_of):
    in_specs, args, out_specs, out_shapes = [], [], [], []
    for arr, layer, rows in jobs:
        r, c = arr.shape[-2:]
        nblk = r // rows
        assert n_steps >= nblk and r % rows == 0
        rep = n_steps // nblk

        def blk(i, j, rep=rep, nblk=nblk):
            return jnp.minimum(step_of(i, j) // rep, nblk - 1)

        in_specs.append(pl.BlockSpec((None, rows, c), lambda i, j, blk=blk, layer=layer: (layer, blk(i, j), 0)))
        args.append(arr)
        out_specs.append(pl.BlockSpec((rows, c), lambda i, j, blk=blk: (blk(i, j), 0)))
        out_shapes.append(jax.ShapeDtypeStruct((r, c), BF16))
    return in_specs, args, out_specs, out_shapes


def _zero_after(vals):
    if not vals:
        return None
    t = None
    for v in vals:
        u = pltpu.bitcast(v, jnp.uint32)
        for k in range(u.shape[1] // 128):
            blk = u[:, k * 128:(k + 1) * 128]
            for r in range(blk.shape[0] // 8):
                piece = blk[r * 8:(r + 1) * 8, :]
                t = piece if t is None else t | piece
    z = lax.shift_right_logical(lax.shift_right_logical(t, jnp.uint32(16)), jnp.uint32(16))
    return pltpu.bitcast(z, F32)[:1, :]


def _rotate_store(acc, o_ref, rope, tabs, zero=None):
    tn = acc.shape[1]
    if rope is None:
        o_ref[...] = acc.astype(o_ref.dtype)
        return
    hd = 256 if rope == "half" else 128
    heads = tn // hd
    for hh in range(heads):
        cos_ref, sin_ref = tabs[hh * len(tabs) // heads]
        cos = cos_ref[...]
        sin = sin_ref[...]
        if rope == "half":
            x1 = acc[:, hh * 256:hh * 256 + 128]
            x2 = acc[:, hh * 256 + 128:(hh + 1) * 256]
            if zero is not None and hh == 0:
                x1 = x1 + zero
            o_ref[:, hh * 256:hh * 256 + 128] = (x1 * cos - x2 * sin).astype(o_ref.dtype)
            o_ref[:, hh * 256 + 128:(hh + 1) * 256] = (x2 * cos + x1 * sin).astype(o_ref.dtype)
        else:
            xh = acc[:, hh * 128:(hh + 1) * 128]
            if zero is not None and hh == 0:
                xh = xh + zero
            o_ref[:, hh * 128:(hh + 1) * 128] = (xh * cos + _swap32(xh) * sin).astype(o_ref.dtype)


def _proj_kernel(*refs, rope, n_tab, n_cast, emit_w, ns, nm, mod_row0, tiles_per_mod):
    refs = list(refs)
    xs_ref = refs.pop(0) if ns else None
    x_ref, g_ref, shift_ref, scale_ref, w_ref = refs[:5]
    refs = refs[5:]
    tabs = [(refs[2 * k], refs[2 * k + 1]) for k in range(n_tab)]
    refs = refs[2 * n_tab:]
    cast_src, refs = refs[:n_cast], refs[n_cast:]
    o_ref = refs.pop(0)
    wout_ref = refs.pop(0) if emit_w else None
    cast_dst, a_refs = refs[:n_cast], refs[n_cast:]
    i = pl.program_id(0)
    j = pl.program_id(1)

    def prologue(x, tile):
        b = mod_row0 + tile // tiles_per_mod
        return _norm_mod(x, g_ref[...], shift_ref[b], scale_ref[b]).astype(BF16)

    def body(a_cur, a_nxt):
        w = w_ref[...]
        if emit_w:
            w = w.astype(BF16)
            wout_ref[...] = w
        side = []
        for src, dst in zip(cast_src, cast_dst):
            v = src[...].astype(BF16)
            dst[...] = v
            side.append(v)
        if a_nxt is not None:
            rs = xs_ref.shape[0]
            rows = pl.ds(pl.multiple_of(jnp.minimum(j, ns - 1) * rs, rs), rs)
            v = prologue(xs_ref[...], jnp.minimum(i + 1, nm - 1))
            a_nxt[rows, :] = v
            side.append(v)
        acc = jnp.dot(a_cur[...], w, preferred_element_type=F32)
        _rotate_store(acc, o_ref, rope, tabs, _zero_after(side))

    if ns == 0:
        @pl.when(j == 0)
        def _():
            a_refs[0][...] = prologue(x_ref[...], i)

        body(a_refs[0], None)
        return

    @pl.when((i == 0) & (j == 0))
    def _():
        a_refs[0][...] = prologue(x_ref[...], 0)

    @pl.when(i % 2 == 0)
    def _():
        body(a_refs[0], a_refs[1])

    @pl.when(i % 2 == 1)
    def _():
        body(a_refs[1], a_refs[0])


def _proj_call(x2d, g, mod4, layer, w, *, tm, tn, rows_per_mod, mod_row0, n_off=0, n_cols=None,
               rope=None, tables=(), ns=0, emit_w=False, casts=(), name="proj"):
    t, d = x2d.shape
    n_cols = w.shape[-1] if n_cols is None else n_cols
    nt = n_cols // tn
    nm = t // tm
    assert t % tm == 0 and n_cols % tn == 0 and rows_per_mod % tm == 0
    assert not emit_w or nm == 1
    assert ns == 0 or (tm % ns == 0 and ns <= nt and nm > 1)

    def mod_spec(k):
        return pl.BlockSpec((None, MOD_ROWS, 1, d), lambda i, j: (layer, 0, 0, k))

    in_specs, args = [], []
    if ns:
        in_specs.append(pl.BlockSpec(
            (tm // ns, d), lambda i, j: (jnp.minimum(i + 1, nm - 1) * ns + jnp.minimum(j, ns - 1), 0)))
        in_specs.append(pl.BlockSpec((tm, d), lambda i, j: (0, 0), pipeline_mode=pl.Buffered(1)))
        args += [x2d, x2d]
    else:
        in_specs.append(pl.BlockSpec((tm, d), lambda i, j: (i, 0)))
        args.append(x2d)
    in_specs += [pl.BlockSpec((1, d), lambda i, j: (0, 0)), mod_spec(0), mod_spec(1)]
    args += [g.reshape(1, d), mod4, mod4]
    if emit_w:
        in_specs.append(pl.BlockSpec((None, d, tn), lambda i, j: (0, 0, n_off + j)))
    else:
        in_specs.append(pl.BlockSpec((d, tn), lambda i, j: (0, n_off + j)))
    args.append(w)
    for cos, sin, tab_fn in tables:
        in_specs += [pl.BlockSpec((tm, 128), lambda i, j, tab_fn=tab_fn: (tab_fn(i, j), 0))] * 2
        args += [cos, sin]
    c_in, c_args, c_out, c_shapes = _cast_plumbing(casts, nm * nt, lambda i, j: i * nt + j)
    out_specs = [pl.BlockSpec((tm, tn), lambda i, j: (i, j))]
    out_shapes = [jax.ShapeDtypeStruct((t, n_cols), BF16)]
    if emit_w:
        out_specs.append(pl.BlockSpec((d, tn), lambda i, j: (0, j)))
        out_shapes.append(jax.ShapeDtypeStruct((d, n_cols), BF16))
    return pl.pallas_call(
        functools.partial(_proj_kernel, rope=rope, n_tab=len(tables), n_cast=len(casts), emit_w=emit_w, ns=ns, nm=nm,
                          mod_row0=mod_row0, tiles_per_mod=rows_per_mod // tm),
        grid=(nm, nt),
        in_specs=in_specs + c_in,
        out_specs=out_specs + c_out,
        out_shape=out_shapes + c_shapes,
        scratch_shapes=[pltpu.VMEM((tm, d), BF16)] * (2 if ns else 1),
        compiler_params=_params(("arbitrary", "arbitrary")),
        name=name,
    )(*args, *c_args)


def _out_kernel(y_ref, w_ref, x_ref, gate_ref, o_ref):
    acc = jnp.dot(y_ref[...], w_ref[...], preferred_element_type=F32)
    o_ref[...] = x_ref[...] + gate_ref[...] * acc


def _out_call(y, w, x2d, mod4, layer, *, tm, tn, rows_per_mod, mod_row0, name="out_proj"):
    t, k = y.shape
    d = w.shape[1]
    tiles_per_mod = rows_per_mod // tm
    gate_blk0 = 2 * d // tn
    return pl.pallas_call(
        _out_kernel,
        grid=(t // tm, d // tn),
        in_specs=[
            pl.BlockSpec((tm, k), lambda i, j: (i, 0)),
            pl.BlockSpec((k, tn), lambda i, j: (0, j)),
            pl.BlockSpec((tm, tn), lambda i, j: (i, j)),
            pl.BlockSpec((None, None, 1, tn),
                         lambda i, j: (layer, mod_row0 + i // tiles_per_mod, 0, gate_blk0 + j)),
        ],
        out_specs=pl.BlockSpec((tm, tn), lambda i, j: (i, j)),
        out_shape=jax.ShapeDtypeStruct((t, d), F32),
        compiler_params=_params(("parallel", "arbitrary")),
        name=name,
    )(y, w, x2d, mod4)


def _mlp_kernel(*refs, nf, final, n_cast, ns, nm, mod_row0, tiles_per_mod):
    refs = list(refs)
    xs_ref = refs.pop(0) if ns else None
    x_ref, g_ref, shift_ref, scale_ref, gate_ref, w1_ref, w2_ref = refs[:7]
    refs = refs[7:]
    fg_ref = refs.pop(0) if final else None
    cast_src, refs = refs[:n_cast], refs[n_cast:]
    o_ref = refs.pop(0)
    cast_dst, a_refs = refs[:n_cast], refs[n_cast:]
    i = pl.program_id(0)
    f = pl.program_id(1)

    def mod_row(tile):
        return mod_row0 + tile // tiles_per_mod

    def prologue(x, tile):
        b = mod_row(tile)
        return _norm_mod(x, g_ref[...], shift_ref[b], scale_ref[b]).astype(BF16)

    def body(a_cur, a_nxt):
        @pl.when(f == 0)
        def _():
            o_ref[...] = jnp.zeros_like(o_ref)

        side = []
        for src, dst in zip(cast_src, cast_dst):
            v = src[...].astype(BF16)
            dst[...] = v
            side.append(v)
        if a_nxt is not None:
            rs = xs_ref.shape[0]
            rows = pl.ds(pl.multiple_of(jnp.minimum(f, ns - 1) * rs, rs), rs)
            v = prologue(xs_ref[...], jnp.minimum(i + 1, nm - 1))
            a_nxt[rows, :] = v
            side.append(v)
        zero = _zero_after(side)
        h = jnp.dot(a_cur[...], w1_ref[...], preferred_element_type=F32)
        h = jnp.maximum(h, 0.0)
        if zero is not None:
            h = jnp.concatenate([h[:, :-128], h[:, -128:] + zero], axis=1)
        h = jnp.square(h).astype(BF16)
        o_ref[...] += jnp.dot(h, w2_ref[...], preferred_element_type=F32)

        @pl.when(f == nf - 1)
        def _():
            r = x_ref[...] + gate_ref[mod_row(i)] * o_ref[...]
            if final:
                ms = jnp.mean(r * r, axis=-1, keepdims=True)
                r = r * lax.rsqrt(ms + NORM_EPS) * fg_ref[...]
            o_ref[...] = r

    if ns == 0:
        @pl.when(f == 0)
        def _():
            a_refs[0][...] = prologue(x_ref[...], i)

        body(a_refs[0], None)
        return

    @pl.when((i == 0) & (f == 0))
    def _():
        a_refs[0][...] = prologue(x_ref[...], 0)

    @pl.when(i % 2 == 0)
    def _():
        body(a_refs[0], a_refs[1])

    @pl.when(i % 2 == 1)
    def _():
        body(a_refs[1], a_refs[0])


def _mlp_call(x2d, g, mod4, layer, w1, w2, *, tm, tf, rows_per_mod, mod_row0, final_g=None, ns=0, casts=(),
              name="mlp"):
    t, d = x2d.shape
    ff = w1.shape[1]
    nf = ff // tf
    nm = t // tm
    assert t % tm == 0 and ff % tf == 0 and rows_per_mod % tm == 0
    assert ns == 0 or (tm % ns == 0 and ns <= nf and nm > 1)

    def mod_spec(k):
        return pl.BlockSpec((None, MOD_ROWS, 1, d), lambda i, f: (layer, 0, 0, k))

    in_specs, args = [], []
    if ns:
        in_specs.append(pl.BlockSpec(
            (tm // ns, d), lambda i, f: (jnp.minimum(i + 1, nm - 1) * ns + jnp.minimum(f, ns - 1), 0)))
        args.append(x2d)
    in_specs += [
        pl.BlockSpec((tm, d), lambda i, f: (i, 0)),
        pl.BlockSpec((1, d), lambda i, f: (0, 0)),
        mod_spec(3), mod_spec(4), mod_spec(5),
        pl.BlockSpec((d, tf), lambda i, f: (0, f)),
        pl.BlockSpec((tf, d), lambda i, f: (f, 0)),
    ]
    args += [x2d, g.reshape(1, d), mod4, mod4, mod4, w1, w2]
    if final_g is not None:
        in_specs.append(pl.BlockSpec((1, d), lambda i, f: (0, 0)))
        args.append(final_g.reshape(1, d))
    c_in, c_args, c_out, c_shapes = _cast_plumbing(casts, nm * nf, lambda i, f: i * nf + f)
    return pl.pallas_call(
        functools.partial(_mlp_kernel, nf=nf, final=final_g is not None, n_cast=len(casts), ns=ns, nm=nm,
                          mod_row0=mod_row0, tiles_per_mod=rows_per_mod // tm),
        grid=(nm, nf),
        in_specs=in_specs + c_in,
        out_specs=[pl.BlockSpec((tm, d), lambda i, f: (i, 0))] + c_out,
        out_shape=[jax.ShapeDtypeStruct((t, d), F32)] + c_shapes,
        scratch_shapes=[pltpu.VMEM((tm, d), BF16)] * (2 if ns else 1),
        compiler_params=_params(("arbitrary", "arbitrary")),
        name=name,
    )(*args, *c_args)


def _log_sigmoid(x):
    return jnp.minimum(x, 0.0) - jnp.log1p(jnp.exp(-jnp.abs(x)))


def _ret_kernel(df_ref, db_ref, q_ref, k_ref, v_ref, g_ref, s0f_ref, s0b_ref,
                y_ref, sff_ref, sfb_ref, oc_ref, sf_ref, sb_ref, *, seq, ch, unroll):
    h = pl.program_id(1)
    n = seq // ch
    lgf = _log_sigmoid(jnp.full((ch, ch), df_ref[h], F32))
    lgb = _log_sigmoid(jnp.full((ch, ch), db_ref[h], F32))
    ii = lax.broadcasted_iota(jnp.int32, (ch, ch), 0)
    jj = lax.broadcasted_iota(jnp.int32, (ch, ch), 1)
    diff = (ii - jj).astype(F32)
    decay = (jnp.where(diff >= 0, jnp.exp(lgf * jnp.maximum(diff, 0.0)), 0.0)
             + jnp.where(diff <= 0, jnp.exp(lgb * jnp.maximum(-diff, 0.0)), 0.0))
    col = lax.broadcasted_iota(jnp.int32, (ch, 1), 0).astype(F32)
    lgf_c = _log_sigmoid(jnp.full((ch, 1), df_ref[h], F32))
    lgb_c = _log_sigmoid(jnp.full((ch, 1), db_ref[h], F32))
    dq_f = jnp.exp(lgf_c * (col + 1.0))
    dk_f = jnp.exp(lgf_c * (ch - 1.0 - col))
    dc_f = jnp.exp(_log_sigmoid(jnp.full((1, 1), df_ref[h], F32)) * float(ch))
    dq_b = jnp.exp(lgb_c * (ch - col))
    dk_b = jnp.exp(lgb_c * col)
    dc_b = jnp.exp(_log_sigmoid(jnp.full((1, 1), db_ref[h], F32)) * float(ch))
    tdims = (((0,), (0,)), ((), ()))
    ndims = (((1,), (1,)), ((), ()))

    sf_ref[...] = s0f_ref[...]
    sb_ref[...] = s0b_ref[...]

    def rows(c):
        return pl.ds(pl.multiple_of(c * ch, ch), ch)

    def cross(c, s_ref, dq, dk, dc, first):
        r = rows(c)
        q = q_ref[r, :]
        k = k_ref[r, :]
        v = v_ref[r, :]
        s = s_ref[...]
        term = jnp.dot(q, s.astype(BF16), preferred_element_type=F32) * dq
        if first:
            oc_ref[r, :] = term
        else:
            oc_ref[r, :] += term
        kd = (k.astype(F32) * dk).astype(BF16)
        s_ref[...] = s * dc + lax.dot_general(kd, v, tdims, preferred_element_type=F32)

    def sweep(first):
        def step(t, carry):
            cross(t, sf_ref, dq_f, dk_f, dc_f, first)
            cross(n - 1 - t, sb_ref, dq_b, dk_b, dc_b, first)
            return carry
        return step

    half = n // 2
    if n % 2:
        oc_ref[pl.ds(half * ch, ch), :] = jnp.zeros((ch, oc_ref.shape[1]), F32)
    if half:
        lax.fori_loop(0, half, sweep(True), 0, unroll=min(unroll, half))
    lax.fori_loop(half, n, sweep(False), 0, unroll=min(unroll, n - half))
    sff_ref[...] = sf_ref[...]
    sfb_ref[...] = sb_ref[...]

    def finish(c, carry):
        r = rows(c)
        q = q_ref[r, :]
        scores = lax.dot_general(q, k_ref[r, :], ndims, preferred_element_type=F32) * decay
        o = jnp.dot(scores.astype(BF16), v_ref[r, :], preferred_element_type=F32) + oc_ref[r, :]
        y = o * lax.rsqrt(jnp.mean(o * o, axis=-1, keepdims=True) + NORM_EPS)
        hg = 0.5 * g_ref[r, :].astype(F32)
        y_ref[r, :] = (hg * (1.0 + jnp.tanh(hg)) * y).astype(y_ref.dtype)
        return carry

    lax.fori_loop(0, n, finish, 0, unroll=min(unroll, n))


def _ret_call(qkvg, decay_f, decay_b, s0f, s0b, *, batch, seq, ch, name="retention"):
    heads = RET_HEADS
    dk = s0f.shape[2]
    dv = s0f.shape[3]
    smem = pl.BlockSpec(memory_space=pltpu.SMEM)
    state = pl.BlockSpec((None, None, dk, dv), lambda b, h: (b, h, 0, 0))
    return pl.pallas_call(
        functools.partial(_ret_kernel, seq=seq, ch=ch, unroll=4),
        grid=(batch, heads),
        in_specs=[
            smem, smem,
            pl.BlockSpec((seq, dk), lambda b, h: (b, h)),
            pl.BlockSpec((seq, dk), lambda b, h: (b, heads + h)),
            pl.BlockSpec((seq, dv), lambda b, h: (b, heads + h)),
            pl.BlockSpec((seq, dv), lambda b, h: (b, 2 * heads + h)),
            state, state,
        ],
        out_specs=[pl.BlockSpec((seq, dv), lambda b, h: (b, h)), state, state],
        out_shape=[
            jax.ShapeDtypeStruct((batch * seq, heads * dv), BF16),
            jax.ShapeDtypeStruct(s0f.shape, F32),
            jax.ShapeDtypeStruct(s0f.shape, F32),
        ],
        scratch_shapes=[
            pltpu.VMEM((seq, dv), F32),
            pltpu.VMEM((dk, dv), F32),
            pltpu.VMEM((dk, dv), F32),
        ],
        compiler_params=_params(("parallel", "arbitrary")),
        name=name,
    )(decay_f, decay_b, qkvg, qkvg, qkvg, qkvg, s0f, s0b)


def _attn_kernel(sink_ref, q_ref, kp_ref, kc_ref, kn_ref, vp_ref, vc_ref, vn_ref, kx_ref, vx_ref,
                 o_ref, *, nq, tq, group):
    n = pl.program_id(1)
    dh = ATT_HEAD_DIM
    n_ctx = kx_ref.shape[0]
    win = tq + 2 * WINDOW
    rows = group * tq
    nk = win + n_ctx
    scale = dh ** -0.5
    assert tq == WINDOW
    rq = lax.broadcasted_iota(jnp.int32, (rows, WINDOW), 0) % tq
    ck = lax.broadcasted_iota(jnp.int32, (rows, WINDOW), 1)
    ok_prev = (ck >= rq) & (n > 0)
    ok_next = (ck <= rq) & (n < nq - 1)
    rcol = lax.broadcasted_iota(jnp.int32, (rows, 1), 0) // tq
    ndims = (((1,), (1,)), ((), ()))

    for kh in range(ATT_KV_HEADS):
        hs = slice(kh * dh, (kh + 1) * dh)
        qs = jnp.concatenate(
            [q_ref[:, (kh * group + g) * dh:(kh * group + g + 1) * dh] for g in range(group)], axis=0)
        k_all = jnp.concatenate([kp_ref[:, hs], kc_ref[:, hs], kn_ref[:, hs], kx_ref[:, hs]], axis=0)
        v_all = jnp.concatenate([vp_ref[:, hs], vc_ref[:, hs], vn_ref[:, hs], vx_ref[:, hs]], axis=0)
        s = lax.dot_general(qs, k_all, ndims, preferred_element_type=F32) * scale
        s = jnp.concatenate([jnp.where(ok_prev, s[:, :WINDOW], NEG_INF), s[:, WINDOW:WINDOW + tq],
                             jnp.where(ok_next, s[:, WINDOW + tq:win], NEG_INF), s[:, win:]], axis=1)
        sink = jnp.zeros((rows, 1), F32)
        for g in range(group):
            sink = jnp.where(rcol == g, sink_ref[kh * group + g], sink)
        m = jnp.maximum(jnp.max(s, axis=-1, keepdims=True), sink)
        p = jnp.exp(s - m)
        denom = jnp.sum(p, axis=-1, keepdims=True) + jnp.exp(sink - m)
        o = jnp.dot(p.astype(BF16), v_all, preferred_element_type=F32) / denom
        for g in range(group):
            o_ref[:, (kh * group + g) * dh:(kh * group + g + 1) * dh] = (
                o[g * tq:(g + 1) * tq, :].astype(o_ref.dtype))


def _attn_call(qkv, kv_ctx, sink, *, batch, seq, n_ctx, tq, name="window_attn"):
    dh = ATT_HEAD_DIM
    kvw = ATT_KV_HEADS * dh
    qw = qkv.shape[1] - 2 * kvw
    group = qw // kvw
    nq = seq // tq
    wpb = seq // WINDOW
    step = tq // WINDOW
    k_blk = qw // kvw
    v_blk = k_blk + 1

    def prev_map(col):
        return lambda b, n: (b * wpb + jnp.maximum(n * step - 1, 0), col)

    def cur_map(col):
        return lambda b, n: (b * nq + n, col)

    def next_map(col):
        return lambda b, n: (b * wpb + jnp.minimum((n + 1) * step, wpb - 1), col)

    return pl.pallas_call(
        functools.partial(_attn_kernel, nq=nq, tq=tq, group=group),
        grid=(batch, nq),
        in_specs=[
            pl.BlockSpec(memory_space=pltpu.SMEM),
            pl.BlockSpec((tq, qw), lambda b, n: (b * nq + n, 0)),
            pl.BlockSpec((WINDOW, kvw), prev_map(k_blk)),
            pl.BlockSpec((tq, kvw), cur_map(k_blk)),
            pl.BlockSpec((WINDOW, kvw), next_map(k_blk)),
            pl.BlockSpec((WINDOW, kvw), prev_map(v_blk)),
            pl.BlockSpec((tq, kvw), cur_map(v_blk)),
            pl.BlockSpec((WINDOW, kvw), next_map(v_blk)),
            pl.BlockSpec((n_ctx, kvw), lambda b, n: (b, 0)),
            pl.BlockSpec((n_ctx, kvw), lambda b, n: (b, 1)),
        ],
        out_specs=pl.BlockSpec((tq, qw), lambda b, n: (b * nq + n, 0)),
        out_shape=jax.ShapeDtypeStruct((batch * seq, qw), BF16),
        compiler_params=_params(("parallel", "arbitrary")),
        name=name,
    )(sink, qkv, qkv, qkv, qkv, qkv, qkv, qkv, kv_ctx, kv_ctx)


def _rope_angles(pos, dim, base):
    inv_freq = base ** (-jnp.arange(0, dim, 2, dtype=F32) / dim)
    return pos.astype(F32)[:, None] * inv_freq[None, :]


def _axial_tables(seq):
    rows = jnp.arange(seq) // GRID_W
    cols = jnp.arange(seq) % GRID_W
    half = ATT_HEAD_DIM // 2
    ar = _rope_angles(rows, half, ROPE_BASE)
    ac = _rope_angles(cols, half, ROPE_BASE)
    cos = jnp.concatenate([jnp.cos(ar), jnp.cos(ar), jnp.cos(ac), jnp.cos(ac)], axis=-1)
    sin = jnp.concatenate([-jnp.sin(ar), jnp.sin(ar), -jnp.sin(ac), jnp.sin(ac)], axis=-1)
    return cos, sin


def kernel(x, c, ctx, c_ctx, ada_w, ada_b, norm_mix_g, norm_mlp_g, mlp_w1, mlp_w2, ret_w_in, ret_w_out,
           ret_decay_fwd, ret_decay_bwd, attn_w_in, attn_w_out, attn_sink, final_norm_g):
    batch, seq, d = x.shape
    n_ctx = ctx.shape[1]
    assert ada_w.shape[0] == 2 and ret_w_in.shape[0] == 1 and attn_w_in.shape[0] == 1
    assert batch + 1 <= MOD_ROWS
    ret_dk = d // RET_HEADS
    ret_dv = 2 * ret_dk
    qk_w = RET_HEADS * ret_dk
    att_qw = attn_w_out.shape[1]
    att_kvw = ATT_KV_HEADS * ATT_HEAD_DIM
    ctx_row = batch

    cond = jnp.concatenate([c, c_ctx[None, :], jnp.zeros((MOD_ROWS - batch - 1, d), F32)], axis=0)
    mod = _ada_call(cond, ada_w, ada_b)
    mod4 = mod.reshape(mod.shape[0], MOD_ROWS, 1, mod.shape[2])

    xl = x.reshape(batch * seq, d)
    xc = ctx.reshape(batch * n_ctx, d)
    t_ctx = batch * n_ctx
    lat = dict(rows_per_mod=seq, mod_row0=0)
    cx = dict(rows_per_mod=t_ctx, mod_row0=ctx_row)

    ret_ang = _rope_angles(jnp.arange(seq), ret_dk, ROPE_BASE)
    ret_cos, ret_sin = jnp.cos(ret_ang), jnp.sin(ret_ang)
    att_cos, att_sin = _axial_tables(seq)

    tm, tn, tn_ctx = 1024, 1024, 512
    k_scale = ret_dk ** -0.5
    tps = seq // tm
    one = jnp.ones((tm, 128), F32)
    zero = jnp.zeros((tm, 128), F32)

    def ret_tab_lat(i, j):
        q_tiles = qk_w // tn
        return jnp.where(j < q_tiles, i % tps, jnp.where(j < 2 * q_tiles, tps + i % tps, 2 * tps))

    def ret_tab_ctx(i, j):
        q_tiles = qk_w // tn_ctx
        return jnp.where((j >= q_tiles) & (j < 2 * q_tiles), 1, 0)

    assert t_ctx == tm
    ctx_tab = (jnp.concatenate([one, one * k_scale]), jnp.concatenate([zero, zero]), ret_tab_ctx)
    p_ctx, w_in = _proj_call(xc, norm_mix_g[0], mod4, 0, ret_w_in, tm=tm, tn=tn_ctx, rope="half",
                             tables=[ctx_tab], emit_w=True, name="ret_proj_ctx", **cx)
    casts = [(ret_w_out, 0, 32), (mlp_w1, 0, 16), (mlp_w2, 0, 64), (attn_w_in, 0, 16), (attn_w_out, 0, 16)]
    lat_tab = (jnp.concatenate([ret_cos, ret_cos * k_scale, one]),
               jnp.concatenate([ret_sin, ret_sin * k_scale, zero]), ret_tab_lat)
    p_lat, w_out, w1, w2, att_w_in, att_w_out = _proj_call(
        xl, norm_mix_g[0], mod4, 0, w_in, tm=tm, tn=tn, rope="half", tables=[lat_tab],
        ns=8, casts=casts, name="ret_proj_lat", **lat)
    zeros = jnp.zeros((batch, RET_HEADS, ret_dk, ret_dv), F32)
    y_ctx, sc_f, sc_b = _ret_call(p_ctx, ret_decay_fwd[0], ret_decay_bwd[0], zeros, zeros,
                                  batch=batch, seq=n_ctx, ch=min(256, n_ctx), name="retention_ctx")
    y_lat, _, _ = _ret_call(p_lat, ret_decay_fwd[0], ret_decay_bwd[0], sc_f, sc_b,
                            batch=batch, seq=seq, ch=256, name="retention_lat")
    xl = _out_call(y_lat, w_out, xl, mod4, 0, tm=1024, tn=1024, name="ret_out_lat", **lat)
    xc = _out_call(y_ctx, w_out, xc, mod4, 0, tm=t_ctx, tn=512, name="ret_out_ctx", **cx)
    xl, w1_next, w2_next = _mlp_call(xl, norm_mlp_g[0], mod4, 0, w1, w2, tm=512, tf=1024, ns=8,
                                     casts=[(mlp_w1, 1, 16), (mlp_w2, 1, 64)], name="mlp0_lat", **lat)
    xc, = _mlp_call(xc, norm_mlp_g[0], mod4, 0, w1, w2, tm=512, tf=1024, name="mlp0_ctx", **cx)

    kv_ctx, = _proj_call(xc, norm_mix_g[1], mod4, 1, att_w_in, tm=t_ctx, tn=512, n_off=att_qw // 512,
                         n_cols=2 * att_kvw, name="attn_proj_ctx", **cx)
    att_tn = 2 * att_kvw
    assert att_qw % att_tn == 0
    q_tiles = att_qw // att_tn
    att_c = jnp.concatenate([att_cos, one])
    att_s = jnp.concatenate([att_sin, zero])
    qkv, = _proj_call(xl, norm_mix_g[1], mod4, 1, att_w_in, tm=tm, tn=att_tn, rope="axial",
                      tables=[(att_c, att_s, lambda i, j: i % tps),
                              (att_c, att_s, lambda i, j: jnp.where(j < q_tiles, i % tps, tps))],
                      ns=2, name="attn_proj_lat", **lat)
    o = _attn_call(qkv, kv_ctx, attn_sink[0], batch=batch, seq=seq, n_ctx=n_ctx, tq=128)
    xl = _out_call(o, att_w_out, xl, mod4, 1, tm=1024, tn=1024, name="attn_out_lat", **lat)
    xl, = _mlp_call(xl, norm_mlp_g[1], mod4, 1, w1_next, w2_next, tm=512, tf=1024, final_g=final_norm_g,
                    ns=8, name="mlp1_lat", **lat)
    return xl.reshape(batch, seq, d)
```

```python
import functools

import jax
import jax.numpy as jnp
from jax import lax
from jax.experimental import pallas as pl
from jax.experimental.pallas import tpu as pltpu

F32 = jnp.float32
BF16 = jnp.bfloat16

NORM_EPS = 1e-6
NEG_INF = -1e30
LOG2E = 1.4426950408889634
GRID_W = 64
ROPE_BASE = 10000.0

RET_HEADS = 8
ATT_HEAD_DIM = 128
ATT_KV_HEADS = 4
WINDOW = 128

MOD_ROWS = 8
VMEM_LIMIT = 56 * 1024 * 1024


def _params(semantics, vmem=VMEM_LIMIT):
    return pltpu.CompilerParams(dimension_semantics=semantics, vmem_limit_bytes=vmem)


def _norm_mod(x, g, shift, scale):
    ms = jnp.mean(x * x, axis=-1, keepdims=True)
    y = x * lax.rsqrt(ms + NORM_EPS) * g
    return y * (1.0 + scale) + shift


def _ada_kernel(cond_ref, w_ref, b_ref, o_ref):
    c = cond_ref[...]
    s = c * jax.nn.sigmoid(c)
    o_ref[...] = jnp.dot(s.astype(BF16), w_ref[...].astype(BF16),
                         preferred_element_type=F32) + b_ref[...]


def _ada_call(cond, ada_w, ada_b, tn=1024):
    depth, d, n = ada_w.shape
    return pl.pallas_call(
        _ada_kernel,
        grid=(depth, n // tn),
        in_specs=[
            pl.BlockSpec((MOD_ROWS, d), lambda l, j: (0, 0)),
            pl.BlockSpec((None, d, tn), lambda l, j: (l, 0, j)),
            pl.BlockSpec((None, 1, tn), lambda l, j: (l, 0, j)),
        ],
        out_specs=pl.BlockSpec((None, MOD_ROWS, tn), lambda l, j: (l, 0, j)),
        out_shape=jax.ShapeDtypeStruct((depth, MOD_ROWS, n), F32),
        compiler_params=_params(("arbitrary", "arbitrary")),
        name="ada_mod",
    )(cond, ada_w, ada_b.reshape(depth, 1, n))


def _swap32(x):
    lane = lax.broadcasted_iota(jnp.int32, x.shape, 1)
    return jnp.where((lane & 32) == 0, pltpu.roll(x, 96, 1), pltpu.roll(x, 32, 1))


def _cast_plumbing(jobs, n_steps, step_of):
    in_specs, args, out_specs, out_shapes = [], [], [], []
    for arr, layer, rows in jobs:
        r, c = arr.shape[-2:]
        nblk = r // rows
        assert n_steps >= nblk and r % rows == 0
        rep = n_steps // nblk

        def blk(i, j, rep=rep, nblk=nblk):
            return jnp.minimum(step_of(i, j) // rep, nblk - 1)

        in_specs.append(pl.BlockSpec((None, rows, c), lambda i, j, blk=blk, layer=layer: (layer, blk(i, j), 0)))
        args.append(arr)
        out_specs.append(pl.BlockSpec((rows, c), lambda i, j, blk=blk: (blk(i, j), 0)))
        out_shapes.append(jax.ShapeDtypeStruct((r, c), BF16))
    return in_specs, args, out_specs, out_shapes


def _rotate_store(acc, o_ref, rope, tabs):
    tn = acc.shape[1]
    if rope is None:
        o_ref[...] = acc.astype(o_ref.dtype)
        return
    hd = 256 if rope == "half" else 128
    heads = tn // hd
    for hh in range(heads):
        cos_ref, sin_ref = tabs[hh * len(tabs) // heads]
        cos = cos_ref[...]
        sin = sin_ref[...]
        if rope == "half":
            x1 = acc[:, hh * 256:hh * 256 + 128]
            x2 = acc[:, hh * 256 + 128:(hh + 1) * 256]
            o_ref[:, hh * 256:hh * 256 + 128] = (x1 * cos - x2 * sin).astype(o_ref.dtype)
            o_ref[:, hh * 256 + 128:(hh + 1) * 256] = (x2 * cos + x1 * sin).astype(o_ref.dtype)
        else:
            xh = acc[:, hh * 128:(hh + 1) * 128]
            o_ref[:, hh * 128:(hh + 1) * 128] = (xh * cos + _swap32(xh) * sin).astype(o_ref.dtype)


def _proj_kernel(*refs, rope, n_tab, n_cast, emit_w, ns, nm, mod_row0, tiles_per_mod):
    refs = list(refs)
    xs_ref = refs.pop(0) if ns else None
    x_ref, g_ref, shift_ref, scale_ref, w_ref = refs[:5]
    refs = refs[5:]
    tabs = [(refs[2 * k], refs[2 * k + 1]) for k in range(n_tab)]
    refs = refs[2 * n_tab:]
    cast_src, refs = refs[:n_cast], refs[n_cast:]
    o_ref = refs.pop(0)
    wout_ref = refs.pop(0) if emit_w else None
    cast_dst, a_refs = refs[:n_cast], refs[n_cast:]
    i = pl.program_id(0)
    j = pl.program_id(1)

    def prologue(x, tile):
        b = mod_row0 + tile // tiles_per_mod
        return _norm_mod(x, g_ref[...], shift_ref[b], scale_ref[b]).astype(BF16)

    def body(a_cur, a_nxt):
        w = w_ref[...]
        if emit_w:
            w = w.astype(BF16)
            wout_ref[...] = w
        acc = jnp.dot(a_cur[...], w, preferred_element_type=F32)
        _rotate_store(acc, o_ref, rope, tabs)
        for src, dst in zip(cast_src, cast_dst):
            dst[...] = src[...].astype(BF16)
        if a_nxt is not None:
            @pl.when(j < ns)
            def _():
                rs = xs_ref.shape[0]
                a_nxt[pl.ds(pl.multiple_of(j * rs, rs), rs), :] = prologue(
                    xs_ref[...], jnp.minimum(i + 1, nm - 1))

    if ns == 0:
        @pl.when(j == 0)
        def _():
            a_refs[0][...] = prologue(x_ref[...], i)

        body(a_refs[0], None)
        return

    @pl.when((i == 0) & (j == 0))
    def _():
        a_refs[0][...] = prologue(x_ref[...], 0)

    @pl.when(i % 2 == 0)
    def _():
        body(a_refs[0], a_refs[1])

    @pl.when(i % 2 == 1)
    def _():
        body(a_refs[1], a_refs[0])


def _proj_call(x2d, g, mod4, layer, w, *, tm, tn, rows_per_mod, mod_row0, n_off=0, n_cols=None,
               rope=None, tables=(), ns=0, emit_w=False, casts=(), name="proj"):
    t, d = x2d.shape
    n_cols = w.shape[-1] if n_cols is None else n_cols
    nt = n_cols // tn
    nm = t // tm
    assert t % tm == 0 and n_cols % tn == 0 and rows_per_mod % tm == 0
    assert not emit_w or nm == 1
    assert ns == 0 or (tm % ns == 0 and ns <= nt and nm > 1)

    def mod_spec(k):
        return pl.BlockSpec((None, MOD_ROWS, 1, d), lambda i, j: (layer, 0, 0, k))

    in_specs, args = [], []
    if ns:
        in_specs.append(pl.BlockSpec(
            (tm // ns, d), lambda i, j: (jnp.minimum(i + 1, nm - 1) * ns + jnp.minimum(j, ns - 1), 0)))
        in_specs.append(pl.BlockSpec((tm, d), lambda i, j: (0, 0), pipeline_mode=pl.Buffered(1)))
        args += [x2d, x2d]
    else:
        in_specs.append(pl.BlockSpec((tm, d), lambda i, j: (i, 0)))
        args.append(x2d)
    in_specs += [pl.BlockSpec((1, d), lambda i, j: (0, 0)), mod_spec(0), mod_spec(1)]
    args += [g.reshape(1, d), mod4, mod4]
    if emit_w:
        in_specs.append(pl.BlockSpec((None, d, tn), lambda i, j: (0, 0, n_off + j)))
    else:
        in_specs.append(pl.BlockSpec((d, tn), lambda i, j: (0, n_off + j)))
    args.append(w)
    for cos, sin, tab_fn in tables:
        in_specs += [pl.BlockSpec((tm, 128), lambda i, j, tab_fn=tab_fn: (tab_fn(i, j), 0))] * 2
        args += [cos, sin]
    c_in, c_args, c_out, c_shapes = _cast_plumbing(casts, nm * nt, lambda i, j: i * nt + j)
    out_specs = [pl.BlockSpec((tm, tn), lambda i, j: (i, j))]
    out_shapes = [jax.ShapeDtypeStruct((t, n_cols), BF16)]
    if emit_w:
        out_specs.append(pl.BlockSpec((d, tn), lambda i, j: (0, j)))
        out_shapes.append(jax.ShapeDtypeStruct((d, n_cols), BF16))
    return pl.pallas_call(
        functools.partial(_proj_kernel, rope=rope, n_tab=len(tables), n_cast=len(casts), emit_w=emit_w, ns=ns, nm=nm,
                          mod_row0=mod_row0, tiles_per_mod=rows_per_mod // tm),
        grid=(nm, nt),
        in_specs=in_specs + c_in,
        out_specs=out_specs + c_out,
        out_shape=out_shapes + c_shapes,
        scratch_shapes=[pltpu.VMEM((tm, d), BF16)] * (2 if ns else 1),
        compiler_params=_params(("arbitrary", "arbitrary")),
        name=name,
    )(*args, *c_args)


def _out_kernel(y_ref, w_ref, x_ref, gate_ref, o_ref):
    acc = jnp.dot(y_ref[...], w_ref[...], preferred_element_type=F32)
    o_ref[...] = x_ref[...] + gate_ref[...] * acc


def _out_call(y, w, x2d, mod4, layer, *, tm, tn, rows_per_mod, mod_row0, name="out_proj"):
    t, k = y.shape
    d = w.shape[1]
    tiles_per_mod = rows_per_mod // tm
    gate_blk0 = 2 * d // tn
    return pl.pallas_call(
        _out_kernel,
        grid=(t // tm, d // tn),
        in_specs=[
            pl.BlockSpec((tm, k), lambda i, j: (i, 0)),
            pl.BlockSpec((k, tn), lambda i, j: (0, j)),
            pl.BlockSpec((tm, tn), lambda i, j: (i, j)),
            pl.BlockSpec((None, None, 1, tn),
                         lambda i, j: (layer, mod_row0 + i // tiles_per_mod, 0, gate_blk0 + j)),
        ],
        out_specs=pl.BlockSpec((tm, tn), lambda i, j: (i, j)),
        out_shape=jax.ShapeDtypeStruct((t, d), F32),
        compiler_params=_params(("parallel", "arbitrary")),
        name=name,
    )(y, w, x2d, mod4)


def _mlp_kernel(*refs, nf, final, n_cast, ns, nm, mod_row0, tiles_per_mod):
    refs = list(refs)
    xs_ref = refs.pop(0) if ns else None
    x_ref, g_ref, shift_ref, scale_ref, gate_ref, w1_ref, w2_ref = refs[:7]
    refs = refs[7:]
    fg_ref = refs.pop(0) if final else None
    cast_src, refs = refs[:n_cast], refs[n_cast:]
    o_ref = refs.pop(0)
    cast_dst, a_refs = refs[:n_cast], refs[n_cast:]
    i = pl.program_id(0)
    f = pl.program_id(1)

    def mod_row(tile):
        return mod_row0 + tile // tiles_per_mod

    def prologue(x, tile):
        b = mod_row(tile)
        return _norm_mod(x, g_ref[...], shift_ref[b], scale_ref[b]).astype(BF16)

    def body(a_cur, a_nxt):
        @pl.when(f == 0)
        def _():
            o_ref[...] = jnp.zeros_like(o_ref)

        h = jnp.dot(a_cur[...], w1_ref[...], preferred_element_type=F32)
        h = jnp.square(jnp.maximum(h, 0.0)).astype(BF16)
        o_ref[...] += jnp.dot(h, w2_ref[...], preferred_element_type=F32)
        for src, dst in zip(cast_src, cast_dst):
            dst[...] = src[...].astype(BF16)
        if a_nxt is not None:
            rs = xs_ref.shape[0]
            rows = pl.ds(pl.multiple_of(jnp.minimum(f, ns - 1) * rs, rs), rs)
            a_nxt[rows, :] = prologue(xs_ref[...], jnp.minimum(i + 1, nm - 1))

        @pl.when(f == nf - 1)
        def _():
            r = x_ref[...] + gate_ref[mod_row(i)] * o_ref[...]
            if final:
                ms = jnp.mean(r * r, axis=-1, keepdims=True)
                r = r * lax.rsqrt(ms + NORM_EPS) * fg_ref[...]
            o_ref[...] = r

    if ns == 0:
        @pl.when(f == 0)
        def _():
            a_refs[0][...] = prologue(x_ref[...], i)

        body(a_refs[0], None)
        return

    @pl.when((i == 0) & (f == 0))
    def _():
        a_refs[0][...] = prologue(x_ref[...], 0)

    @pl.when(i % 2 == 0)
    def _():
        body(a_refs[0], a_refs[1])

    @pl.when(i % 2 == 1)
    def _():
        body(a_refs[1], a_refs[0])


def _mlp_call(x2d, g, mod4, layer, w1, w2, *, tm, tf, rows_per_mod, mod_row0, final_g=None, ns=0, casts=(),
              name="mlp"):
    t, d = x2d.shape
    ff = w1.shape[1]
    nf = ff // tf
    nm = t // tm
    assert t % tm == 0 and ff % tf == 0 and rows_per_mod % tm == 0
    assert ns == 0 or (tm % ns == 0 and ns <= nf and nm > 1)

    def mod_spec(k):
        return pl.BlockSpec((None, MOD_ROWS, 1, d), lambda i, f: (layer, 0, 0, k))

    in_specs, args = [], []
    if ns:
        in_specs.append(pl.BlockSpec(
            (tm // ns, d), lambda i, f: (jnp.minimum(i + 1, nm - 1) * ns + jnp.minimum(f, ns - 1), 0)))
        args.append(x2d)
    in_specs += [
        pl.BlockSpec((tm, d), lambda i, f: (i, 0)),
        pl.BlockSpec((1, d), lambda i, f: (0, 0)),
        mod_spec(3), mod_spec(4), mod_spec(5),
        pl.BlockSpec((d, tf), lambda i, f: (0, f)),
        pl.BlockSpec((tf, d), lambda i, f: (f, 0)),
    ]
    args += [x2d, g.reshape(1, d), mod4, mod4, mod4, w1, w2]
    if final_g is not None:
        in_specs.append(pl.BlockSpec((1, d), lambda i, f: (0, 0)))
        args.append(final_g.reshape(1, d))
    c_in, c_args, c_out, c_shapes = _cast_plumbing(casts, nm * nf, lambda i, f: i * nf + f)
    return pl.pallas_call(
        functools.partial(_mlp_kernel, nf=nf, final=final_g is not None, n_cast=len(casts), ns=ns, nm=nm,
                          mod_row0=mod_row0, tiles_per_mod=rows_per_mod // tm),
        grid=(nm, nf),
        in_specs=in_specs + c_in,
        out_specs=[pl.BlockSpec((tm, d), lambda i, f: (i, 0))] + c_out,
        out_shape=[jax.ShapeDtypeStruct((t, d), F32)] + c_shapes,
        scratch_shapes=[pltpu.VMEM((tm, d), BF16)] * (2 if ns else 1),
        compiler_params=_params(("arbitrary", "arbitrary")),
        name=name,
    )(*args, *c_args)


def _log_sigmoid(x):
    return jnp.minimum(x, 0.0) - jnp.log1p(jnp.exp(-jnp.abs(x)))


def _ret_kernel(df_ref, db_ref, q_ref, k_ref, v_ref, g_ref, s0f_ref, s0b_ref,
                y_ref, sff_ref, sfb_ref, oc_ref, sf_ref, sb_ref, *, seq, ch, unroll):
    h = pl.program_id(1)
    n = seq // ch
    lgf = _log_sigmoid(jnp.full((ch, ch), df_ref[h], F32))
    lgb = _log_sigmoid(jnp.full((ch, ch), db_ref[h], F32))
    ii = lax.broadcasted_iota(jnp.int32, (ch, ch), 0)
    jj = lax.broadcasted_iota(jnp.int32, (ch, ch), 1)
    diff = (ii - jj).astype(F32)
    decay = (jnp.where(diff >= 0, jnp.exp(lgf * jnp.maximum(diff, 0.0)), 0.0)
             + jnp.where(diff <= 0, jnp.exp(lgb * jnp.maximum(-diff, 0.0)), 0.0))
    col = lax.broadcasted_iota(jnp.int32, (ch, 1), 0).astype(F32)
    lgf_c = _log_sigmoid(jnp.full((ch, 1), df_ref[h], F32))
    lgb_c = _log_sigmoid(jnp.full((ch, 1), db_ref[h], F32))
    dq_f = jnp.exp(lgf_c * (col + 1.0))
    dk_f = jnp.exp(lgf_c * (ch - 1.0 - col))
    dc_f = jnp.exp(_log_sigmoid(jnp.full((1, 1), df_ref[h], F32)) * float(ch))
    dq_b = jnp.exp(lgb_c * (ch - col))
    dk_b = jnp.exp(lgb_c * col)
    dc_b = jnp.exp(_log_sigmoid(jnp.full((1, 1), db_ref[h], F32)) * float(ch))
    tdims = (((0,), (0,)), ((), ()))
    ndims = (((1,), (1,)), ((), ()))

    sf_ref[...] = s0f_ref[...]
    sb_ref[...] = s0b_ref[...]

    def rows(c):
        return pl.ds(pl.multiple_of(c * ch, ch), ch)

    def cross(c, s_ref, dq, dk, dc, first):
        r = rows(c)
        q = q_ref[r, :]
        k = k_ref[r, :]
        v = v_ref[r, :]
        s = s_ref[...]
        term = jnp.dot(q, s.astype(BF16), preferred_element_type=F32) * dq
        if first:
            oc_ref[r, :] = term
        else:
            oc_ref[r, :] += term
        kd = (k.astype(F32) * dk).astype(BF16)
        s_ref[...] = s * dc + lax.dot_general(kd, v, tdims, preferred_element_type=F32)

    def sweep(first):
        def step(t, carry):
            cross(t, sf_ref, dq_f, dk_f, dc_f, first)
            cross(n - 1 - t, sb_ref, dq_b, dk_b, dc_b, first)
            return carry
        return step

    half = n // 2
    if n % 2:
        oc_ref[pl.ds(half * ch, ch), :] = jnp.zeros((ch, oc_ref.shape[1]), F32)
    if half:
        lax.fori_loop(0, half, sweep(True), 0, unroll=min(unroll, half))
    lax.fori_loop(half, n, sweep(False), 0, unroll=min(unroll, n - half))
    sff_ref[...] = sf_ref[...]
    sfb_ref[...] = sb_ref[...]

    def finish(c, carry):
        r = rows(c)
        q = q_ref[r, :]
        scores = lax.dot_general(q, k_ref[r, :], ndims, preferred_element_type=F32) * decay
        o = jnp.dot(scores.astype(BF16), v_ref[r, :], preferred_element_type=F32) + oc_ref[r, :]
        y = o * lax.rsqrt(jnp.mean(o * o, axis=-1, keepdims=True) + NORM_EPS)
        hg = 0.5 * g_ref[r, :].astype(F32)
        y_ref[r, :] = (hg * (1.0 + jnp.tanh(hg)) * y).astype(y_ref.dtype)
        return carry

    lax.fori_loop(0, n, finish, 0, unroll=min(unroll, n))


def _ret_call(qkvg, decay_f, decay_b, s0f, s0b, *, batch, seq, ch, name="retention"):
    heads = RET_HEADS
    dk = s0f.shape[2]
    dv = s0f.shape[3]
    smem = pl.BlockSpec(memory_space=pltpu.SMEM)
    state = pl.BlockSpec((None, None, dk, dv), lambda b, h: (b, h, 0, 0))
    return pl.pallas_call(
        functools.partial(_ret_kernel, seq=seq, ch=ch, unroll=4),
        grid=(batch, heads),
        in_specs=[
            smem, smem,
            pl.BlockSpec((seq, dk), lambda b, h: (b, h)),
            pl.BlockSpec((seq, dk), lambda b, h: (b, heads + h)),
            pl.BlockSpec((seq, dv), lambda b, h: (b, heads + h)),
            pl.BlockSpec((seq, dv), lambda b, h: (b, 2 * heads + h)),
            state, state,
        ],
        out_specs=[pl.BlockSpec((seq, dv), lambda b, h: (b, h)), state, state],
        out_shape=[
            jax.ShapeDtypeStruct((batch * seq, heads * dv), BF16),
            jax.ShapeDtypeStruct(s0f.shape, F32),
            jax.ShapeDtypeStruct(s0f.shape, F32),
        ],
        scratch_shapes=[
            pltpu.VMEM((seq, dv), F32),
            pltpu.VMEM((dk, dv), F32),
            pltpu.VMEM((dk, dv), F32),
        ],
        compiler_params=_params(("parallel", "arbitrary")),
        name=name,
    )(decay_f, decay_b, qkvg, qkvg, qkvg, qkvg, s0f, s0b)


def _attn_kernel(sink_ref, q_ref, kp_ref, kc_ref, kn_ref, vp_ref, vc_ref, vn_ref, kx_ref, vx_ref,
                 o_ref, *, nq, tq, group):
    n = pl.program_id(1)
    dh = ATT_HEAD_DIM
    n_ctx = kx_ref.shape[0]
    win = tq + 2 * WINDOW
    rows = group * tq
    nk = win + n_ctx
    scale = dh ** -0.5
    assert tq == WINDOW
    rq = lax.broadcasted_iota(jnp.int32, (rows, WINDOW), 0) % tq
    ck = lax.broadcasted_iota(jnp.int32, (rows, WINDOW), 1)
    ok_prev = (ck >= rq) & (n > 0)
    ok_next = (ck <= rq) & (n < nq - 1)
    rcol = lax.broadcasted_iota(jnp.int32, (rows, 1), 0) // tq
    ndims = (((1,), (1,)), ((), ()))
    ones = jnp.ones((nk, dh), BF16)

    for kh in range(ATT_KV_HEADS):
        hs = slice(kh * dh, (kh + 1) * dh)
        qs = jnp.concatenate(
            [q_ref[:, (kh * group + g) * dh:(kh * group + g + 1) * dh] for g in range(group)], axis=0)
        k_all = jnp.concatenate([kp_ref[:, hs], kc_ref[:, hs], kn_ref[:, hs], kx_ref[:, hs]], axis=0)
        v_all = jnp.concatenate([vp_ref[:, hs], vc_ref[:, hs], vn_ref[:, hs], vx_ref[:, hs]], axis=0)
        s = lax.dot_general(qs, k_all, ndims, preferred_element_type=F32) * (scale * LOG2E)
        s = jnp.concatenate([jnp.where(ok_prev, s[:, :WINDOW], NEG_INF), s[:, WINDOW:WINDOW + tq],
                             jnp.where(ok_next, s[:, WINDOW + tq:win], NEG_INF), s[:, win:]], axis=1)
        sink = jnp.zeros((rows, 1), F32)
        for g in range(group):
            sink = jnp.where(rcol == g, sink_ref[kh * group + g] * LOG2E, sink)
        m = jnp.maximum(jnp.max(s, axis=-1, keepdims=True), sink)
        p = jnp.exp2(s - m).astype(BF16)
        pv = jnp.dot(p, jnp.concatenate([v_all, ones], axis=1), preferred_element_type=F32)
        o = pv[:, :dh] / (pv[:, dh:dh + 1] + jnp.exp2(sink - m))
        for g in range(group):
            o_ref[:, (kh * group + g) * dh:(kh * group + g + 1) * dh] = (
                o[g * tq:(g + 1) * tq, :].astype(o_ref.dtype))


def _attn_call(qkv, kv_ctx, sink, *, batch, seq, n_ctx, tq, name="window_attn"):
    dh = ATT_HEAD_DIM
    kvw = ATT_KV_HEADS * dh
    qw = qkv.shape[1] - 2 * kvw
    group = qw // kvw
    nq = seq // tq
    wpb = seq // WINDOW
    step = tq // WINDOW
    k_blk = qw // kvw
    v_blk = k_blk + 1

    def prev_map(col):
        return lambda b, n: (b * wpb + jnp.maximum(n * step - 1, 0), col)

    def cur_map(col):
        return lambda b, n: (b * nq + n, col)

    def next_map(col):
        return lambda b, n: (b * wpb + jnp.minimum((n + 1) * step, wpb - 1), col)

    return pl.pallas_call(
        functools.partial(_attn_kernel, nq=nq, tq=tq, group=group),
        grid=(batch, nq),
        in_specs=[
            pl.BlockSpec(memory_space=pltpu.SMEM),
            pl.BlockSpec((tq, qw), lambda b, n: (b * nq + n, 0)),
            pl.BlockSpec((WINDOW, kvw), prev_map(k_blk)),
            pl.BlockSpec((tq, kvw), cur_map(k_blk)),
            pl.BlockSpec((WINDOW, kvw), next_map(k_blk)),
            pl.BlockSpec((WINDOW, kvw), prev_map(v_blk)),
            pl.BlockSpec((tq, kvw), cur_map(v_blk)),
            pl.BlockSpec((WINDOW, kvw), next_map(v_blk)),
            pl.BlockSpec((n_ctx, kvw), lambda b, n: (b, 0)),
            pl.BlockSpec((n_ctx, kvw), lambda b, n: (b, 1)),
        ],
        out_specs=pl.BlockSpec((tq, qw), lambda b, n: (b * nq + n, 0)),
        out_shape=jax.ShapeDtypeStruct((batch * seq, qw), BF16),
        compiler_params=_params(("parallel", "arbitrary")),
        name=name,
    )(sink, qkv, qkv, qkv, qkv, qkv, qkv, qkv, kv_ctx, kv_ctx)


def _rope_angles(pos, dim, base):
    inv_freq = base ** (-jnp.arange(0, dim, 2, dtype=F32) / dim)
    return pos.astype(F32)[:, None] * inv_freq[None, :]


def _axial_tables(seq):
    rows = jnp.arange(seq) // GRID_W
    cols = jnp.arange(seq) % GRID_W
    half = ATT_HEAD_DIM // 2
    ar = _rope_angles(rows, half, ROPE_BASE)
    ac = _rope_angles(cols, half, ROPE_BASE)
    cos = jnp.concatenate([jnp.cos(ar), jnp.cos(ar), jnp.cos(ac), jnp.cos(ac)], axis=-1)
    sin = jnp.concatenate([-jnp.sin(ar), jnp.sin(ar), -jnp.sin(ac), jnp.sin(ac)], axis=-1)
    return cos, sin


def kernel(x, c, ctx, c_ctx, ada_w, ada_b, norm_mix_g, norm_mlp_g, mlp_w1, mlp_w2, ret_w_in, ret_w_out,
           ret_decay_fwd, ret_decay_bwd, attn_w_in, attn_w_out, attn_sink, final_norm_g):
    batch, seq, d = x.shape
    n_ctx = ctx.shape[1]
    assert ada_w.shape[0] == 2 and ret_w_in.shape[0] == 1 and attn_w_in.shape[0] == 1
    assert batch + 1 <= MOD_ROWS
    ret_dk = d // RET_HEADS
    ret_dv = 2 * ret_dk
    qk_w = RET_HEADS * ret_dk
    att_qw = attn_w_out.shape[1]
    att_kvw = ATT_KV_HEADS * ATT_HEAD_DIM
    ctx_row = batch

    cond = jnp.concatenate([c, c_ctx[None, :], jnp.zeros((MOD_ROWS - batch - 1, d), F32)], axis=0)
    mod = _ada_call(cond, ada_w, ada_b)
    mod4 = mod.reshape(mod.shape[0], MOD_ROWS, 1, mod.shape[2])

    xl = x.reshape(batch * seq, d)
    xc = ctx.reshape(batch * n_ctx, d)
    t_ctx = batch * n_ctx
    lat = dict(rows_per_mod=seq, mod_row0=0)
    cx = dict(rows_per_mod=t_ctx, mod_row0=ctx_row)

    ret_ang = _rope_angles(jnp.arange(seq), ret_dk, ROPE_BASE)
    ret_cos, ret_sin = jnp.cos(ret_ang), jnp.sin(ret_ang)
    att_cos, att_sin = _axial_tables(seq)

    tm, tn, tn_ctx = 1024, 1024, 512
    k_scale = ret_dk ** -0.5
    tps = seq // tm
    one = jnp.ones((tm, 128), F32)
    zero = jnp.zeros((tm, 128), F32)

    def ret_tab_lat(i, j):
        q_tiles = qk_w // tn
        return jnp.where(j < q_tiles, i % tps, jnp.where(j < 2 * q_tiles, tps + i % tps, 2 * tps))

    def ret_tab_ctx(i, j):
        q_tiles = qk_w // tn_ctx
        return jnp.where((j >= q_tiles) & (j < 2 * q_tiles), 1, 0)

    assert t_ctx == tm
    ctx_tab = (jnp.concatenate([one, one * k_scale]), jnp.concatenate([zero, zero]), ret_tab_ctx)
    p_ctx, w_in = _proj_call(xc, norm_mix_g[0], mod4, 0, ret_w_in, tm=tm, tn=tn_ctx, rope="half",
                             tables=[ctx_tab], emit_w=True, name="ret_proj_ctx", **cx)
    casts = [(ret_w_out, 0, 32), (mlp_w1, 0, 16), (mlp_w2, 0, 64), (attn_w_in, 0, 16), (attn_w_out, 0, 16)]
    lat_tab = (jnp.concatenate([ret_cos, ret_cos * k_scale, one]),
               jnp.concatenate([ret_sin, ret_sin * k_scale, zero]), ret_tab_lat)
    p_lat, w_out, w1, w2, att_w_in, att_w_out = _proj_call(
        xl, norm_mix_g[0], mod4, 0, w_in, tm=tm, tn=tn, rope="half", tables=[lat_tab],
        ns=8, casts=casts, name="ret_proj_lat", **lat)
    zeros = jnp.zeros((batch, RET_HEADS, ret_dk, ret_dv), F32)
    y_ctx, sc_f, sc_b = _ret_call(p_ctx, ret_decay_fwd[0], ret_decay_bwd[0], zeros, zeros,
                                  batch=batch, seq=n_ctx, ch=min(256, n_ctx), name="retention_ctx")
    y_lat, _, _ = _ret_call(p_lat, ret_decay_fwd[0], ret_decay_bwd[0], sc_f, sc_b,
                            batch=batch, seq=seq, ch=256, name="retention_lat")
    xl = _out_call(y_lat, w_out, xl, mod4, 0, tm=1024, tn=1024, name="ret_out_lat", **lat)
    xc = _out_call(y_ctx, w_out, xc, mod4, 0, tm=t_ctx, tn=512, name="ret_out_ctx", **cx)
    xl, w1_next, w2_next = _mlp_call(xl, norm_mlp_g[0], mod4, 0, w1, w2, tm=512, tf=1024, ns=8,
                                     casts=[(mlp_w1, 1, 16), (mlp_w2, 1, 64)], name="mlp0_lat", **lat)
    xc, = _mlp_call(xc, norm_mlp_g[0], mod4, 0, w1, w2, tm=512, tf=1024, name="mlp0_ctx", **cx)

    kv_ctx, = _proj_call(xc, norm_mix_g[1], mod4, 1, att_w_in, tm=t_ctx, tn=512, n_off=att_qw // 512,
                         n_cols=2 * att_kvw, name="attn_proj_ctx", **cx)
    att_tn = 2 * att_kvw
    assert att_qw % att_tn == 0
    q_tiles = att_qw // att_tn
    att_c = jnp.concatenate([att_cos, one])
    att_s = jnp.concatenate([att_sin, zero])
    qkv, = _proj_call(xl, norm_mix_g[1], mod4, 1, att_w_in, tm=tm, tn=att_tn, rope="axial",
                      tables=[(att_c, att_s, lambda i, j: i % tps),
                              (att_c, att_s, lambda i, j: jnp.where(j < q_tiles, i % tps, tps))],
                      ns=2, name="attn_proj_lat", **lat)
    o = _attn_call(qkv, kv_ctx, attn_sink[0], batch=batch, seq=seq, n_ctx=n_ctx, tq=128)
    xl = _out_call(o, att_w_out, xl, mod4, 1, tm=1024, tn=1024, name="attn_out_lat", **lat)
    xl, = _mlp_call(xl, norm_mlp_g[1], mod4, 1, w1_next, w2_next, tm=512, tf=1024, final_g=final_norm_g,
                    ns=8, name="mlp1_lat", **lat)
    return xl.reshape(batch, seq, d)
```

```python
import functools

import jax
import jax.numpy as jnp
from jax import lax
from jax.experimental import pallas as pl
from jax.experimental.pallas import tpu as pltpu

F32 = jnp.float32
BF16 = jnp.bfloat16

NORM_EPS = 1e-6
NEG_INF = -1e30
LOG2E = 1.4426950408889634
GRID_W = 64
ROPE_BASE = 10000.0

RET_HEADS = 8
ATT_HEAD_DIM = 128
ATT_KV_HEADS = 4
WINDOW = 128

MOD_ROWS = 8
VMEM_LIMIT = 56 * 1024 * 1024


def _params(semantics, vmem=VMEM_LIMIT):
    return pltpu.CompilerParams(dimension_semantics=semantics, vmem_limit_bytes=vmem)


def _norm_mod(x, g, shift, scale):
    ms = jnp.mean(x * x, axis=-1, keepdims=True)
    y = x * lax.rsqrt(ms + NORM_EPS) * g
    return y * (1.0 + scale) + shift


def _ada_kernel(cond_ref, w_ref, b_ref, o_ref):
    c = cond_ref[...]
    s = c * jax.nn.sigmoid(c)
    o_ref[...] = jnp.dot(s.astype(BF16), w_ref[...].astype(BF16),
                         preferred_element_type=F32) + b_ref[...]


def _ada_call(cond, ada_w, ada_b, tn=1024):
    depth, d, n = ada_w.shape
    return pl.pallas_call(
        _ada_kernel,
        grid=(depth, n // tn),
        in_specs=[
            pl.BlockSpec((MOD_ROWS, d), lambda l, j: (0, 0)),
            pl.BlockSpec((None, d, tn), lambda l, j: (l, 0, j)),
            pl.BlockSpec((None, 1, tn), lambda l, j: (l, 0, j)),
        ],
        out_specs=pl.BlockSpec((None, MOD_ROWS, tn), lambda l, j: (l, 0, j)),
        out_shape=jax.ShapeDtypeStruct((depth, MOD_ROWS, n), F32),
        compiler_params=_params(("arbitrary", "arbitrary")),
        name="ada_mod",
    )(cond, ada_w, ada_b.reshape(depth, 1, n))


def _cast_plumbing(jobs, n_steps, step_of):
    in_specs, args, out_specs, out_shapes = [], [], [], []
    for arr, layer, rows in jobs:
        r, c = arr.shape[-2:]
        nblk = r // rows
        assert n_steps >= nblk and r % rows == 0
        rep = n_steps // nblk

        def blk(i, j, rep=rep, nblk=nblk):
            return jnp.minimum(step_of(i, j) // rep, nblk - 1)

        in_specs.append(pl.BlockSpec((None, rows, c), lambda i, j, blk=blk, layer=layer: (layer, blk(i, j), 0)))
        args.append(arr)
        out_specs.append(pl.BlockSpec((rows, c), lambda i, j, blk=blk: (blk(i, j), 0)))
        out_shapes.append(jax.ShapeDtypeStruct((r, c), BF16))
    return in_specs, args, out_specs, out_shapes


def _rotate_store(acc, o_ref, rope, tabs):
    tn = acc.shape[1]
    if not rope:
        o_ref[...] = acc.astype(o_ref.dtype)
        return
    groups = tn // 256
    for hh in range(groups):
        cos_ref, sin_ref = tabs[hh * len(tabs) // groups]
        cos = cos_ref[...]
        sin = sin_ref[...]
        x1 = acc[:, hh * 256:hh * 256 + 128]
        x2 = acc[:, hh * 256 + 128:(hh + 1) * 256]
        o_ref[:, hh * 256:hh * 256 + 128] = (x1 * cos - x2 * sin).astype(o_ref.dtype)
        o_ref[:, hh * 256 + 128:(hh + 1) * 256] = (x2 * cos + x1 * sin).astype(o_ref.dtype)


def _proj_kernel(*refs, rope, n_tab, n_cast, emit_w, ns, nm, mod_row0, tiles_per_mod):
    refs = list(refs)
    xs_ref = refs.pop(0) if ns else None
    x_ref, g_ref, shift_ref, scale_ref, w_ref = refs[:5]
    refs = refs[5:]
    tabs = [(refs[2 * k], refs[2 * k + 1]) for k in range(n_tab)]
    refs = refs[2 * n_tab:]
    cast_src, refs = refs[:n_cast], refs[n_cast:]
    o_ref = refs.pop(0)
    wout_ref = refs.pop(0) if emit_w else None
    cast_dst, a_refs = refs[:n_cast], refs[n_cast:]
    i = pl.program_id(0)
    j = pl.program_id(1)

    def prologue(x, tile):
        b = mod_row0 + tile // tiles_per_mod
        return _norm_mod(x, g_ref[...], shift_ref[b], scale_ref[b]).astype(BF16)

    def body(a_cur, a_nxt):
        w = w_ref[...]
        if emit_w:
            w = w.astype(BF16)
            wout_ref[...] = w
        acc = jnp.dot(a_cur[...], w, preferred_element_type=F32)
        _rotate_store(acc, o_ref, rope, tabs)
        for src, dst in zip(cast_src, cast_dst):
            dst[...] = src[...].astype(BF16)
        if a_nxt is not None:
            @pl.when(j < ns)
            def _():
                rs = xs_ref.shape[0]
                a_nxt[pl.ds(pl.multiple_of(j * rs, rs), rs), :] = prologue(
                    xs_ref[...], jnp.minimum(i + 1, nm - 1))

    if ns == 0:
        @pl.when(j == 0)
        def _():
            a_refs[0][...] = prologue(x_ref[...], i)

        body(a_refs[0], None)
        return

    @pl.when((i == 0) & (j == 0))
    def _():
        a_refs[0][...] = prologue(x_ref[...], 0)

    @pl.when(i % 2 == 0)
    def _():
        body(a_refs[0], a_refs[1])

    @pl.when(i % 2 == 1)
    def _():
        body(a_refs[1], a_refs[0])


def _proj_call(x2d, g, mod4, layer, w, *, tm, tn, rows_per_mod, mod_row0, n_off=0, n_cols=None,
               rope=None, tables=(), ns=0, emit_w=False, casts=(), name="proj"):
    t, d = x2d.shape
    n_cols = w.shape[-1] if n_cols is None else n_cols
    nt = n_cols // tn
    nm = t // tm
    assert t % tm == 0 and n_cols % tn == 0 and rows_per_mod % tm == 0
    assert not emit_w or nm == 1
    assert ns == 0 or (tm % ns == 0 and ns <= nt and nm > 1)

    def mod_spec(k):
        return pl.BlockSpec((None, MOD_ROWS, 1, d), lambda i, j: (layer, 0, 0, k))

    in_specs, args = [], []
    if ns:
        in_specs.append(pl.BlockSpec(
            (tm // ns, d), lambda i, j: (jnp.minimum(i + 1, nm - 1) * ns + jnp.minimum(j, ns - 1), 0)))
        in_specs.append(pl.BlockSpec((tm, d), lambda i, j: (0, 0), pipeline_mode=pl.Buffered(1)))
        args += [x2d, x2d]
    else:
        in_specs.append(pl.BlockSpec((tm, d), lambda i, j: (i, 0)))
        args.append(x2d)
    in_specs += [pl.BlockSpec((1, d), lambda i, j: (0, 0)), mod_spec(0), mod_spec(1)]
    args += [g.reshape(1, d), mod4, mod4]
    if emit_w:
        in_specs.append(pl.BlockSpec((None, d, tn), lambda i, j: (0, 0, n_off + j)))
    else:
        in_specs.append(pl.BlockSpec((d, tn), lambda i, j: (0, n_off + j)))
    args.append(w)
    for cos, sin, tab_fn in tables:
        in_specs += [pl.BlockSpec((tm, 128), lambda i, j, tab_fn=tab_fn: (tab_fn(i, j), 0))] * 2
        args += [cos, sin]
    c_in, c_args, c_out, c_shapes = _cast_plumbing(casts, nm * nt, lambda i, j: i * nt + j)
    out_specs = [pl.BlockSpec((tm, tn), lambda i, j: (i, j))]
    out_shapes = [jax.ShapeDtypeStruct((t, n_cols), BF16)]
    if emit_w:
        out_specs.append(pl.BlockSpec((d, tn), lambda i, j: (0, j)))
        out_shapes.append(jax.ShapeDtypeStruct((d, n_cols), BF16))
    return pl.pallas_call(
        functools.partial(_proj_kernel, rope=rope, n_tab=len(tables), n_cast=len(casts), emit_w=emit_w, ns=ns, nm=nm,
                          mod_row0=mod_row0, tiles_per_mod=rows_per_mod // tm),
        grid=(nm, nt),
        in_specs=in_specs + c_in,
        out_specs=out_specs + c_out,
        out_shape=out_shapes + c_shapes,
        scratch_shapes=[pltpu.VMEM((tm, d), BF16)] * (2 if ns else 1),
        compiler_params=_params(("arbitrary", "arbitrary")),
        name=name,
    )(*args, *c_args)


def _out_kernel(y_ref, w_ref, x_ref, gate_ref, o_ref):
    acc = jnp.dot(y_ref[...], w_ref[...], preferred_element_type=F32)
    o_ref[...] = x_ref[...] + gate_ref[...] * acc


def _out_call(y, w, x2d, mod4, layer, *, tm, tn, rows_per_mod, mod_row0, name="out_proj"):
    t, k = y.shape
    d = w.shape[1]
    tiles_per_mod = rows_per_mod // tm
    gate_blk0 = 2 * d // tn
    return pl.pallas_call(
        _out_kernel,
        grid=(t // tm, d // tn),
        in_specs=[
            pl.BlockSpec((tm, k), lambda i, j: (i, 0)),
            pl.BlockSpec((k, tn), lambda i, j: (0, j)),
            pl.BlockSpec((tm, tn), lambda i, j: (i, j)),
            pl.BlockSpec((None, None, 1, tn),
                         lambda i, j: (layer, mod_row0 + i // tiles_per_mod, 0, gate_blk0 + j)),
        ],
        out_specs=pl.BlockSpec((tm, tn), lambda i, j: (i, j)),
        out_shape=jax.ShapeDtypeStruct((t, d), F32),
        compiler_params=_params(("parallel", "arbitrary")),
        name=name,
    )(y, w, x2d, mod4)


def _mlp_kernel(*refs, nf, final, n_cast, ns, nm, mod_row0, tiles_per_mod):
    refs = list(refs)
    xs_ref = refs.pop(0) if ns else None
    x_ref, g_ref, shift_ref, scale_ref, gate_ref, w1_ref, w2_ref = refs[:7]
    refs = refs[7:]
    fg_ref = refs.pop(0) if final else None
    cast_src, refs = refs[:n_cast], refs[n_cast:]
    o_ref = refs.pop(0)
    cast_dst, a_refs = refs[:n_cast], refs[n_cast:]
    i = pl.program_id(0)
    f = pl.program_id(1)

    def mod_row(tile):
        return mod_row0 + tile // tiles_per_mod

    def prologue(x, tile):
        b = mod_row(tile)
        return _norm_mod(x, g_ref[...], shift_ref[b], scale_ref[b]).astype(BF16)

    def body(a_cur, a_nxt):
        @pl.when(f == 0)
        def _():
            o_ref[...] = jnp.zeros_like(o_ref)

        h = jnp.dot(a_cur[...], w1_ref[...], preferred_element_type=F32)
        h = jnp.square(jnp.maximum(h, 0.0)).astype(BF16)
        o_ref[...] += jnp.dot(h, w2_ref[...], preferred_element_type=F32)
        for src, dst in zip(cast_src, cast_dst):
            dst[...] = src[...].astype(BF16)
        if a_nxt is not None:
            rs = xs_ref.shape[0]
            rows = pl.ds(pl.multiple_of(jnp.minimum(f, ns - 1) * rs, rs), rs)
            a_nxt[rows, :] = prologue(xs_ref[...], jnp.minimum(i + 1, nm - 1))

        @pl.when(f == nf - 1)
        def _():
            r = x_ref[...] + gate_ref[mod_row(i)] * o_ref[...]
            if final:
                ms = jnp.mean(r * r, axis=-1, keepdims=True)
                r = r * lax.rsqrt(ms + NORM_EPS) * fg_ref[...]
            o_ref[...] = r

    if ns == 0:
        @pl.when(f == 0)
        def _():
            a_refs[0][...] = prologue(x_ref[...], i)

        body(a_refs[0], None)
        return

    @pl.when((i == 0) & (f == 0))
    def _():
        a_refs[0][...] = prologue(x_ref[...], 0)

    @pl.when(i % 2 == 0)
    def _():
        body(a_refs[0], a_refs[1])

    @pl.when(i % 2 == 1)
    def _():
        body(a_refs[1], a_refs[0])


def _mlp_call(x2d, g, mod4, layer, w1, w2, *, tm, tf, rows_per_mod, mod_row0, final_g=None, ns=0, casts=(),
              name="mlp"):
    t, d = x2d.shape
    ff = w1.shape[1]
    nf = ff // tf
    nm = t // tm
    assert t % tm == 0 and ff % tf == 0 and rows_per_mod % tm == 0
    assert ns == 0 or (tm % ns == 0 and ns <= nf and nm > 1)

    def mod_spec(k):
        return pl.BlockSpec((None, MOD_ROWS, 1, d), lambda i, f: (layer, 0, 0, k))

    in_specs, args = [], []
    if ns:
        in_specs.append(pl.BlockSpec(
            (tm // ns, d), lambda i, f: (jnp.minimum(i + 1, nm - 1) * ns + jnp.minimum(f, ns - 1), 0)))
        args.append(x2d)
    in_specs += [
        pl.BlockSpec((tm, d), lambda i, f: (i, 0)),
        pl.BlockSpec((1, d), lambda i, f: (0, 0)),
        mod_spec(3), mod_spec(4), mod_spec(5),
        pl.BlockSpec((d, tf), lambda i, f: (0, f)),
        pl.BlockSpec((tf, d), lambda i, f: (f, 0)),
    ]
    args += [x2d, g.reshape(1, d), mod4, mod4, mod4, w1, w2]
    if final_g is not None:
        in_specs.append(pl.BlockSpec((1, d), lambda i, f: (0, 0)))
        args.append(final_g.reshape(1, d))
    c_in, c_args, c_out, c_shapes = _cast_plumbing(casts, nm * nf, lambda i, f: i * nf + f)
    return pl.pallas_call(
        functools.partial(_mlp_kernel, nf=nf, final=final_g is not None, n_cast=len(casts), ns=ns, nm=nm,
                          mod_row0=mod_row0, tiles_per_mod=rows_per_mod // tm),
        grid=(nm, nf),
        in_specs=in_specs + c_in,
        out_specs=[pl.BlockSpec((tm, d), lambda i, f: (i, 0))] + c_out,
        out_shape=[jax.ShapeDtypeStruct((t, d), F32)] + c_shapes,
        scratch_shapes=[pltpu.VMEM((tm, d), BF16)] * (2 if ns else 1),
        compiler_params=_params(("arbitrary", "arbitrary")),
        name=name,
    )(*args, *c_args)


def _log_sigmoid(x):
    return jnp.minimum(x, 0.0) - jnp.log1p(jnp.exp(-jnp.abs(x)))


def _ret_kernel(*refs, seq, ch, unroll, has_init, emit_state):
    refs = list(refs)
    df_ref, db_ref, q_ref, k_ref, v_ref, g_ref = refs[:6]
    refs = refs[6:]
    s0f_ref, s0b_ref = (refs.pop(0), refs.pop(0)) if has_init else (None, None)
    y_ref = refs.pop(0)
    sff_ref, sfb_ref = (refs.pop(0), refs.pop(0)) if emit_state else (None, None)
    oc_ref, sf_ref, sb_ref = refs
    h = pl.program_id(1)
    n = seq // ch
    lgf = _log_sigmoid(jnp.full((ch, ch), df_ref[h], F32))
    lgb = _log_sigmoid(jnp.full((ch, ch), db_ref[h], F32))
    ii = lax.broadcasted_iota(jnp.int32, (ch, ch), 0)
    jj = lax.broadcasted_iota(jnp.int32, (ch, ch), 1)
    diff = (ii - jj).astype(F32)
    decay = (jnp.where(diff >= 0, jnp.exp(lgf * jnp.maximum(diff, 0.0)), 0.0)
             + jnp.where(diff <= 0, jnp.exp(lgb * jnp.maximum(-diff, 0.0)), 0.0))
    col = lax.broadcasted_iota(jnp.int32, (ch, 1), 0).astype(F32)
    lgf_c = _log_sigmoid(jnp.full((ch, 1), df_ref[h], F32))
    lgb_c = _log_sigmoid(jnp.full((ch, 1), db_ref[h], F32))
    dq_f = jnp.exp(lgf_c * (col + 1.0))
    dk_f = jnp.exp(lgf_c * (ch - 1.0 - col))
    dc_f = jnp.exp(_log_sigmoid(jnp.full((1, 1), df_ref[h], F32)) * float(ch))
    dq_b = jnp.exp(lgb_c * (ch - col))
    dk_b = jnp.exp(lgb_c * col)
    dc_b = jnp.exp(_log_sigmoid(jnp.full((1, 1), db_ref[h], F32)) * float(ch))
    tdims = (((0,), (0,)), ((), ()))
    ndims = (((1,), (1,)), ((), ()))

    if has_init:
        sf_ref[...] = s0f_ref[...]
        sb_ref[...] = s0b_ref[...]
    else:
        sf_ref[...] = jnp.zeros_like(sf_ref)
        sb_ref[...] = jnp.zeros_like(sb_ref)

    def rows(c):
        return pl.ds(pl.multiple_of(c * ch, ch), ch)

    def cross(c, s_ref, dq, dk, dc, first):
        r = rows(c)
        q = q_ref[r, :]
        k = k_ref[r, :]
        v = v_ref[r, :]
        s = s_ref[...]
        term = jnp.dot(q, s.astype(BF16), preferred_element_type=F32) * dq
        if first:
            oc_ref[r, :] = term
        else:
            oc_ref[r, :] += term
        kd = (k.astype(F32) * dk).astype(BF16)
        s_ref[...] = s * dc + lax.dot_general(kd, v, tdims, preferred_element_type=F32)

    def sweep(first):
        def step(t, carry):
            cross(t, sf_ref, dq_f, dk_f, dc_f, first)
            cross(n - 1 - t, sb_ref, dq_b, dk_b, dc_b, first)
            return carry
        return step

    half = n // 2
    if n % 2:
        oc_ref[pl.ds(half * ch, ch), :] = jnp.zeros((ch, oc_ref.shape[1]), F32)
    if half:
        lax.fori_loop(0, half, sweep(True), 0, unroll=min(unroll, half))
    lax.fori_loop(half, n, sweep(False), 0, unroll=min(unroll, n - half))
    if emit_state:
        sff_ref[...] = sf_ref[...]
        sfb_ref[...] = sb_ref[...]

    def finish(c, carry):
        r = rows(c)
        q = q_ref[r, :]
        scores = lax.dot_general(q, k_ref[r, :], ndims, preferred_element_type=F32) * decay
        o = jnp.dot(scores.astype(BF16), v_ref[r, :], preferred_element_type=F32) + oc_ref[r, :]
        y = o * lax.rsqrt(jnp.mean(o * o, axis=-1, keepdims=True) + NORM_EPS)
        hg = 0.5 * g_ref[r, :].astype(F32)
        y_ref[r, :] = (hg * (1.0 + jnp.tanh(hg)) * y).astype(y_ref.dtype)
        return carry

    lax.fori_loop(0, n, finish, 0, unroll=min(unroll, n))


def _ret_call(qkvg, decay_f, decay_b, init=None, *, batch, seq, ch, dk, dv, emit_state, name="retention"):
    heads = RET_HEADS
    smem = pl.BlockSpec(memory_space=pltpu.SMEM)
    state = pl.BlockSpec((None, None, dk, dv), lambda b, h: (b, h, 0, 0))
    state_shape = jax.ShapeDtypeStruct((batch, heads, dk, dv), F32)
    n_init = 0 if init is None else 2
    n_fin = 2 if emit_state else 0
    return pl.pallas_call(
        functools.partial(_ret_kernel, seq=seq, ch=ch, unroll=4, has_init=init is not None,
                          emit_state=emit_state),
        grid=(batch, heads),
        in_specs=[
            smem, smem,
            pl.BlockSpec((seq, dk), lambda b, h: (b, h)),
            pl.BlockSpec((seq, dk), lambda b, h: (b, heads + h)),
            pl.BlockSpec((seq, dv), lambda b, h: (b, heads + h)),
            pl.BlockSpec((seq, dv), lambda b, h: (b, 2 * heads + h)),
        ] + [state] * n_init,
        out_specs=[pl.BlockSpec((seq, dv), lambda b, h: (b, h))] + [state] * n_fin,
        out_shape=[jax.ShapeDtypeStruct((batch * seq, heads * dv), BF16)] + [state_shape] * n_fin,
        scratch_shapes=[
            pltpu.VMEM((seq, dv), F32),
            pltpu.VMEM((dk, dv), F32),
            pltpu.VMEM((dk, dv), F32),
        ],
        compiler_params=_params(("parallel", "arbitrary")),
        name=name,
    )(decay_f, decay_b, qkvg, qkvg, qkvg, qkvg, *(init or ()))


def _attn_kernel(sink_ref, q_ref, kp_ref, kc_ref, kn_ref, vp_ref, vc_ref, vn_ref, kx_ref, vx_ref,
                 o_ref, *, nq, tq, group):
    n = pl.program_id(1)
    dh = ATT_HEAD_DIM
    n_ctx = kx_ref.shape[0]
    win = tq + 2 * WINDOW
    rows = group * tq
    nk = win + n_ctx
    scale = dh ** -0.5
    assert tq == WINDOW
    rq = lax.broadcasted_iota(jnp.int32, (rows, WINDOW), 0) % tq
    ck = lax.broadcasted_iota(jnp.int32, (rows, WINDOW), 1)
    ok_prev = (ck >= rq) & (n > 0)
    ok_next = (ck <= rq) & (n < nq - 1)
    rcol = lax.broadcasted_iota(jnp.int32, (rows, 1), 0) // tq
    ndims = (((1,), (1,)), ((), ()))
    ones = jnp.ones((nk, dh), BF16)

    lane = lax.broadcasted_iota(jnp.int32, (tq, 2 * dh), 1)
    for kh in range(ATT_KV_HEADS):
        pair, e = divmod(kh, 2)
        mine = (lane // (dh // 2)) % 2 == e
        hs = slice(kh * dh, (kh + 1) * dh)
        ps = slice(pair * 2 * dh, (pair + 1) * 2 * dh)
        qs = jnp.concatenate(
            [jnp.where(mine, q_ref[:, (pair * group + g) * 2 * dh:(pair * group + g + 1) * 2 * dh], 0)
             for g in range(group)], axis=0)
        k_all = jnp.concatenate([kp_ref[:, ps], kc_ref[:, ps], kn_ref[:, ps], kx_ref[:, ps]], axis=0)
        v_all = jnp.concatenate([vp_ref[:, hs], vc_ref[:, hs], vn_ref[:, hs], vx_ref[:, hs]], axis=0)
        s = lax.dot_general(qs, k_all, ndims, preferred_element_type=F32) * (scale * LOG2E)
        s = jnp.concatenate([jnp.where(ok_prev, s[:, :WINDOW], NEG_INF), s[:, WINDOW:WINDOW + tq],
                             jnp.where(ok_next, s[:, WINDOW + tq:win], NEG_INF), s[:, win:]], axis=1)
        sink = jnp.zeros((rows, 1), F32)
        for g in range(group):
            sink = jnp.where(rcol == g, sink_ref[kh * group + g] * LOG2E, sink)
        m = jnp.maximum(jnp.max(s, axis=-1, keepdims=True), sink)
        p = jnp.exp2(s - m).astype(BF16)
        pv = jnp.dot(p, jnp.concatenate([v_all, ones], axis=1), preferred_element_type=F32)
        o = pv[:, :dh] / (pv[:, dh:dh + 1] + jnp.exp2(sink - m))
        for g in range(group):
            o_ref[:, (kh * group + g) * dh:(kh * group + g + 1) * dh] = (
                o[g * tq:(g + 1) * tq, :].astype(o_ref.dtype))


def _attn_call(qkv, kv_ctx, sink, *, batch, seq, n_ctx, tq, name="window_attn"):
    dh = ATT_HEAD_DIM
    kvw = ATT_KV_HEADS * dh
    qw = qkv.shape[1] - 2 * kvw
    group = qw // kvw
    nq = seq // tq
    wpb = seq // WINDOW
    step = tq // WINDOW
    k_blk = qw // kvw
    v_blk = k_blk + 1

    def prev_map(col):
        return lambda b, n: (b * wpb + jnp.maximum(n * step - 1, 0), col)

    def cur_map(col):
        return lambda b, n: (b * nq + n, col)

    def next_map(col):
        return lambda b, n: (b * wpb + jnp.minimum((n + 1) * step, wpb - 1), col)

    return pl.pallas_call(
        functools.partial(_attn_kernel, nq=nq, tq=tq, group=group),
        grid=(batch, nq),
        in_specs=[
            pl.BlockSpec(memory_space=pltpu.SMEM),
            pl.BlockSpec((tq, qw), lambda b, n: (b * nq + n, 0)),
            pl.BlockSpec((WINDOW, kvw), prev_map(k_blk)),
            pl.BlockSpec((tq, kvw), cur_map(k_blk)),
            pl.BlockSpec((WINDOW, kvw), next_map(k_blk)),
            pl.BlockSpec((WINDOW, kvw), prev_map(v_blk)),
            pl.BlockSpec((tq, kvw), cur_map(v_blk)),
            pl.BlockSpec((WINDOW, kvw), next_map(v_blk)),
            pl.BlockSpec((n_ctx, kvw), lambda b, n: (b, 0)),
            pl.BlockSpec((n_ctx, kvw), lambda b, n: (b, 1)),
        ],
        out_specs=pl.BlockSpec((tq, qw), lambda b, n: (b * nq + n, 0)),
        out_shape=jax.ShapeDtypeStruct((batch * seq, qw), BF16),
        compiler_params=_params(("parallel", "arbitrary")),
        name=name,
    )(sink, qkv, qkv, qkv, qkv, qkv, qkv, qkv, kv_ctx, kv_ctx)


def _rope_angles(pos, dim, base):
    inv_freq = base ** (-jnp.arange(0, dim, 2, dtype=F32) / dim)
    return pos.astype(F32)[:, None] * inv_freq[None, :]


def _axial_tables(seq):
    rows = jnp.arange(seq) // GRID_W
    cols = jnp.arange(seq) % GRID_W
    half = ATT_HEAD_DIM // 2
    ar = _rope_angles(rows, half, ROPE_BASE)
    ac = _rope_angles(cols, half, ROPE_BASE)
    cos = jnp.concatenate([jnp.cos(ar), jnp.cos(ac), jnp.cos(ar), jnp.cos(ac)], axis=-1)
    sin = jnp.concatenate([jnp.sin(ar), jnp.sin(ac), jnp.sin(ar), jnp.sin(ac)], axis=-1)
    return cos, sin


def _pair_layout_blocks(n_q_heads, n_kv_heads):
    group = n_q_heads // n_kv_heads

    def pair(a, b):
        return [4 * a, 4 * a + 2, 4 * b, 4 * b + 2, 4 * a + 1, 4 * a + 3, 4 * b + 1, 4 * b + 3]

    blocks = []
    for p in range(n_kv_heads // 2):
        for g in range(group):
            blocks += pair(2 * p * group + g, (2 * p + 1) * group + g)
    for p in range(n_kv_heads // 2):
        blocks += pair(n_q_heads + 2 * p, n_q_heads + 2 * p + 1)
    blocks += range(4 * (n_q_heads + n_kv_heads), 4 * (n_q_heads + 2 * n_kv_heads))
    return blocks


def kernel(x, c, ctx, c_ctx, ada_w, ada_b, norm_mix_g, norm_mlp_g, mlp_w1, mlp_w2, ret_w_in, ret_w_out,
           ret_decay_fwd, ret_decay_bwd, attn_w_in, attn_w_out, attn_sink, final_norm_g):
    batch, seq, d = x.shape
    n_ctx = ctx.shape[1]
    assert ada_w.shape[0] == 2 and ret_w_in.shape[0] == 1 and attn_w_in.shape[0] == 1
    assert batch + 1 <= MOD_ROWS
    ret_dk = d // RET_HEADS
    ret_dv = 2 * ret_dk
    qk_w = RET_HEADS * ret_dk
    att_qw = attn_w_out.shape[1]
    att_kvw = ATT_KV_HEADS * ATT_HEAD_DIM
    ctx_row = batch

    cond = jnp.concatenate([c, c_ctx[None, :], jnp.zeros((MOD_ROWS - batch - 1, d), F32)], axis=0)
    mod = _ada_call(cond, ada_w, ada_b)
    mod4 = mod.reshape(mod.shape[0], MOD_ROWS, 1, mod.shape[2])

    xl = x.reshape(batch * seq, d)
    xc = ctx.reshape(batch * n_ctx, d)
    t_ctx = batch * n_ctx
    lat = dict(rows_per_mod=seq, mod_row0=0)
    cx = dict(rows_per_mod=t_ctx, mod_row0=ctx_row)

    ret_ang = _rope_angles(jnp.arange(seq), ret_dk, ROPE_BASE)
    ret_cos, ret_sin = jnp.cos(ret_ang), jnp.sin(ret_ang)
    att_cos, att_sin = _axial_tables(seq)

    tm, tn, tn_ctx = 1024, 1024, 512
    k_scale = ret_dk ** -0.5
    tps = seq // tm
    one = jnp.ones((tm, 128), F32)
    zero = jnp.zeros((tm, 128), F32)

    def ret_tab_lat(i, j):
        q_tiles = qk_w // tn
        return jnp.where(j < q_tiles, i % tps, jnp.where(j < 2 * q_tiles, tps + i % tps, 2 * tps))

    def ret_tab_ctx(i, j):
        q_tiles = qk_w // tn_ctx
        return jnp.where((j >= q_tiles) & (j < 2 * q_tiles), 1, 0)

    assert t_ctx == tm
    ctx_tab = (jnp.concatenate([one, one * k_scale]), jnp.concatenate([zero, zero]), ret_tab_ctx)
    p_ctx, w_in = _proj_call(xc, norm_mix_g[0], mod4, 0, ret_w_in, tm=tm, tn=tn_ctx, rope="half",
                             tables=[ctx_tab], emit_w=True, name="ret_proj_ctx", **cx)
    assert ATT_KV_HEADS % 2 == 0
    quarter = ATT_HEAD_DIM // 4
    blocks = jnp.asarray(_pair_layout_blocks(att_qw // ATT_HEAD_DIM, ATT_KV_HEADS), jnp.int32)
    att_w_pair = jnp.take(attn_w_in.reshape(1, d, -1, quarter), blocks, axis=2).reshape(attn_w_in.shape)
    casts = [(ret_w_out, 0, 32), (mlp_w1, 0, 16), (mlp_w2, 0, 64), (att_w_pair, 0, 16), (attn_w_out, 0, 16)]
    lat_tab = (jnp.concatenate([ret_cos, ret_cos * k_scale, one]),
               jnp.concatenate([ret_sin, ret_sin * k_scale, zero]), ret_tab_lat)
    p_lat, w_out, w1, w2, att_w_in, att_w_out = _proj_call(
        xl, norm_mix_g[0], mod4, 0, w_in, tm=tm, tn=tn, rope="half", tables=[lat_tab],
        ns=8, casts=casts, name="ret_proj_lat", **lat)
    ret = dict(batch=batch, dk=ret_dk, dv=ret_dv)
    y_ctx, sc_f, sc_b = _ret_call(p_ctx, ret_decay_fwd[0], ret_decay_bwd[0], seq=n_ctx, ch=min(256, n_ctx),
                                  emit_state=True, name="retention_ctx", **ret)
    y_lat, = _ret_call(p_lat, ret_decay_fwd[0], ret_decay_bwd[0], (sc_f, sc_b), seq=seq, ch=256,
                       emit_state=False, name="retention_lat", **ret)
    xl = _out_call(y_lat, w_out, xl, mod4, 0, tm=1024, tn=1024, name="ret_out_lat", **lat)
    xc = _out_call(y_ctx, w_out, xc, mod4, 0, tm=t_ctx, tn=512, name="ret_out_ctx", **cx)
    xl, w1_next, w2_next = _mlp_call(xl, norm_mlp_g[0], mod4, 0, w1, w2, tm=512, tf=1024, ns=8,
                                     casts=[(mlp_w1, 1, 16), (mlp_w2, 1, 64)], name="mlp0_lat", **lat)
    xc, = _mlp_call(xc, norm_mlp_g[0], mod4, 0, w1, w2, tm=512, tf=1024, name="mlp0_ctx", **cx)

    kv_ctx, = _proj_call(xc, norm_mix_g[1], mod4, 1, att_w_in, tm=t_ctx, tn=512, n_off=att_qw // 512,
                         n_cols=2 * att_kvw, name="attn_proj_ctx", **cx)
    att_tn = 2 * att_kvw
    assert att_qw % att_tn == 0
    q_tiles = att_qw // att_tn
    att_c = jnp.concatenate([att_cos, one])
    att_s = jnp.concatenate([att_sin, zero])
    qkv, = _proj_call(xl, norm_mix_g[1], mod4, 1, att_w_in, tm=tm, tn=att_tn, rope="half",
                      tables=[(att_c, att_s, lambda i, j: i % tps),
                              (att_c, att_s, lambda i, j: jnp.where(j < q_tiles, i % tps, tps))],
                      ns=2, name="attn_proj_lat", **lat)
    o = _attn_call(qkv, kv_ctx, attn_sink[0], batch=batch, seq=seq, n_ctx=n_ctx, tq=128)
    xl = _out_call(o, att_w_out, xl, mod4, 1, tm=1024, tn=1024, name="attn_out_lat", **lat)
    xl, = _mlp_call(xl, norm_mlp_g[1], mod4, 1, w1_next, w2_next, tm=512, tf=1024, final_g=final_norm_g,
                    ns=8, name="mlp1_lat", **lat)
    return xl.reshape(batch, seq, d)
```

```python
import functools

import jax
import jax.numpy as jnp
from jax import lax
from jax.experimental import pallas as pl
from jax.experimental.pallas import tpu as pltpu

F32 = jnp.float32
BF16 = jnp.bfloat16

NORM_EPS = 1e-6
NEG_INF = -1e30
LOG2E = 1.4426950408889634
GRID_W = 64
ROPE_BASE = 10000.0

RET_HEADS = 8
ATT_HEAD_DIM = 128
ATT_KV_HEADS = 4
WINDOW = 128

MOD_ROWS = 8
VMEM_LIMIT = 56 * 1024 * 1024


def _params(semantics, vmem=VMEM_LIMIT):
    return pltpu.CompilerParams(dimension_semantics=semantics, vmem_limit_bytes=vmem)


def _norm_mod(x, g, shift, scale):
    ms = jnp.mean(x * x, axis=-1, keepdims=True)
    return x * lax.rsqrt(ms + NORM_EPS) * (g * (1.0 + scale)) + shift


def _ada_kernel(cond_ref, w_ref, b_ref, o_ref):
    c = cond_ref[...]
    s = c * jax.nn.sigmoid(c)
    o_ref[...] = jnp.dot(s.astype(BF16), w_ref[...].astype(BF16),
                         preferred_element_type=F32) + b_ref[...]


def _ada_call(cond, ada_w, ada_b, tn=1024):
    depth, d, n = ada_w.shape
    return pl.pallas_call(
        _ada_kernel,
        grid=(depth, n // tn),
        in_specs=[
            pl.BlockSpec((MOD_ROWS, d), lambda l, j: (0, 0)),
            pl.BlockSpec((None, d, tn), lambda l, j: (l, 0, j)),
            pl.BlockSpec((None, 1, tn), lambda l, j: (l, 0, j)),
        ],
        out_specs=pl.BlockSpec((None, MOD_ROWS, tn), lambda l, j: (l, 0, j)),
        out_shape=jax.ShapeDtypeStruct((depth, MOD_ROWS, n), F32),
        compiler_params=_params(("arbitrary", "arbitrary")),
        name="ada_mod",
    )(cond, ada_w, ada_b.reshape(depth, 1, n))


def _cast_plumbing(jobs, n_steps, step_of):
    in_specs, args, out_specs, out_shapes = [], [], [], []
    for arr, layer, rows, _ in jobs:
        r, c = arr.shape[-2:]
        nblk = r // rows
        assert n_steps >= nblk and r % rows == 0
        rep = n_steps // nblk

        def blk(i, j, rep=rep, nblk=nblk):
            return jnp.minimum(step_of(i, j) // rep, nblk - 1)

        in_specs.append(pl.BlockSpec((None, rows, c), lambda i, j, blk=blk, layer=layer: (layer, blk(i, j), 0)))
        args.append(arr)
        out_specs.append(pl.BlockSpec((rows, c), lambda i, j, blk=blk: (blk(i, j), 0)))
        out_shapes.append(jax.ShapeDtypeStruct((r, c), BF16))
    return in_specs, args, out_specs, out_shapes


def _rotate_store(acc, o_ref, rope, tabs):
    tn = acc.shape[1]
    if not rope:
        o_ref[...] = acc.astype(o_ref.dtype)
        return
    groups = tn // 256
    for hh in range(groups):
        cos_ref, sin_ref = tabs[hh * len(tabs) // groups]
        cos = cos_ref[...]
        sin = sin_ref[...]
        x1 = acc[:, hh * 256:hh * 256 + 128]
        x2 = acc[:, hh * 256 + 128:(hh + 1) * 256]
        o_ref[:, hh * 256:hh * 256 + 128] = (x1 * cos - x2 * sin).astype(o_ref.dtype)
        o_ref[:, hh * 256 + 128:(hh + 1) * 256] = (x2 * cos + x1 * sin).astype(o_ref.dtype)


def _cast_block(v, order):
    if order is not None:
        q = v.shape[1] // len(order)
        v = jnp.concatenate([v[:, b * q:(b + 1) * q] for b in order], axis=1)
    return v.astype(BF16)


def _proj_kernel(*refs, rope, n_tab, cast_orders, emit_w, ns, nm, mod_row0, tiles_per_mod):
    refs = list(refs)
    xs_ref = refs.pop(0) if ns else None
    x_ref, g_ref, shift_ref, scale_ref, w_ref = refs[:5]
    refs = refs[5:]
    tabs = [(refs[2 * k], refs[2 * k + 1]) for k in range(n_tab)]
    refs = refs[2 * n_tab:]
    n_cast = len(cast_orders)
    cast_src, refs = refs[:n_cast], refs[n_cast:]
    o_ref = refs.pop(0)
    wout_ref = refs.pop(0) if emit_w else None
    cast_dst, a_refs = refs[:n_cast], refs[n_cast:]
    i = pl.program_id(0)
    j = pl.program_id(1)

    def prologue(x, tile):
        b = mod_row0 + tile // tiles_per_mod
        return _norm_mod(x, g_ref[...], shift_ref[b], scale_ref[b]).astype(BF16)

    def body(a_cur, a_nxt):
        w = w_ref[...]
        if emit_w:
            w = w.astype(BF16)
            wout_ref[...] = w
        acc = jnp.dot(a_cur[...], w, preferred_element_type=F32)
        _rotate_store(acc, o_ref, rope, tabs)
        for src, dst, order in zip(cast_src, cast_dst, cast_orders):
            dst[...] = _cast_block(src[...], order)
        if a_nxt is not None:
            @pl.when(j < ns)
            def _():
                rs = xs_ref.shape[0]
                a_nxt[pl.ds(pl.multiple_of(j * rs, rs), rs), :] = prologue(
                    xs_ref[...], jnp.minimum(i + 1, nm - 1))

    if ns == 0:
        @pl.when(j == 0)
        def _():
            a_refs[0][...] = prologue(x_ref[...], i)

        body(a_refs[0], None)
        return

    @pl.when((i == 0) & (j == 0))
    def _():
        a_refs[0][...] = prologue(x_ref[...], 0)

    @pl.when(i % 2 == 0)
    def _():
        body(a_refs[0], a_refs[1])

    @pl.when(i % 2 == 1)
    def _():
        body(a_refs[1], a_refs[0])


def _proj_call(x2d, g, mod4, layer, w, *, tm, tn, rows_per_mod, mod_row0, n_off=0, n_cols=None,
               rope=None, tables=(), ns=0, emit_w=False, casts=(), name="proj"):
    t, d = x2d.shape
    n_cols = w.shape[-1] if n_cols is None else n_cols
    nt = n_cols // tn
    nm = t // tm
    assert t % tm == 0 and n_cols % tn == 0 and rows_per_mod % tm == 0
    assert not emit_w or nm == 1
    assert ns == 0 or (tm % ns == 0 and ns <= nt and nm > 1)

    def mod_spec(k):
        return pl.BlockSpec((None, MOD_ROWS, 1, d), lambda i, j: (layer, 0, 0, k))

    in_specs, args = [], []
    if ns:
        in_specs.append(pl.BlockSpec(
            (tm // ns, d), lambda i, j: (jnp.minimum(i + 1, nm - 1) * ns + jnp.minimum(j, ns - 1), 0)))
        in_specs.append(pl.BlockSpec((tm, d), lambda i, j: (0, 0), pipeline_mode=pl.Buffered(1)))
        args += [x2d, x2d]
    else:
        in_specs.append(pl.BlockSpec((tm, d), lambda i, j: (i, 0)))
        args.append(x2d)
    in_specs += [pl.BlockSpec((1, d), lambda i, j: (0, 0)), mod_spec(0), mod_spec(1)]
    args += [g.reshape(1, d), mod4, mod4]
    if emit_w:
        in_specs.append(pl.BlockSpec((None, d, tn), lambda i, j: (0, 0, n_off + j)))
    else:
        in_specs.append(pl.BlockSpec((d, tn), lambda i, j: (0, n_off + j)))
    args.append(w)
    for cos, sin, tab_fn in tables:
        in_specs += [pl.BlockSpec((tm, 128), lambda i, j, tab_fn=tab_fn: (tab_fn(i, j), 0))] * 2
        args += [cos, sin]
    c_in, c_args, c_out, c_shapes = _cast_plumbing(casts, nm * nt, lambda i, j: i * nt + j)
    out_specs = [pl.BlockSpec((tm, tn), lambda i, j: (i, j))]
    out_shapes = [jax.ShapeDtypeStruct((t, n_cols), BF16)]
    if emit_w:
        out_specs.append(pl.BlockSpec((d, tn), lambda i, j: (0, j)))
        out_shapes.append(jax.ShapeDtypeStruct((d, n_cols), BF16))
    return pl.pallas_call(
        functools.partial(_proj_kernel, rope=rope, n_tab=len(tables),
                          cast_orders=tuple(job[3] for job in casts), emit_w=emit_w, ns=ns, nm=nm,
                          mod_row0=mod_row0, tiles_per_mod=rows_per_mod // tm),
        grid=(nm, nt),
        in_specs=in_specs + c_in,
        out_specs=out_specs + c_out,
        out_shape=out_shapes + c_shapes,
        scratch_shapes=[pltpu.VMEM((tm, d), BF16)] * (2 if ns else 1),
        compiler_params=_params(("arbitrary", "arbitrary")),
        name=name,
    )(*args, *c_args)


def _out_kernel(y_ref, w_ref, x_ref, gate_ref, o_ref):
    acc = jnp.dot(y_ref[...], w_ref[...], preferred_element_type=F32)
    o_ref[...] = x_ref[...] + gate_ref[...] * acc


def _out_call(y, w, x2d, mod4, layer, *, tm, tn, rows_per_mod, mod_row0, name="out_proj"):
    t, k = y.shape
    d = w.shape[1]
    tiles_per_mod = rows_per_mod // tm
    gate_blk0 = 2 * d // tn
    return pl.pallas_call(
        _out_kernel,
        grid=(t // tm, d // tn),
        in_specs=[
            pl.BlockSpec((tm, k), lambda i, j: (i, 0)),
            pl.BlockSpec((k, tn), lambda i, j: (0, j)),
            pl.BlockSpec((tm, tn), lambda i, j: (i, j)),
            pl.BlockSpec((None, None, 1, tn),
                         lambda i, j: (layer, mod_row0 + i // tiles_per_mod, 0, gate_blk0 + j)),
        ],
        out_specs=pl.BlockSpec((tm, tn), lambda i, j: (i, j)),
        out_shape=jax.ShapeDtypeStruct((t, d), F32),
        compiler_params=_params(("parallel", "arbitrary")),
        name=name,
    )(y, w, x2d, mod4)


def _mlp_kernel(*refs, nf, final, n_cast, ns, nm, mod_row0, tiles_per_mod):
    refs = list(refs)
    xs_ref = refs.pop(0) if ns else None
    x_ref, g_ref, shift_ref, scale_ref, gate_ref, w1_ref, w2_ref = refs[:7]
    refs = refs[7:]
    fg_ref = refs.pop(0) if final else None
    cast_src, refs = refs[:n_cast], refs[n_cast:]
    o_ref = refs.pop(0)
    cast_dst, a_refs = refs[:n_cast], refs[n_cast:]
    i = pl.program_id(0)
    f = pl.program_id(1)

    def mod_row(tile):
        return mod_row0 + tile // tiles_per_mod

    def prologue(x, tile):
        b = mod_row(tile)
        return _norm_mod(x, g_ref[...], shift_ref[b], scale_ref[b]).astype(BF16)

    def body(a_cur, a_nxt):
        @pl.when(f == 0)
        def _():
            o_ref[...] = jnp.zeros_like(o_ref)

        h = jnp.dot(a_cur[...], w1_ref[...], preferred_element_type=F32)
        h = jnp.square(jnp.maximum(h, 0.0)).astype(BF16)
        o_ref[...] += jnp.dot(h, w2_ref[...], preferred_element_type=F32)
        for src, dst in zip(cast_src, cast_dst):
            dst[...] = src[...].astype(BF16)
        if a_nxt is not None:
            rs = xs_ref.shape[0]
            rows = pl.ds(pl.multiple_of(jnp.minimum(f, ns - 1) * rs, rs), rs)
            a_nxt[rows, :] = prologue(xs_ref[...], jnp.minimum(i + 1, nm - 1))

        @pl.when(f == nf - 1)
        def _():
            r = x_ref[...] + gate_ref[mod_row(i)] * o_ref[...]
            if final:
                ms = jnp.mean(r * r, axis=-1, keepdims=True)
                r = r * lax.rsqrt(ms + NORM_EPS) * fg_ref[...]
            o_ref[...] = r

    if ns == 0:
        @pl.when(f == 0)
        def _():
            a_refs[0][...] = prologue(x_ref[...], i)

        body(a_refs[0], None)
        return

    @pl.when((i == 0) & (f == 0))
    def _():
        a_refs[0][...] = prologue(x_ref[...], 0)

    @pl.when(i % 2 == 0)
    def _():
        body(a_refs[0], a_refs[1])

    @pl.when(i % 2 == 1)
    def _():
        body(a_refs[1], a_refs[0])


def _mlp_call(x2d, g, mod4, layer, w1, w2, *, tm, tf, rows_per_mod, mod_row0, final_g=None, ns=0, casts=(),
              name="mlp"):
    t, d = x2d.shape
    ff = w1.shape[1]
    nf = ff // tf
    nm = t // tm
    assert t % tm == 0 and ff % tf == 0 and rows_per_mod % tm == 0
    assert ns == 0 or (tm % ns == 0 and ns <= nf and nm > 1)

    def mod_spec(k):
        return pl.BlockSpec((None, MOD_ROWS, 1, d), lambda i, f: (layer, 0, 0, k))

    in_specs, args = [], []
    if ns:
        in_specs.append(pl.BlockSpec(
            (tm // ns, d), lambda i, f: (jnp.minimum(i + 1, nm - 1) * ns + jnp.minimum(f, ns - 1), 0)))
        args.append(x2d)
    in_specs += [
        pl.BlockSpec((tm, d), lambda i, f: (i, 0)),
        pl.BlockSpec((1, d), lambda i, f: (0, 0)),
        mod_spec(3), mod_spec(4), mod_spec(5),
        pl.BlockSpec((d, tf), lambda i, f: (0, f)),
        pl.BlockSpec((tf, d), lambda i, f: (f, 0)),
    ]
    args += [x2d, g.reshape(1, d), mod4, mod4, mod4, w1, w2]
    if final_g is not None:
        in_specs.append(pl.BlockSpec((1, d), lambda i, f: (0, 0)))
        args.append(final_g.reshape(1, d))
    c_in, c_args, c_out, c_shapes = _cast_plumbing(casts, nm * nf, lambda i, f: i * nf + f)
    return pl.pallas_call(
        functools.partial(_mlp_kernel, nf=nf, final=final_g is not None, n_cast=len(casts), ns=ns, nm=nm,
                          mod_row0=mod_row0, tiles_per_mod=rows_per_mod // tm),
        grid=(nm, nf),
        in_specs=in_specs + c_in,
        out_specs=[pl.BlockSpec((tm, d), lambda i, f: (i, 0))] + c_out,
        out_shape=[jax.ShapeDtypeStruct((t, d), F32)] + c_shapes,
        scratch_shapes=[pltpu.VMEM((tm, d), BF16)] * (2 if ns else 1),
        compiler_params=_params(("arbitrary", "arbitrary")),
        name=name,
    )(*args, *c_args)


def _log_sigmoid(x):
    return jnp.minimum(x, 0.0) - jnp.log1p(jnp.exp(-jnp.abs(x)))


def _ret_kernel(*refs, seq, ch, unroll, has_init, emit_state):
    refs = list(refs)
    df_ref, db_ref, q_ref, k_ref, v_ref, g_ref = refs[:6]
    refs = refs[6:]
    s0f_ref, s0b_ref = (refs.pop(0), refs.pop(0)) if has_init else (None, None)
    y_ref = refs.pop(0)
    sff_ref, sfb_ref = (refs.pop(0), refs.pop(0)) if emit_state else (None, None)
    oc_ref, sf_ref, sb_ref = refs
    h = pl.program_id(1)
    n = seq // ch
    lgf = _log_sigmoid(jnp.full((ch, ch), df_ref[h], F32))
    lgb = _log_sigmoid(jnp.full((ch, ch), db_ref[h], F32))
    ii = lax.broadcasted_iota(jnp.int32, (ch, ch), 0)
    jj = lax.broadcasted_iota(jnp.int32, (ch, ch), 1)
    diff = (ii - jj).astype(F32)
    decay = (jnp.where(diff >= 0, jnp.exp(lgf * jnp.maximum(diff, 0.0)), 0.0)
             + jnp.where(diff <= 0, jnp.exp(lgb * jnp.maximum(-diff, 0.0)), 0.0))
    col = lax.broadcasted_iota(jnp.int32, (ch, 1), 0).astype(F32)
    lgf_c = _log_sigmoid(jnp.full((ch, 1), df_ref[h], F32))
    lgb_c = _log_sigmoid(jnp.full((ch, 1), db_ref[h], F32))
    dq_f = jnp.exp(lgf_c * (col + 1.0))
    dk_f = jnp.exp(lgf_c * (ch - 1.0 - col))
    dc_f = jnp.exp(_log_sigmoid(jnp.full((1, 1), df_ref[h], F32)) * float(ch))
    dq_b = jnp.exp(lgb_c * (ch - col))
    dk_b = jnp.exp(lgb_c * col)
    dc_b = jnp.exp(_log_sigmoid(jnp.full((1, 1), db_ref[h], F32)) * float(ch))
    tdims = (((0,), (0,)), ((), ()))
    ndims = (((1,), (1,)), ((), ()))

    if has_init:
        sf_ref[...] = s0f_ref[...]
        sb_ref[...] = s0b_ref[...]
    else:
        sf_ref[...] = jnp.zeros_like(sf_ref)
        sb_ref[...] = jnp.zeros_like(sb_ref)

    def rows(c):
        return pl.ds(pl.multiple_of(c * ch, ch), ch)

    def cross(c, s_ref, dq, dk, dc, first):
        r = rows(c)
        q = q_ref[r, :]
        k = k_ref[r, :]
        v = v_ref[r, :]
        s = s_ref[...]
        term = jnp.dot(q, s.astype(BF16), preferred_element_type=F32) * dq
        if first:
            oc_ref[r, :] = term
        else:
            oc_ref[r, :] += term
        kd = (k.astype(F32) * dk).astype(BF16)
        s_ref[...] = s * dc + lax.dot_general(kd, v, tdims, preferred_element_type=F32)

    def sweep(first):
        def step(t, carry):
            cross(t, sf_ref, dq_f, dk_f, dc_f, first)
            cross(n - 1 - t, sb_ref, dq_b, dk_b, dc_b, first)
            return carry
        return step

    half = n // 2
    if n % 2:
        oc_ref[pl.ds(half * ch, ch), :] = jnp.zeros((ch, oc_ref.shape[1]), F32)
    if half:
        lax.fori_loop(0, half, sweep(True), 0, unroll=min(unroll, half))
    lax.fori_loop(half, n, sweep(False), 0, unroll=min(unroll, n - half))
    if emit_state:
        sff_ref[...] = sf_ref[...]
        sfb_ref[...] = sb_ref[...]

    def finish(c, carry):
        r = rows(c)
        q = q_ref[r, :]
        scores = lax.dot_general(q, k_ref[r, :], ndims, preferred_element_type=F32) * decay
        o = jnp.dot(scores.astype(BF16), v_ref[r, :], preferred_element_type=F32) + oc_ref[r, :]
        y = o * lax.rsqrt(jnp.mean(o * o, axis=-1, keepdims=True) + NORM_EPS)
        hg = 0.5 * g_ref[r, :].astype(F32)
        y_ref[r, :] = (hg * (1.0 + jnp.tanh(hg)) * y).astype(y_ref.dtype)
        return carry

    lax.fori_loop(0, n, finish, 0, unroll=min(unroll, n))


def _ret_call(qkvg, decay_f, decay_b, init=None, *, batch, seq, ch, dk, dv, emit_state, name="retention"):
    heads = RET_HEADS
    smem = pl.BlockSpec(memory_space=pltpu.SMEM)
    state = pl.BlockSpec((None, None, dk, dv), lambda b, h: (b, h, 0, 0))
    state_shape = jax.ShapeDtypeStruct((batch, heads, dk, dv), F32)
    n_init = 0 if init is None else 2
    n_fin = 2 if emit_state else 0
    return pl.pallas_call(
        functools.partial(_ret_kernel, seq=seq, ch=ch, unroll=4, has_init=init is not None,
                          emit_state=emit_state),
        grid=(batch, heads),
        in_specs=[
            smem, smem,
            pl.BlockSpec((seq, dk), lambda b, h: (b, h)),
            pl.BlockSpec((seq, dk), lambda b, h: (b, heads + h)),
            pl.BlockSpec((seq, dv), lambda b, h: (b, heads + h)),
            pl.BlockSpec((seq, dv), lambda b, h: (b, 2 * heads + h)),
        ] + [state] * n_init,
        out_specs=[pl.BlockSpec((seq, dv), lambda b, h: (b, h))] + [state] * n_fin,
        out_shape=[jax.ShapeDtypeStruct((batch * seq, heads * dv), BF16)] + [state_shape] * n_fin,
        scratch_shapes=[
            pltpu.VMEM((seq, dv), F32),
            pltpu.VMEM((dk, dv), F32),
            pltpu.VMEM((dk, dv), F32),
        ],
        compiler_params=_params(("parallel", "arbitrary")),
        name=name,
    )(decay_f, decay_b, qkvg, qkvg, qkvg, qkvg, *(init or ()))


def _attn_kernel(sink_ref, q_ref, kp_ref, kc_ref, kn_ref, vp_ref, vc_ref, vn_ref, kx_ref, vx_ref,
                 o_ref, *, nq, tq, group):
    n = pl.program_id(1)
    dh = ATT_HEAD_DIM
    n_ctx = kx_ref.shape[0]
    win = tq + 2 * WINDOW
    rows = group * tq
    nk = win + n_ctx
    scale = dh ** -0.5
    assert tq == WINDOW
    rq = lax.broadcasted_iota(jnp.int32, (rows, WINDOW), 0) % tq
    ck = lax.broadcasted_iota(jnp.int32, (rows, WINDOW), 1)
    ok_prev = (ck >= rq) & (n > 0)
    ok_next = (ck <= rq) & (n < nq - 1)
    rcol = lax.broadcasted_iota(jnp.int32, (rows, 1), 0) // tq
    ndims = (((1,), (1,)), ((), ()))
    ones = jnp.ones((nk, dh), BF16)

    lane = lax.broadcasted_iota(jnp.int32, (tq, 2 * dh), 1)
    for kh in range(ATT_KV_HEADS):
        pair, e = divmod(kh, 2)
        mine = (lane // (dh // 2)) % 2 == e
        hs = slice(kh * dh, (kh + 1) * dh)
        ps = slice(pair * 2 * dh, (pair + 1) * 2 * dh)
        qs = jnp.concatenate(
            [jnp.where(mine, q_ref[:, (pair * group + g) * 2 * dh:(pair * group + g + 1) * 2 * dh], 0)
             for g in range(group)], axis=0)
        k_all = jnp.concatenate([kp_ref[:, ps], kc_ref[:, ps], kn_ref[:, ps], kx_ref[:, ps]], axis=0)
        v_all = jnp.concatenate([vp_ref[:, hs], vc_ref[:, hs], vn_ref[:, hs], vx_ref[:, hs]], axis=0)
        s = lax.dot_general(qs, k_all, ndims, preferred_element_type=F32) * (scale * LOG2E)
        s = jnp.concatenate([jnp.where(ok_prev, s[:, :WINDOW], NEG_INF), s[:, WINDOW:WINDOW + tq],
                             jnp.where(ok_next, s[:, WINDOW + tq:win], NEG_INF), s[:, win:]], axis=1)
        sink = jnp.zeros((rows, 1), F32)
        for g in range(group):
            sink = jnp.where(rcol == g, sink_ref[kh * group + g] * LOG2E, sink)
        m = jnp.maximum(jnp.max(s, axis=-1, keepdims=True), sink)
        p = jnp.exp2(s - m).astype(BF16)
        pv = jnp.dot(p, jnp.concatenate([v_all, ones], axis=1), preferred_element_type=F32)
        o = pv[:, :dh] / (pv[:, dh:dh + 1] + jnp.exp2(sink - m))
        for g in range(group):
            o_ref[:, (kh * group + g) * dh:(kh * group + g + 1) * dh] = (
                o[g * tq:(g + 1) * tq, :].astype(o_ref.dtype))


def _attn_call(qkv, kv_ctx, sink, *, batch, seq, n_ctx, tq, name="window_attn"):
    dh = ATT_HEAD_DIM
    kvw = ATT_KV_HEADS * dh
    qw = qkv.shape[1] - 2 * kvw
    group = qw // kvw
    nq = seq // tq
    wpb = seq // WINDOW
    step = tq // WINDOW
    k_blk = qw // kvw
    v_blk = k_blk + 1

    def prev_map(col):
        return lambda b, n: (b * wpb + jnp.maximum(n * step - 1, 0), col)

    def cur_map(col):
        return lambda b, n: (b * nq + n, col)

    def next_map(col):
        return lambda b, n: (b * wpb + jnp.minimum((n + 1) * step, wpb - 1), col)

    return pl.pallas_call(
        functools.partial(_attn_kernel, nq=nq, tq=tq, group=group),
        grid=(batch, nq),
        in_specs=[
            pl.BlockSpec(memory_space=pltpu.SMEM),
            pl.BlockSpec((tq, qw), lambda b, n: (b * nq + n, 0)),
            pl.BlockSpec((WINDOW, kvw), prev_map(k_blk)),
            pl.BlockSpec((tq, kvw), cur_map(k_blk)),
            pl.BlockSpec((WINDOW, kvw), next_map(k_blk)),
            pl.BlockSpec((WINDOW, kvw), prev_map(v_blk)),
            pl.BlockSpec((tq, kvw), cur_map(v_blk)),
            pl.BlockSpec((WINDOW, kvw), next_map(v_blk)),
            pl.BlockSpec((n_ctx, kvw), lambda b, n: (b, 0)),
            pl.BlockSpec((n_ctx, kvw), lambda b, n: (b, 1)),
        ],
        out_specs=pl.BlockSpec((tq, qw), lambda b, n: (b * nq + n, 0)),
        out_shape=jax.ShapeDtypeStruct((batch * seq, qw), BF16),
        compiler_params=_params(("parallel", "arbitrary")),
        name=name,
    )(sink, qkv, qkv, qkv, qkv, qkv, qkv, qkv, kv_ctx, kv_ctx)


def _rope_angles(pos, dim, base):
    inv_freq = base ** (-jnp.arange(0, dim, 2, dtype=F32) / dim)
    return pos.astype(F32)[:, None] * inv_freq[None, :]


def _axial_tables(seq):
    rows = jnp.arange(seq) // GRID_W
    cols = jnp.arange(seq) % GRID_W
    half = ATT_HEAD_DIM // 2
    ar = _rope_angles(rows, half, ROPE_BASE)
    ac = _rope_angles(cols, half, ROPE_BASE)
    cos = jnp.concatenate([jnp.cos(ar), jnp.cos(ac), jnp.cos(ar), jnp.cos(ac)], axis=-1)
    sin = jnp.concatenate([jnp.sin(ar), jnp.sin(ac), jnp.sin(ar), jnp.sin(ac)], axis=-1)
    return cos, sin


def _pair_layout_blocks(n_q_heads, n_kv_heads):
    group = n_q_heads // n_kv_heads

    def pair(a, b):
        return [4 * a, 4 * a + 2, 4 * b, 4 * b + 2, 4 * a + 1, 4 * a + 3, 4 * b + 1, 4 * b + 3]

    blocks = []
    for p in range(n_kv_heads // 2):
        for g in range(group):
            blocks += pair(2 * p * group + g, (2 * p + 1) * group + g)
    for p in range(n_kv_heads // 2):
        blocks += pair(n_q_heads + 2 * p, n_q_heads + 2 * p + 1)
    blocks += range(4 * (n_q_heads + n_kv_heads), 4 * (n_q_heads + 2 * n_kv_heads))
    return blocks


def kernel(x, c, ctx, c_ctx, ada_w, ada_b, norm_mix_g, norm_mlp_g, mlp_w1, mlp_w2, ret_w_in, ret_w_out,
           ret_decay_fwd, ret_decay_bwd, attn_w_in, attn_w_out, attn_sink, final_norm_g):
    batch, seq, d = x.shape
    n_ctx = ctx.shape[1]
    assert ada_w.shape[0] == 2 and ret_w_in.shape[0] == 1 and attn_w_in.shape[0] == 1
    assert batch + 1 <= MOD_ROWS
    ret_dk = d // RET_HEADS
    ret_dv = 2 * ret_dk
    qk_w = RET_HEADS * ret_dk
    att_qw = attn_w_out.shape[1]
    att_kvw = ATT_KV_HEADS * ATT_HEAD_DIM
    ctx_row = batch

    cond = jnp.concatenate([c, c_ctx[None, :], jnp.zeros((MOD_ROWS - batch - 1, d), F32)], axis=0)
    mod = _ada_call(cond, ada_w, ada_b)
    mod4 = mod.reshape(mod.shape[0], MOD_ROWS, 1, mod.shape[2])

    xl = x.reshape(batch * seq, d)
    xc = ctx.reshape(batch * n_ctx, d)
    t_ctx = batch * n_ctx
    lat = dict(rows_per_mod=seq, mod_row0=0)
    cx = dict(rows_per_mod=t_ctx, mod_row0=ctx_row)

    ret_ang = _rope_angles(jnp.arange(seq), ret_dk, ROPE_BASE)
    ret_cos, ret_sin = jnp.cos(ret_ang), jnp.sin(ret_ang)
    att_cos, att_sin = _axial_tables(seq)

    tm, tn, tn_ctx = 1024, 1024, 512
    k_scale = ret_dk ** -0.5
    tps = seq // tm
    one = jnp.ones((tm, 128), F32)
    zero = jnp.zeros((tm, 128), F32)

    def ret_tab_lat(i, j):
        q_tiles = qk_w // tn
        return jnp.where(j < q_tiles, i % tps, jnp.where(j < 2 * q_tiles, tps + i % tps, 2 * tps))

    def ret_tab_ctx(i, j):
        q_tiles = qk_w // tn_ctx
        return jnp.where((j >= q_tiles) & (j < 2 * q_tiles), 1, 0)

    assert t_ctx == tm
    ctx_tab = (jnp.concatenate([one, one * k_scale]), jnp.concatenate([zero, zero]), ret_tab_ctx)
    p_ctx, w_in = _proj_call(xc, norm_mix_g[0], mod4, 0, ret_w_in, tm=tm, tn=tn_ctx, rope="half",
                             tables=[ctx_tab], emit_w=True, name="ret_proj_ctx", **cx)
    assert ATT_KV_HEADS % 2 == 0
    pair_order = tuple(_pair_layout_blocks(att_qw // ATT_HEAD_DIM, ATT_KV_HEADS))
    casts = [(ret_w_out, 0, 32, None), (mlp_w1, 0, 16, None), (mlp_w2, 0, 64, None),
             (attn_w_in, 0, 16, pair_order), (attn_w_out, 0, 16, None)]
    lat_tab = (jnp.concatenate([ret_cos, ret_cos * k_scale, one]),
               jnp.concatenate([ret_sin, ret_sin * k_scale, zero]), ret_tab_lat)
    p_lat, w_out, w1, w2, att_w_in, att_w_out = _proj_call(
        xl, norm_mix_g[0], mod4, 0, w_in, tm=tm, tn=tn, rope="half", tables=[lat_tab],
        ns=8, casts=casts, name="ret_proj_lat", **lat)
    ret = dict(batch=batch, dk=ret_dk, dv=ret_dv)
    y_ctx, sc_f, sc_b = _ret_call(p_ctx, ret_decay_fwd[0], ret_decay_bwd[0], seq=n_ctx, ch=min(256, n_ctx),
                                  emit_state=True, name="retention_ctx", **ret)
    y_lat, = _ret_call(p_lat, ret_decay_fwd[0], ret_decay_bwd[0], (sc_f, sc_b), seq=seq, ch=256,
                       emit_state=False, name="retention_lat", **ret)
    xl = _out_call(y_lat, w_out, xl, mod4, 0, tm=1024, tn=1024, name="ret_out_lat", **lat)
    xc = _out_call(y_ctx, w_out, xc, mod4, 0, tm=t_ctx, tn=512, name="ret_out_ctx", **cx)
    xl, w1_next, w2_next = _mlp_call(xl, norm_mlp_g[0], mod4, 0, w1, w2, tm=512, tf=1024, ns=8,
                                     casts=[(mlp_w1, 1, 16, None), (mlp_w2, 1, 64, None)], name="mlp0_lat",
                                     **lat)
    xc, = _mlp_call(xc, norm_mlp_g[0], mod4, 0, w1, w2, tm=512, tf=1024, name="mlp0_ctx", **cx)

    kv_ctx, = _proj_call(xc, norm_mix_g[1], mod4, 1, att_w_in, tm=t_ctx, tn=512, n_off=att_qw // 512,
                         n_cols=2 * att_kvw, name="attn_proj_ctx", **cx)
    att_tn = 2 * att_kvw
    assert att_qw % att_tn == 0
    q_tiles = att_qw // att_tn
    att_c = jnp.concatenate([att_cos, one])
    att_s = jnp.concatenate([att_sin, zero])
    qkv, = _proj_call(xl, norm_mix_g[1], mod4, 1, att_w_in, tm=tm, tn=att_tn, rope="half",
                      tables=[(att_c, att_s, lambda i, j: i % tps),
                              (att_c, att_s, lambda i, j: jnp.where(j < q_tiles, i % tps, tps))],
                      ns=2, name="attn_proj_lat", **lat)
    o = _attn_call(qkv, kv_ctx, attn_sink[0], batch=batch, seq=seq, n_ctx=n_ctx, tq=128)
    xl = _out_call(o, att_w_out, xl, mod4, 1, tm=1024, tn=1024, name="attn_out_lat", **lat)
    xl, = _mlp_call(xl, norm_mlp_g[1], mod4, 1, w1_next, w2_next, tm=512, tf=1024, final_g=final_norm_g,
                    ns=8, name="mlp1_lat", **lat)
    return xl.reshape(batch, seq, d)
```

```python
import functools

import jax
import numpy as np
import jax.numpy as jnp
from jax import lax
from jax.experimental import pallas as pl
from jax.experimental.pallas import tpu as pltpu

F32 = jnp.float32
BF16 = jnp.bfloat16

NORM_EPS = 1e-6
NEG_INF = -1e30
LOG2E = 1.4426950408889634
GRID_W = 64
ROPE_BASE = 10000.0

RET_HEADS = 8
ATT_HEAD_DIM = 128
ATT_KV_HEADS = 4
WINDOW = 128
LANES = 128

MOD_ROWS = 8
VMEM_LIMIT = 56 * 1024 * 1024


def _params(semantics, vmem=VMEM_LIMIT):
    return pltpu.CompilerParams(dimension_semantics=semantics, vmem_limit_bytes=vmem)


def _norm_mod(x, g, shift, scale):
    ms = jnp.mean(x * x, axis=-1, keepdims=True)
    return x * lax.rsqrt(ms + NORM_EPS) * (g * (1.0 + scale)) + shift


def _ada_kernel(cond_ref, w_ref, b_ref, o_ref):
    c = cond_ref[...]
    s = c * jax.nn.sigmoid(c)
    o_ref[...] = jnp.dot(s.astype(BF16), w_ref[...].astype(BF16),
                         preferred_element_type=F32) + b_ref[...]


def _ada_call(cond, ada_w, ada_b, tn=1024):
    depth, d, n = ada_w.shape
    return pl.pallas_call(
        _ada_kernel,
        grid=(depth, n // tn),
        in_specs=[
            pl.BlockSpec((MOD_ROWS, d), lambda l, j: (0, 0)),
            pl.BlockSpec((None, d, tn), lambda l, j: (l, 0, j)),
            pl.BlockSpec((None, 1, tn), lambda l, j: (l, 0, j)),
        ],
        out_specs=pl.BlockSpec((None, MOD_ROWS, tn), lambda l, j: (l, 0, j)),
        out_shape=jax.ShapeDtypeStruct((depth, MOD_ROWS, n), F32),
        compiler_params=_params(("arbitrary", "arbitrary")),
        name="ada_mod",
    )(cond, ada_w, ada_b.reshape(depth, 1, n))


def _cast_plumbing(jobs, n_steps, step_of):
    in_specs, args, out_specs, out_shapes = [], [], [], []
    for arr, layer, rows, _ in jobs:
        r, c = arr.shape[-2:]
        nblk = r // rows
        assert n_steps >= nblk and r % rows == 0
        rep = n_steps // nblk

        def blk(i, j, rep=rep, nblk=nblk):
            return jnp.minimum(step_of(i, j) // rep, nblk - 1)

        in_specs.append(pl.BlockSpec((None, rows, c), lambda i, j, blk=blk, layer=layer: (layer, blk(i, j), 0)))
        args.append(arr)
        out_specs.append(pl.BlockSpec((rows, c), lambda i, j, blk=blk: (blk(i, j), 0)))
        out_shapes.append(jax.ShapeDtypeStruct((r, c), BF16))
    return in_specs, args, out_specs, out_shapes


def _rotate_store(acc, o_ref, rope, tabs):
    tn = acc.shape[1]
    if not rope:
        o_ref[...] = acc.astype(o_ref.dtype)
        return
    groups = tn // (2 * LANES)
    for hh in range(groups):
        cos_ref, sin_ref = tabs[hh * len(tabs) // groups]
        cos = cos_ref[...]
        sin = sin_ref[...]
        lo = slice(2 * hh * LANES, (2 * hh + 1) * LANES)
        hi = slice((2 * hh + 1) * LANES, (2 * hh + 2) * LANES)
        x1 = acc[:, lo]
        x2 = acc[:, hi]
        o_ref[:, lo] = (x1 * cos - x2 * sin).astype(o_ref.dtype)
        o_ref[:, hi] = (x2 * cos + x1 * sin).astype(o_ref.dtype)


def _cast_block(v, order):
    if order is not None:
        q = v.shape[1] // len(order)
        v = jnp.concatenate([v[:, b * q:(b + 1) * q] for b in order], axis=1)
    return v.astype(BF16)


def _proj_kernel(*refs, rope, n_tab, cast_orders, emit_w, ns, nm, mod_row0, tiles_per_mod):
    refs = list(refs)
    xs_ref = refs.pop(0) if ns else None
    x_ref, g_ref, shift_ref, scale_ref, w_ref = refs[:5]
    refs = refs[5:]
    tabs = [(refs[2 * k], refs[2 * k + 1]) for k in range(n_tab)]
    refs = refs[2 * n_tab:]
    n_cast = len(cast_orders)
    cast_src, refs = refs[:n_cast], refs[n_cast:]
    o_ref = refs.pop(0)
    wout_ref = refs.pop(0) if emit_w else None
    cast_dst, a_refs = refs[:n_cast], refs[n_cast:]
    i = pl.program_id(0)
    j = pl.program_id(1)

    def prologue(x, tile):
        b = mod_row0 + tile // tiles_per_mod
        return _norm_mod(x, g_ref[...], shift_ref[b], scale_ref[b]).astype(BF16)

    def body(a_cur, a_nxt):
        w = w_ref[...]
        if emit_w:
            w = w.astype(BF16)
            wout_ref[...] = w
        acc = jnp.dot(a_cur[...], w, preferred_element_type=F32)
        _rotate_store(acc, o_ref, rope, tabs)
        for src, dst, order in zip(cast_src, cast_dst, cast_orders):
            dst[...] = _cast_block(src[...], order)
        if a_nxt is not None:
            @pl.when(j < ns)
            def _():
                rs = xs_ref.shape[0]
                a_nxt[pl.ds(pl.multiple_of(j * rs, rs), rs), :] = prologue(
                    xs_ref[...], jnp.minimum(i + 1, nm - 1))

    if ns == 0:
        @pl.when(j == 0)
        def _():
            a_refs[0][...] = prologue(x_ref[...], i)

        body(a_refs[0], None)
        return

    @pl.when((i == 0) & (j == 0))
    def _():
        a_refs[0][...] = prologue(x_ref[...], 0)

    @pl.when(i % 2 == 0)
    def _():
        body(a_refs[0], a_refs[1])

    @pl.when(i % 2 == 1)
    def _():
        body(a_refs[1], a_refs[0])


def _proj_call(x2d, g, mod4, layer, w, *, tm, tn, rows_per_mod, mod_row0, n_off=0, n_cols=None,
               rope=None, tables=(), ns=0, emit_w=False, casts=(), name="proj"):
    t, d = x2d.shape
    n_cols = w.shape[-1] if n_cols is None else n_cols
    nt = n_cols // tn
    nm = t // tm
    assert t % tm == 0 and n_cols % tn == 0 and rows_per_mod % tm == 0
    assert not emit_w or nm == 1
    assert ns == 0 or (tm % ns == 0 and ns <= nt and nm > 1)

    def mod_spec(k):
        return pl.BlockSpec((None, MOD_ROWS, 1, d), lambda i, j: (layer, 0, 0, k))

    in_specs, args = [], []
    if ns:
        in_specs.append(pl.BlockSpec(
            (tm // ns, d), lambda i, j: (jnp.minimum(i + 1, nm - 1) * ns + jnp.minimum(j, ns - 1), 0)))
        in_specs.append(pl.BlockSpec((tm, d), lambda i, j: (0, 0), pipeline_mode=pl.Buffered(1)))
        args += [x2d, x2d]
    else:
        in_specs.append(pl.BlockSpec((tm, d), lambda i, j: (i, 0)))
        args.append(x2d)
    in_specs += [pl.BlockSpec((1, d), lambda i, j: (0, 0)), mod_spec(0), mod_spec(1)]
    args += [g.reshape(1, d), mod4, mod4]
    if emit_w:
        in_specs.append(pl.BlockSpec((None, d, tn), lambda i, j: (0, 0, n_off + j)))
    else:
        in_specs.append(pl.BlockSpec((d, tn), lambda i, j: (0, n_off + j)))
    args.append(w)
    for cos, sin, tab_fn in tables:
        in_specs += [pl.BlockSpec((tm, LANES), lambda i, j, tab_fn=tab_fn: (tab_fn(i, j), 0))] * 2
        args += [cos, sin]
    c_in, c_args, c_out, c_shapes = _cast_plumbing(casts, nm * nt, lambda i, j: i * nt + j)
    out_specs = [pl.BlockSpec((tm, tn), lambda i, j: (i, j))]
    out_shapes = [jax.ShapeDtypeStruct((t, n_cols), BF16)]
    if emit_w:
        out_specs.append(pl.BlockSpec((d, tn), lambda i, j: (0, j)))
        out_shapes.append(jax.ShapeDtypeStruct((d, n_cols), BF16))
    return pl.pallas_call(
        functools.partial(_proj_kernel, rope=rope, n_tab=len(tables),
                          cast_orders=tuple(job[3] for job in casts), emit_w=emit_w, ns=ns, nm=nm,
                          mod_row0=mod_row0, tiles_per_mod=rows_per_mod // tm),
        grid=(nm, nt),
        in_specs=in_specs + c_in,
        out_specs=out_specs + c_out,
        out_shape=out_shapes + c_shapes,
        scratch_shapes=[pltpu.VMEM((tm, d), BF16)] * (2 if ns else 1),
        compiler_params=_params(("arbitrary", "arbitrary")),
        name=name,
    )(*args, *c_args)


def _out_kernel(y_ref, w_ref, x_ref, gate_ref, o_ref):
    acc = jnp.dot(y_ref[...], w_ref[...], preferred_element_type=F32)
    o_ref[...] = x_ref[...] + gate_ref[...] * acc


def _out_call(y, w, x2d, mod4, layer, *, tm, tn, rows_per_mod, mod_row0, name="out_proj"):
    t, k = y.shape
    d = w.shape[1]
    tiles_per_mod = rows_per_mod // tm
    gate_blk0 = 2 * d // tn
    return pl.pallas_call(
        _out_kernel,
        grid=(t // tm, d // tn),
        in_specs=[
            pl.BlockSpec((tm, k), lambda i, j: (i, 0)),
            pl.BlockSpec((k, tn), lambda i, j: (0, j)),
            pl.BlockSpec((tm, tn), lambda i, j: (i, j)),
            pl.BlockSpec((None, None, 1, tn),
                         lambda i, j: (layer, mod_row0 + i // tiles_per_mod, 0, gate_blk0 + j)),
        ],
        out_specs=pl.BlockSpec((tm, tn), lambda i, j: (i, j)),
        out_shape=jax.ShapeDtypeStruct((t, d), F32),
        compiler_params=_params(("parallel", "arbitrary")),
        name=name,
    )(y, w, x2d, mod4)


def _mlp_kernel(*refs, nf, final, n_cast, ns, nm, mod_row0, tiles_per_mod):
    refs = list(refs)
    xs_ref = refs.pop(0) if ns else None
    x_ref, g_ref, shift_ref, scale_ref, gate_ref, w1_ref, w2_ref = refs[:7]
    refs = refs[7:]
    fg_ref = refs.pop(0) if final else None
    cast_src, refs = refs[:n_cast], refs[n_cast:]
    o_ref = refs.pop(0)
    cast_dst, a_refs = refs[:n_cast], refs[n_cast:]
    i = pl.program_id(0)
    f = pl.program_id(1)

    def mod_row(tile):
        return mod_row0 + tile // tiles_per_mod

    def prologue(x, tile):
        b = mod_row(tile)
        return _norm_mod(x, g_ref[...], shift_ref[b], scale_ref[b]).astype(BF16)

    def body(a_cur, a_nxt):
        @pl.when(f == 0)
        def _():
            o_ref[...] = jnp.zeros_like(o_ref)

        h = jnp.dot(a_cur[...], w1_ref[...], preferred_element_type=F32)
        h = jnp.square(jnp.maximum(h, 0.0)).astype(BF16)
        o_ref[...] += jnp.dot(h, w2_ref[...], preferred_element_type=F32)
        for src, dst in zip(cast_src, cast_dst):
            dst[...] = src[...].astype(BF16)
        if a_nxt is not None:
            rs = xs_ref.shape[0]
            rows = pl.ds(pl.multiple_of(jnp.minimum(f, ns - 1) * rs, rs), rs)
            a_nxt[rows, :] = prologue(xs_ref[...], jnp.minimum(i + 1, nm - 1))

        @pl.when(f == nf - 1)
        def _():
            r = x_ref[...] + gate_ref[mod_row(i)] * o_ref[...]
            if final:
                ms = jnp.mean(r * r, axis=-1, keepdims=True)
                r = r * lax.rsqrt(ms + NORM_EPS) * fg_ref[...]
            o_ref[...] = r

    if ns == 0:
        @pl.when(f == 0)
        def _():
            a_refs[0][...] = prologue(x_ref[...], i)

        body(a_refs[0], None)
        return

    @pl.when((i == 0) & (f == 0))
    def _():
        a_refs[0][...] = prologue(x_ref[...], 0)

    @pl.when(i % 2 == 0)
    def _():
        body(a_refs[0], a_refs[1])

    @pl.when(i % 2 == 1)
    def _():
        body(a_refs[1], a_refs[0])


def _mlp_call(x2d, g, mod4, layer, w1, w2, *, tm, tf, rows_per_mod, mod_row0, final_g=None, ns=0, casts=(),
              name="mlp"):
    t, d = x2d.shape
    ff = w1.shape[1]
    nf = ff // tf
    nm = t // tm
    assert t % tm == 0 and ff % tf == 0 and rows_per_mod % tm == 0
    assert ns == 0 or (tm % ns == 0 and ns <= nf and nm > 1)

    def mod_spec(k):
        return pl.BlockSpec((None, MOD_ROWS, 1, d), lambda i, f: (layer, 0, 0, k))

    in_specs, args = [], []
    if ns:
        in_specs.append(pl.BlockSpec(
            (tm // ns, d), lambda i, f: (jnp.minimum(i + 1, nm - 1) * ns + jnp.minimum(f, ns - 1), 0)))
        args.append(x2d)
    in_specs += [
        pl.BlockSpec((tm, d), lambda i, f: (i, 0)),
        pl.BlockSpec((1, d), lambda i, f: (0, 0)),
        mod_spec(3), mod_spec(4), mod_spec(5),
        pl.BlockSpec((d, tf), lambda i, f: (0, f)),
        pl.BlockSpec((tf, d), lambda i, f: (f, 0)),
    ]
    args += [x2d, g.reshape(1, d), mod4, mod4, mod4, w1, w2]
    if final_g is not None:
        in_specs.append(pl.BlockSpec((1, d), lambda i, f: (0, 0)))
        args.append(final_g.reshape(1, d))
    c_in, c_args, c_out, c_shapes = _cast_plumbing(casts, nm * nf, lambda i, f: i * nf + f)
    return pl.pallas_call(
        functools.partial(_mlp_kernel, nf=nf, final=final_g is not None, n_cast=len(casts), ns=ns, nm=nm,
                          mod_row0=mod_row0, tiles_per_mod=rows_per_mod // tm),
        grid=(nm, nf),
        in_specs=in_specs + c_in,
        out_specs=[pl.BlockSpec((tm, d), lambda i, f: (i, 0))] + c_out,
        out_shape=[jax.ShapeDtypeStruct((t, d), F32)] + c_shapes,
        scratch_shapes=[pltpu.VMEM((tm, d), BF16)] * (2 if ns else 1),
        compiler_params=_params(("arbitrary", "arbitrary")),
        name=name,
    )(*args, *c_args)


def _log_sigmoid(x):
    return jnp.minimum(x, 0.0) - jnp.log1p(jnp.exp(-jnp.abs(x)))


def _ret_kernel(*refs, seq, ch, unroll, has_init, emit_state):
    refs = list(refs)
    df_ref, db_ref, q_ref, k_ref, v_ref, g_ref = refs[:6]
    refs = refs[6:]
    s0f_ref, s0b_ref = (refs.pop(0), refs.pop(0)) if has_init else (None, None)
    y_ref = refs.pop(0)
    sff_ref, sfb_ref = (refs.pop(0), refs.pop(0)) if emit_state else (None, None)
    oc_ref, sf_ref, sb_ref = refs
    h = pl.program_id(1)
    n = seq // ch
    lgf = _log_sigmoid(jnp.full((ch, ch), df_ref[h], F32))
    lgb = _log_sigmoid(jnp.full((ch, ch), db_ref[h], F32))
    ii = lax.broadcasted_iota(jnp.int32, (ch, ch), 0)
    jj = lax.broadcasted_iota(jnp.int32, (ch, ch), 1)
    diff = (ii - jj).astype(F32)
    decay = (jnp.where(diff >= 0, jnp.exp(lgf * jnp.maximum(diff, 0.0)), 0.0)
             + jnp.where(diff <= 0, jnp.exp(lgb * jnp.maximum(-diff, 0.0)), 0.0))
    col = lax.broadcasted_iota(jnp.int32, (ch, 1), 0).astype(F32)
    lgf_c = _log_sigmoid(jnp.full((ch, 1), df_ref[h], F32))
    lgb_c = _log_sigmoid(jnp.full((ch, 1), db_ref[h], F32))
    dq_f = jnp.exp(lgf_c * (col + 1.0))
    dk_f = jnp.exp(lgf_c * (ch - 1.0 - col))
    dc_f = jnp.exp(_log_sigmoid(jnp.full((1, 1), df_ref[h], F32)) * float(ch))
    dq_b = jnp.exp(lgb_c * (ch - col))
    dk_b = jnp.exp(lgb_c * col)
    dc_b = jnp.exp(_log_sigmoid(jnp.full((1, 1), db_ref[h], F32)) * float(ch))
    tdims = (((0,), (0,)), ((), ()))
    ndims = (((1,), (1,)), ((), ()))

    if has_init:
        sf_ref[...] = s0f_ref[...]
        sb_ref[...] = s0b_ref[...]
    else:
        sf_ref[...] = jnp.zeros_like(sf_ref)
        sb_ref[...] = jnp.zeros_like(sb_ref)

    def rows(c):
        return pl.ds(pl.multiple_of(c * ch, ch), ch)

    def cross(c, s_ref, dq, dk, dc, first):
        r = rows(c)
        q = q_ref[r, :]
        k = k_ref[r, :]
        v = v_ref[r, :]
        s = s_ref[...]
        term = jnp.dot(q, s.astype(BF16), preferred_element_type=F32) * dq
        if first:
            oc_ref[r, :] = term
        else:
            oc_ref[r, :] += term
        kd = (k.astype(F32) * dk).astype(BF16)
        s_ref[...] = s * dc + lax.dot_general(kd, v, tdims, preferred_element_type=F32)

    def sweep(first):
        def step(t, carry):
            cross(t, sf_ref, dq_f, dk_f, dc_f, first)
            cross(n - 1 - t, sb_ref, dq_b, dk_b, dc_b, first)
            return carry
        return step

    half = n // 2
    if n % 2:
        oc_ref[pl.ds(half * ch, ch), :] = jnp.zeros((ch, oc_ref.shape[1]), F32)
    if half:
        lax.fori_loop(0, half, sweep(True), 0, unroll=min(unroll, half))
    lax.fori_loop(half, n, sweep(False), 0, unroll=min(unroll, n - half))
    if emit_state:
        sff_ref[...] = sf_ref[...]
        sfb_ref[...] = sb_ref[...]

    def finish(c, carry):
        r = rows(c)
        q = q_ref[r, :]
        scores = lax.dot_general(q, k_ref[r, :], ndims, preferred_element_type=F32) * decay
        o = jnp.dot(scores.astype(BF16), v_ref[r, :], preferred_element_type=F32) + oc_ref[r, :]
        y = o * lax.rsqrt(jnp.mean(o * o, axis=-1, keepdims=True) + NORM_EPS)
        hg = 0.5 * g_ref[r, :].astype(F32)
        y_ref[r, :] = (hg * (1.0 + jnp.tanh(hg)) * y).astype(y_ref.dtype)
        return carry

    lax.fori_loop(0, n, finish, 0, unroll=min(unroll, n))


def _ret_call(qkvg, decay_f, decay_b, init=None, *, batch, seq, ch, dk, dv, emit_state, name="retention"):
    heads = RET_HEADS
    smem = pl.BlockSpec(memory_space=pltpu.SMEM)
    state = pl.BlockSpec((None, None, dk, dv), lambda b, h: (b, h, 0, 0))
    state_shape = jax.ShapeDtypeStruct((batch, heads, dk, dv), F32)
    n_init = 0 if init is None else 2
    n_fin = 2 if emit_state else 0
    return pl.pallas_call(
        functools.partial(_ret_kernel, seq=seq, ch=ch, unroll=4, has_init=init is not None,
                          emit_state=emit_state),
        grid=(batch, heads),
        in_specs=[
            smem, smem,
            pl.BlockSpec((seq, dk), lambda b, h: (b, h)),
            pl.BlockSpec((seq, dk), lambda b, h: (b, heads + h)),
            pl.BlockSpec((seq, dv), lambda b, h: (b, heads + h)),
            pl.BlockSpec((seq, dv), lambda b, h: (b, 2 * heads + h)),
        ] + [state] * n_init,
        out_specs=[pl.BlockSpec((seq, dv), lambda b, h: (b, h))] + [state] * n_fin,
        out_shape=[jax.ShapeDtypeStruct((batch * seq, heads * dv), BF16)] + [state_shape] * n_fin,
        scratch_shapes=[
            pltpu.VMEM((seq, dv), F32),
            pltpu.VMEM((dk, dv), F32),
            pltpu.VMEM((dk, dv), F32),
        ],
        compiler_params=_params(("parallel", "arbitrary")),
        name=name,
    )(decay_f, decay_b, qkvg, qkvg, qkvg, qkvg, *(init or ()))


def _attn_kernel(sink_ref, q_ref, kp_ref, kc_ref, kn_ref, vp_ref, vc_ref, vn_ref, kx_ref, vx_ref,
                 o_ref, *, nq, tq, group):
    n = pl.program_id(1)
    dh = ATT_HEAD_DIM
    n_ctx = kx_ref.shape[0]
    win = tq + 2 * WINDOW
    rows = group * tq
    nk = win + n_ctx
    scale = dh ** -0.5
    assert tq == WINDOW
    rq = lax.broadcasted_iota(jnp.int32, (rows, WINDOW), 0) % tq
    ck = lax.broadcasted_iota(jnp.int32, (rows, WINDOW), 1)
    ok_prev = (ck >= rq) & (n > 0)
    ok_next = (ck <= rq) & (n < nq - 1)
    rcol = lax.broadcasted_iota(jnp.int32, (rows, 1), 0) // tq
    ndims = (((1,), (1,)), ((), ()))
    ones = jnp.ones((nk, dh), BF16)

    lane = lax.broadcasted_iota(jnp.int32, (tq, 2 * dh), 1)
    for kh in range(ATT_KV_HEADS):
        pair, e = divmod(kh, 2)
        mine = (lane // (dh // 2)) % 2 == e
        hs = slice(kh * dh, (kh + 1) * dh)
        ps = slice(pair * 2 * dh, (pair + 1) * 2 * dh)
        qs = jnp.concatenate(
            [jnp.where(mine, q_ref[:, (pair * group + g) * 2 * dh:(pair * group + g + 1) * 2 * dh], 0)
             for g in range(group)], axis=0)
        k_all = jnp.concatenate([kp_ref[:, ps], kc_ref[:, ps], kn_ref[:, ps], kx_ref[:, ps]], axis=0)
        v_all = jnp.concatenate([vp_ref[:, hs], vc_ref[:, hs], vn_ref[:, hs], vx_ref[:, hs]], axis=0)
        s = lax.dot_general(qs, k_all, ndims, preferred_element_type=F32) * (scale * LOG2E)
        s = jnp.concatenate([jnp.where(ok_prev, s[:, :WINDOW], NEG_INF), s[:, WINDOW:WINDOW + tq],
                             jnp.where(ok_next, s[:, WINDOW + tq:win], NEG_INF), s[:, win:]], axis=1)
        sink = jnp.zeros((rows, 1), F32)
        for g in range(group):
            sink = jnp.where(rcol == g, sink_ref[kh * group + g] * LOG2E, sink)
        m = jnp.maximum(jnp.max(s, axis=-1, keepdims=True), sink)
        p = jnp.exp2(s - m).astype(BF16)
        pv = jnp.dot(p, jnp.concatenate([v_all, ones], axis=1), preferred_element_type=F32)
        o = pv[:, :dh] / (pv[:, dh:dh + 1] + jnp.exp2(sink - m))
        for g in range(group):
            o_ref[:, (kh * group + g) * dh:(kh * group + g + 1) * dh] = (
                o[g * tq:(g + 1) * tq, :].astype(o_ref.dtype))


def _attn_call(qkv, kv_ctx, sink, *, batch, seq, n_ctx, tq, name="window_attn"):
    dh = ATT_HEAD_DIM
    kvw = ATT_KV_HEADS * dh
    qw = qkv.shape[1] - 2 * kvw
    group = qw // kvw
    nq = seq // tq
    wpb = seq // WINDOW
    step = tq // WINDOW
    k_blk = qw // kvw
    v_blk = k_blk + 1

    def prev_map(col):
        return lambda b, n: (b * wpb + jnp.maximum(n * step - 1, 0), col)

    def cur_map(col):
        return lambda b, n: (b * nq + n, col)

    def next_map(col):
        return lambda b, n: (b * wpb + jnp.minimum((n + 1) * step, wpb - 1), col)

    return pl.pallas_call(
        functools.partial(_attn_kernel, nq=nq, tq=tq, group=group),
        grid=(batch, nq),
        in_specs=[
            pl.BlockSpec(memory_space=pltpu.SMEM),
            pl.BlockSpec((tq, qw), lambda b, n: (b * nq + n, 0)),
            pl.BlockSpec((WINDOW, kvw), prev_map(k_blk)),
            pl.BlockSpec((tq, kvw), cur_map(k_blk)),
            pl.BlockSpec((WINDOW, kvw), next_map(k_blk)),
            pl.BlockSpec((WINDOW, kvw), prev_map(v_blk)),
            pl.BlockSpec((tq, kvw), cur_map(v_blk)),
            pl.BlockSpec((WINDOW, kvw), next_map(v_blk)),
            pl.BlockSpec((n_ctx, kvw), lambda b, n: (b, 0)),
            pl.BlockSpec((n_ctx, kvw), lambda b, n: (b, 1)),
        ],
        out_specs=pl.BlockSpec((tq, qw), lambda b, n: (b * nq + n, 0)),
        out_shape=jax.ShapeDtypeStruct((batch * seq, qw), BF16),
        compiler_params=_params(("parallel", "arbitrary")),
        name=name,
    )(sink, qkv, qkv, qkv, qkv, qkv, qkv, qkv, kv_ctx, kv_ctx)


def _rope_angles(pos, dim, base):
    inv_freq = base ** (-np.arange(0, dim, 2, dtype=np.float64) / dim)
    return pos.astype(np.float64)[:, None] * inv_freq[None, :]


def _table(cos_parts, sin_parts, tab_fn):
    return (jnp.asarray(np.concatenate(cos_parts), F32), jnp.asarray(np.concatenate(sin_parts), F32), tab_fn)


def _axial_tables(seq):
    rows = np.arange(seq) // GRID_W
    cols = np.arange(seq) % GRID_W
    half = ATT_HEAD_DIM // 2
    ar = _rope_angles(rows, half, ROPE_BASE)
    ac = _rope_angles(cols, half, ROPE_BASE)
    cos = np.concatenate([np.cos(ar), np.cos(ac), np.cos(ar), np.cos(ac)], axis=-1)
    sin = np.concatenate([np.sin(ar), np.sin(ac), np.sin(ar), np.sin(ac)], axis=-1)
    return cos, sin


def _pair_layout_blocks(n_q_heads, n_kv_heads):
    group = n_q_heads // n_kv_heads

    def pair(a, b):
        return [4 * a, 4 * a + 2, 4 * b, 4 * b + 2, 4 * a + 1, 4 * a + 3, 4 * b + 1, 4 * b + 3]

    blocks = []
    for p in range(n_kv_heads // 2):
        for g in range(group):
            blocks += pair(2 * p * group + g, (2 * p + 1) * group + g)
    for p in range(n_kv_heads // 2):
        blocks += pair(n_q_heads + 2 * p, n_q_heads + 2 * p + 1)
    blocks += range(4 * (n_q_heads + n_kv_heads), 4 * (n_q_heads + 2 * n_kv_heads))
    return blocks


def kernel(x, c, ctx, c_ctx, ada_w, ada_b, norm_mix_g, norm_mlp_g, mlp_w1, mlp_w2, ret_w_in, ret_w_out,
           ret_decay_fwd, ret_decay_bwd, attn_w_in, attn_w_out, attn_sink, final_norm_g):
    batch, seq, d = x.shape
    n_ctx = ctx.shape[1]
    assert ada_w.shape[0] == 2 and ret_w_in.shape[0] == 1 and attn_w_in.shape[0] == 1
    assert batch + 1 <= MOD_ROWS
    ret_dk = d // RET_HEADS
    ret_dv = 2 * ret_dk
    qk_w = RET_HEADS * ret_dk
    att_qw = attn_w_out.shape[1]
    att_kvw = ATT_KV_HEADS * ATT_HEAD_DIM
    ctx_row = batch

    cond = jnp.concatenate([c, c_ctx[None, :], jnp.zeros((MOD_ROWS - batch - 1, d), F32)], axis=0)
    mod = _ada_call(cond, ada_w, ada_b)
    mod4 = mod.reshape(mod.shape[0], MOD_ROWS, 1, mod.shape[2])

    xl = x.reshape(batch * seq, d)
    xc = ctx.reshape(batch * n_ctx, d)
    t_ctx = batch * n_ctx
    lat = dict(rows_per_mod=seq, mod_row0=0)
    cx = dict(rows_per_mod=t_ctx, mod_row0=ctx_row)

    ret_ang = _rope_angles(np.arange(seq), ret_dk, ROPE_BASE)
    ret_cos, ret_sin = np.cos(ret_ang), np.sin(ret_ang)
    att_cos, att_sin = _axial_tables(seq)

    tm, tn, tn_ctx = 1024, 1024, 512
    k_scale = ret_dk ** -0.5
    tps = seq // tm
    one = np.ones((tm, LANES))
    zero = np.zeros((tm, LANES))

    def ret_tab_lat(i, j):
        q_tiles = qk_w // tn
        return jnp.where(j < q_tiles, i % tps, jnp.where(j < 2 * q_tiles, tps + i % tps, 2 * tps))

    def ret_tab_ctx(i, j):
        q_tiles = qk_w // tn_ctx
        return jnp.where((j >= q_tiles) & (j < 2 * q_tiles), 1, 0)

    assert t_ctx == tm
    ctx_tab = _table([one, one * k_scale], [zero, zero], ret_tab_ctx)
    p_ctx, w_in = _proj_call(xc, norm_mix_g[0], mod4, 0, ret_w_in, tm=tm, tn=tn_ctx, rope="half",
                             tables=[ctx_tab], emit_w=True, name="ret_proj_ctx", **cx)
    assert ATT_KV_HEADS % 2 == 0
    pair_order = tuple(_pair_layout_blocks(att_qw // ATT_HEAD_DIM, ATT_KV_HEADS))
    casts = [(ret_w_out, 0, 32, None), (mlp_w1, 0, 16, None), (mlp_w2, 0, 64, None),
             (attn_w_in, 0, 16, pair_order), (attn_w_out, 0, 16, None)]
    lat_tab = _table([ret_cos, ret_cos * k_scale, one], [ret_sin, ret_sin * k_scale, zero], ret_tab_lat)
    p_lat, w_out, w1, w2, att_w_in, att_w_out = _proj_call(
        xl, norm_mix_g[0], mod4, 0, w_in, tm=tm, tn=tn, rope="half", tables=[lat_tab],
        ns=8, casts=casts, name="ret_proj_lat", **lat)
    ret = dict(batch=batch, dk=ret_dk, dv=ret_dv)
    y_ctx, sc_f, sc_b = _ret_call(p_ctx, ret_decay_fwd[0], ret_decay_bwd[0], seq=n_ctx, ch=min(256, n_ctx),
                                  emit_state=True, name="retention_ctx", **ret)
    y_lat, = _ret_call(p_lat, ret_decay_fwd[0], ret_decay_bwd[0], (sc_f, sc_b), seq=seq, ch=256,
                       emit_state=False, name="retention_lat", **ret)
    xl = _out_call(y_lat, w_out, xl, mod4, 0, tm=1024, tn=1024, name="ret_out_lat", **lat)
    xc = _out_call(y_ctx, w_out, xc, mod4, 0, tm=t_ctx, tn=512, name="ret_out_ctx", **cx)
    xl, w1_next, w2_next = _mlp_call(xl, norm_mlp_g[0], mod4, 0, w1, w2, tm=512, tf=1024, ns=8,
                                     casts=[(mlp_w1, 1, 16, None), (mlp_w2, 1, 64, None)], name="mlp0_lat",
                                     **lat)
    xc, = _mlp_call(xc, norm_mlp_g[0], mod4, 0, w1, w2, tm=512, tf=1024, name="mlp0_ctx", **cx)

    kv_ctx, = _proj_call(xc, norm_mix_g[1], mod4, 1, att_w_in, tm=t_ctx, tn=512, n_off=att_qw // 512,
                         n_cols=2 * att_kvw, name="attn_proj_ctx", **cx)
    att_tn = 2 * att_kvw
    assert att_qw % att_tn == 0
    q_tiles = att_qw // att_tn
    qkv, = _proj_call(xl, norm_mix_g[1], mod4, 1, att_w_in, tm=tm, tn=att_tn, rope="half",
                      tables=[_table([att_cos, one], [att_sin, zero], lambda i, j: i % tps),
                              _table([att_cos, one], [att_sin, zero],
                                     lambda i, j: jnp.where(j < q_tiles, i % tps, tps))],
                      ns=2, name="attn_proj_lat", **lat)
    o = _attn_call(qkv, kv_ctx, attn_sink[0], batch=batch, seq=seq, n_ctx=n_ctx, tq=WINDOW)
    xl = _out_call(o, att_w_out, xl, mod4, 1, tm=1024, tn=1024, name="attn_out_lat", **lat)
    xl, = _mlp_call(xl, norm_mlp_g[1], mod4, 1, w1_next, w2_next, tm=512, tf=1024, final_g=final_norm_g,
                    ns=8, name="mlp1_lat", **lat)
    return xl.reshape(batch, seq, d)
```

```python
import functools

import jax
import numpy as np
import jax.numpy as jnp
from jax import lax
from jax.experimental import pallas as pl
from jax.experimental.pallas import tpu as pltpu

F32 = jnp.float32
BF16 = jnp.bfloat16

NORM_EPS = 1e-6
NEG_INF = -1e30
LOG2E = 1.4426950408889634
GRID_W = 64
ROPE_BASE = 10000.0

RET_HEADS = 8
ATT_HEAD_DIM = 128
ATT_KV_HEADS = 4
WINDOW = 128
LANES = 128

MOD_ROWS = 8
VMEM_LIMIT = 56 * 1024 * 1024
MLP_VMEM_LIMIT = 64 * 1024 * 1024 - 256 * 1024
X_LEAD = 2


def _params(semantics, vmem=VMEM_LIMIT):
    return pltpu.CompilerParams(dimension_semantics=semantics, vmem_limit_bytes=vmem)


def _norm_mod(x, g, shift, scale):
    ms = jnp.mean(x * x, axis=-1, keepdims=True)
    return x * lax.rsqrt(ms + NORM_EPS) * (g * (1.0 + scale)) + shift


def _ada_kernel(cond_ref, w_ref, b_ref, o_ref):
    c = cond_ref[...]
    s = c * jax.nn.sigmoid(c)
    o_ref[...] = jnp.dot(s.astype(BF16), w_ref[...].astype(BF16),
                         preferred_element_type=F32) + b_ref[...]


def _ada_call(cond, ada_w, ada_b, tn=1024):
    depth, d, n = ada_w.shape
    return pl.pallas_call(
        _ada_kernel,
        grid=(depth, n // tn),
        in_specs=[
            pl.BlockSpec((MOD_ROWS, d), lambda l, j: (0, 0)),
            pl.BlockSpec((None, d, tn), lambda l, j: (l, 0, j)),
            pl.BlockSpec((None, 1, tn), lambda l, j: (l, 0, j)),
        ],
        out_specs=pl.BlockSpec((None, MOD_ROWS, tn), lambda l, j: (l, 0, j)),
        out_shape=jax.ShapeDtypeStruct((depth, MOD_ROWS, n), F32),
        compiler_params=_params(("arbitrary", "arbitrary")),
        name="ada_mod",
    )(cond, ada_w, ada_b.reshape(depth, 1, n))


def _cast_plumbing(jobs, n_steps, step_of):
    in_specs, args, out_specs, out_shapes = [], [], [], []
    for arr, layer, rows, _ in jobs:
        r, c = arr.shape[-2:]
        nblk = r // rows
        assert n_steps >= nblk and r % rows == 0
        rep = n_steps // nblk

        def blk(i, j, rep=rep, nblk=nblk):
            return jnp.minimum(step_of(i, j) // rep, nblk - 1)

        in_specs.append(pl.BlockSpec((None, rows, c), lambda i, j, blk=blk, layer=layer: (layer, blk(i, j), 0)))
        args.append(arr)
        out_specs.append(pl.BlockSpec((rows, c), lambda i, j, blk=blk: (blk(i, j), 0)))
        out_shapes.append(jax.ShapeDtypeStruct((r, c), BF16))
    return in_specs, args, out_specs, out_shapes


def _rotate_store(acc, o_ref, rope, tabs):
    tn = acc.shape[1]
    if not rope:
        o_ref[...] = acc.astype(o_ref.dtype)
        return
    groups = tn // (2 * LANES)
    for hh in range(groups):
        cos_ref, sin_ref = tabs[hh * len(tabs) // groups]
        cos = cos_ref[...]
        sin = sin_ref[...]
        lo = slice(2 * hh * LANES, (2 * hh + 1) * LANES)
        hi = slice((2 * hh + 1) * LANES, (2 * hh + 2) * LANES)
        x1 = acc[:, lo]
        x2 = acc[:, hi]
        o_ref[:, lo] = (x1 * cos - x2 * sin).astype(o_ref.dtype)
        o_ref[:, hi] = (x2 * cos + x1 * sin).astype(o_ref.dtype)


def _cast_block(v, order):
    if order is not None:
        q = v.shape[1] // len(order)
        v = jnp.concatenate([v[:, b * q:(b + 1) * q] for b in order], axis=1)
    return v.astype(BF16)


def _proj_kernel(*refs, rope, n_tab, cast_orders, emit_w, ns, nm, mod_row0, tiles_per_mod):
    refs = list(refs)
    xs_ref = refs.pop(0) if ns else None
    x_ref, g_ref, shift_ref, scale_ref, w_ref = refs[:5]
    refs = refs[5:]
    tabs = [(refs[2 * k], refs[2 * k + 1]) for k in range(n_tab)]
    refs = refs[2 * n_tab:]
    n_cast = len(cast_orders)
    cast_src, refs = refs[:n_cast], refs[n_cast:]
    o_ref = refs.pop(0)
    wout_ref = refs.pop(0) if emit_w else None
    cast_dst, a_refs = refs[:n_cast], refs[n_cast:]
    i = pl.program_id(0)
    j = pl.program_id(1)

    def prologue(x, tile):
        b = mod_row0 + tile // tiles_per_mod
        return _norm_mod(x, g_ref[...], shift_ref[b], scale_ref[b]).astype(BF16)

    def body(a_cur, a_nxt):
        w = w_ref[...]
        if emit_w:
            w = w.astype(BF16)
            wout_ref[...] = w
        acc = jnp.dot(a_cur[...], w, preferred_element_type=F32)
        _rotate_store(acc, o_ref, rope, tabs)
        for src, dst, order in zip(cast_src, cast_dst, cast_orders):
            dst[...] = _cast_block(src[...], order)
        if a_nxt is not None:
            @pl.when(j < ns)
            def _():
                rs = xs_ref.shape[0]
                a_nxt[pl.ds(pl.multiple_of(j * rs, rs), rs), :] = prologue(
                    xs_ref[...], jnp.minimum(i + 1, nm - 1))

    if ns == 0:
        @pl.when(j == 0)
        def _():
            a_refs[0][...] = prologue(x_ref[...], i)

        body(a_refs[0], None)
        return

    @pl.when((i == 0) & (j == 0))
    def _():
        a_refs[0][...] = prologue(x_ref[...], 0)

    @pl.when(i % 2 == 0)
    def _():
        body(a_refs[0], a_refs[1])

    @pl.when(i % 2 == 1)
    def _():
        body(a_refs[1], a_refs[0])


def _proj_call(x2d, g, mod4, layer, w, *, tm, tn, rows_per_mod, mod_row0, n_off=0, n_cols=None,
               rope=None, tables=(), ns=0, emit_w=False, casts=(), name="proj"):
    t, d = x2d.shape
    n_cols = w.shape[-1] if n_cols is None else n_cols
    nt = n_cols // tn
    nm = t // tm
    assert t % tm == 0 and n_cols % tn == 0 and rows_per_mod % tm == 0
    assert not emit_w or nm == 1
    assert ns == 0 or (tm % ns == 0 and ns <= nt and nm > 1)

    def mod_spec(k):
        return pl.BlockSpec((None, MOD_ROWS, 1, d), lambda i, j: (layer, 0, 0, k))

    in_specs, args = [], []
    if ns:
        in_specs.append(pl.BlockSpec(
            (tm // ns, d), lambda i, j: (jnp.minimum(i + 1, nm - 1) * ns + jnp.minimum(j, ns - 1), 0)))
        in_specs.append(pl.BlockSpec((tm, d), lambda i, j: (0, 0), pipeline_mode=pl.Buffered(1)))
        args += [x2d, x2d]
    else:
        in_specs.append(pl.BlockSpec((tm, d), lambda i, j: (i, 0)))
        args.append(x2d)
    in_specs += [pl.BlockSpec((1, d), lambda i, j: (0, 0)), mod_spec(0), mod_spec(1)]
    args += [g.reshape(1, d), mod4, mod4]
    if emit_w:
        in_specs.append(pl.BlockSpec((None, d, tn), lambda i, j: (0, 0, n_off + j)))
    else:
        in_specs.append(pl.BlockSpec((d, tn), lambda i, j: (0, n_off + j)))
    args.append(w)
    for cos, sin, tab_fn in tables:
        in_specs += [pl.BlockSpec((tm, LANES), lambda i, j, tab_fn=tab_fn: (tab_fn(i, j), 0))] * 2
        args += [cos, sin]
    c_in, c_args, c_out, c_shapes = _cast_plumbing(casts, nm * nt, lambda i, j: i * nt + j)
    out_specs = [pl.BlockSpec((tm, tn), lambda i, j: (i, j))]
    out_shapes = [jax.ShapeDtypeStruct((t, n_cols), BF16)]
    if emit_w:
        out_specs.append(pl.BlockSpec((d, tn), lambda i, j: (0, j)))
        out_shapes.append(jax.ShapeDtypeStruct((d, n_cols), BF16))
    return pl.pallas_call(
        functools.partial(_proj_kernel, rope=rope, n_tab=len(tables),
                          cast_orders=tuple(job[3] for job in casts), emit_w=emit_w, ns=ns, nm=nm,
                          mod_row0=mod_row0, tiles_per_mod=rows_per_mod // tm),
        grid=(nm, nt),
        in_specs=in_specs + c_in,
        out_specs=out_specs + c_out,
        out_shape=out_shapes + c_shapes,
        scratch_shapes=[pltpu.VMEM((tm, d), BF16)] * (2 if ns else 1),
        compiler_params=_params(("arbitrary", "arbitrary")),
        name=name,
    )(*args, *c_args)


def _out_kernel(y_ref, w_ref, x_ref, gate_ref, o_ref):
    acc = jnp.dot(y_ref[...], w_ref[...], preferred_element_type=F32)
    o_ref[...] = x_ref[...] + gate_ref[...] * acc


def _out_call(y, w, x2d, mod4, layer, *, tm, tn, rows_per_mod, mod_row0, name="out_proj"):
    t, k = y.shape
    d = w.shape[1]
    tiles_per_mod = rows_per_mod // tm
    gate_blk0 = 2 * d // tn
    return pl.pallas_call(
        _out_kernel,
        grid=(t // tm, d // tn),
        in_specs=[
            pl.BlockSpec((tm, k), lambda i, j: (i, 0)),
            pl.BlockSpec((k, tn), lambda i, j: (0, j)),
            pl.BlockSpec((tm, tn), lambda i, j: (i, j)),
            pl.BlockSpec((None, None, 1, tn),
                         lambda i, j: (layer, mod_row0 + i // tiles_per_mod, 0, gate_blk0 + j)),
        ],
        out_specs=pl.BlockSpec((tm, tn), lambda i, j: (i, j)),
        out_shape=jax.ShapeDtypeStruct((t, d), F32),
        compiler_params=_params(("parallel", "arbitrary")),
        name=name,
    )(y, w, x2d, mod4)


def _mlp_kernel(*refs, nf, final, n_cast, nm, mod_row0, tiles_per_mod):
    refs = list(refs)
    xs_ref, x_hbm, g_ref, shift_ref, scale_ref, gate_ref, w1_ref, w2_ref = refs[:8]
    refs = refs[8:]
    fg_ref = refs.pop(0) if final else None
    cast_src, refs = refs[:n_cast], refs[n_cast:]
    o_ref = refs.pop(0)
    cast_dst, refs = refs[:n_cast], refs[n_cast:]
    a_even, a_odd, x_buf, x_sem = refs
    i = pl.program_id(0)
    f = pl.program_id(1)
    tm = x_buf.shape[0]
    rs = xs_ref.shape[0]

    def x_copy(tile):
        return pltpu.make_async_copy(x_hbm.at[pl.ds(pl.multiple_of(tile * tm, tm), tm), :], x_buf, x_sem)

    def mod_row(tile):
        return mod_row0 + tile // tiles_per_mod

    def prologue(x, tile):
        b = mod_row(tile)
        return _norm_mod(x, g_ref[...], shift_ref[b], scale_ref[b]).astype(BF16)

    @pl.when((i == 0) & (f == 0))
    def _():
        first = x_copy(0)
        first.start()
        first.wait()
        a_even[...] = prologue(x_buf[...], 0)

    @pl.when(f == nf - 1 - X_LEAD)
    def _():
        x_copy(i).start()

    def body(a_cur, a_nxt):
        @pl.when(f == 0)
        def _():
            o_ref[...] = jnp.zeros_like(o_ref)

        h = jnp.dot(a_cur[...], w1_ref[...], preferred_element_type=F32)
        h = jnp.square(jnp.maximum(h, 0.0)).astype(BF16)
        o_ref[...] += jnp.dot(h, w2_ref[...], preferred_element_type=F32)
        for src, dst in zip(cast_src, cast_dst):
            dst[...] = src[...].astype(BF16)
        a_nxt[pl.ds(pl.multiple_of(f * rs, rs), rs), :] = prologue(xs_ref[...], jnp.minimum(i + 1, nm - 1))

        @pl.when(f == nf - 1)
        def _():
            x_copy(i).wait()
            r = x_buf[...] + gate_ref[mod_row(i)] * o_ref[...]
            if final:
                ms = jnp.mean(r * r, axis=-1, keepdims=True)
                r = r * lax.rsqrt(ms + NORM_EPS) * fg_ref[...]
            o_ref[...] = r

    @pl.when(i % 2 == 0)
    def _():
        body(a_even, a_odd)

    @pl.when(i % 2 == 1)
    def _():
        body(a_odd, a_even)


def _mlp_call(x2d, g, mod4, layer, w1, w2, *, tm, tf, rows_per_mod, mod_row0, final_g=None, casts=(),
              name="mlp"):
    t, d = x2d.shape
    ff = w1.shape[1]
    nf = ff // tf
    nm = t // tm
    assert t % tm == 0 and ff % tf == 0 and rows_per_mod % tm == 0
    assert tm % nf == 0 and (tm // nf) % 16 == 0 and nf > X_LEAD

    def mod_spec(k):
        return pl.BlockSpec((None, MOD_ROWS, 1, d), lambda i, f: (layer, 0, 0, k))

    in_specs = [
        pl.BlockSpec((tm // nf, d), lambda i, f: (jnp.minimum(i + 1, nm - 1) * nf + f, 0)),
        pl.BlockSpec(memory_space=pl.ANY),
        pl.BlockSpec((1, d), lambda i, f: (0, 0)),
        mod_spec(3), mod_spec(4), mod_spec(5),
        pl.BlockSpec((d, tf), lambda i, f: (0, f)),
        pl.BlockSpec((tf, d), lambda i, f: (f, 0)),
    ]
    args = [x2d, x2d, g.reshape(1, d), mod4, mod4, mod4, w1, w2]
    if final_g is not None:
        in_specs.append(pl.BlockSpec((1, d), lambda i, f: (0, 0)))
        args.append(final_g.reshape(1, d))
    c_in, c_args, c_out, c_shapes = _cast_plumbing(casts, nm * nf, lambda i, f: i * nf + f)
    return pl.pallas_call(
        functools.partial(_mlp_kernel, nf=nf, final=final_g is not None, n_cast=len(casts), nm=nm,
                          mod_row0=mod_row0, tiles_per_mod=rows_per_mod // tm),
        grid=(nm, nf),
        in_specs=in_specs + c_in,
        out_specs=[pl.BlockSpec((tm, d), lambda i, f: (i, 0))] + c_out,
        out_shape=[jax.ShapeDtypeStruct((t, d), F32)] + c_shapes,
        scratch_shapes=[pltpu.VMEM((tm, d), BF16), pltpu.VMEM((tm, d), BF16), pltpu.VMEM((tm, d), F32),
                        pltpu.SemaphoreType.DMA(())],
        compiler_params=_params(("arbitrary", "arbitrary"), vmem=MLP_VMEM_LIMIT),
        name=name,
    )(*args, *c_args)


def _log_sigmoid(x):
    return jnp.minimum(x, 0.0) - jnp.log1p(jnp.exp(-jnp.abs(x)))


def _ret_kernel(*refs, seq, ch, unroll, has_init, emit_state):
    refs = list(refs)
    df_ref, db_ref, q_ref, k_ref, v_ref, g_ref = refs[:6]
    refs = refs[6:]
    s0f_ref, s0b_ref = (refs.pop(0), refs.pop(0)) if has_init else (None, None)
    y_ref = refs.pop(0)
    sff_ref, sfb_ref = (refs.pop(0), refs.pop(0)) if emit_state else (None, None)
    oc_ref, sf_ref, sb_ref = refs
    h = pl.program_id(1)
    n = seq // ch
    lgf = _log_sigmoid(jnp.full((ch, ch), df_ref[h], F32))
    lgb = _log_sigmoid(jnp.full((ch, ch), db_ref[h], F32))
    ii = lax.broadcasted_iota(jnp.int32, (ch, ch), 0)
    jj = lax.broadcasted_iota(jnp.int32, (ch, ch), 1)
    diff = (ii - jj).astype(F32)
    decay = (jnp.where(diff >= 0, jnp.exp(lgf * jnp.maximum(diff, 0.0)), 0.0)
             + jnp.where(diff <= 0, jnp.exp(lgb * jnp.maximum(-diff, 0.0)), 0.0))
    col = lax.broadcasted_iota(jnp.int32, (ch, 1), 0).astype(F32)
    lgf_c = _log_sigmoid(jnp.full((ch, 1), df_ref[h], F32))
    lgb_c = _log_sigmoid(jnp.full((ch, 1), db_ref[h], F32))
    dq_f = jnp.exp(lgf_c * (col + 1.0))
    dk_f = jnp.exp(lgf_c * (ch - 1.0 - col))
    dc_f = jnp.exp(_log_sigmoid(jnp.full((1, 1), df_ref[h], F32)) * float(ch))
    dq_b = jnp.exp(lgb_c * (ch - col))
    dk_b = jnp.exp(lgb_c * col)
    dc_b = jnp.exp(_log_sigmoid(jnp.full((1, 1), db_ref[h], F32)) * float(ch))
    tdims = (((0,), (0,)), ((), ()))
    ndims = (((1,), (1,)), ((), ()))

    if has_init:
        sf_ref[...] = s0f_ref[...]
        sb_ref[...] = s0b_ref[...]
    else:
        sf_ref[...] = jnp.zeros_like(sf_ref)
        sb_ref[...] = jnp.zeros_like(sb_ref)

    def rows(c):
        return pl.ds(pl.multiple_of(c * ch, ch), ch)

    def cross(c, s_ref, dq, dk, dc, first):
        r = rows(c)
        q = q_ref[r, :]
        k = k_ref[r, :]
        v = v_ref[r, :]
        s = s_ref[...]
        term = jnp.dot(q, s.astype(BF16), preferred_element_type=F32) * dq
        if first:
            oc_ref[r, :] = term
        else:
            oc_ref[r, :] += term
        kd = (k.astype(F32) * dk).astype(BF16)
        s_ref[...] = s * dc + lax.dot_general(kd, v, tdims, preferred_element_type=F32)

    def sweep(first):
        def step(t, carry):
            cross(t, sf_ref, dq_f, dk_f, dc_f, first)
            cross(n - 1 - t, sb_ref, dq_b, dk_b, dc_b, first)
            return carry
        return step

    half = n // 2
    if n % 2:
        oc_ref[pl.ds(half * ch, ch), :] = jnp.zeros((ch, oc_ref.shape[1]), F32)
    if half:
        lax.fori_loop(0, half, sweep(True), 0, unroll=min(unroll, half))
    lax.fori_loop(half, n, sweep(False), 0, unroll=min(unroll, n - half))
    if emit_state:
        sff_ref[...] = sf_ref[...]
        sfb_ref[...] = sb_ref[...]

    def finish(c, carry):
        r = rows(c)
        q = q_ref[r, :]
        scores = lax.dot_general(q, k_ref[r, :], ndims, preferred_element_type=F32) * decay
        o = jnp.dot(scores.astype(BF16), v_ref[r, :], preferred_element_type=F32) + oc_ref[r, :]
        y = o * lax.rsqrt(jnp.mean(o * o, axis=-1, keepdims=True) + NORM_EPS)
        hg = 0.5 * g_ref[r, :].astype(F32)
        y_ref[r, :] = (hg * (1.0 + jnp.tanh(hg)) * y).astype(y_ref.dtype)
        return carry

    lax.fori_loop(0, n, finish, 0, unroll=min(unroll, n))


def _ret_call(qkvg, decay_f, decay_b, init=None, *, batch, seq, ch, dk, dv, emit_state, name="retention"):
    heads = RET_HEADS
    smem = pl.BlockSpec(memory_space=pltpu.SMEM)
    state = pl.BlockSpec((None, None, dk, dv), lambda b, h: (b, h, 0, 0))
    state_shape = jax.ShapeDtypeStruct((batch, heads, dk, dv), F32)
    n_init = 0 if init is None else 2
    n_fin = 2 if emit_state else 0
    return pl.pallas_call(
        functools.partial(_ret_kernel, seq=seq, ch=ch, unroll=4, has_init=init is not None,
                          emit_state=emit_state),
        grid=(batch, heads),
        in_specs=[
            smem, smem,
            pl.BlockSpec((seq, dk), lambda b, h: (b, h)),
            pl.BlockSpec((seq, dk), lambda b, h: (b, heads + h)),
            pl.BlockSpec((seq, dv), lambda b, h: (b, heads + h)),
            pl.BlockSpec((seq, dv), lambda b, h: (b, 2 * heads + h)),
        ] + [state] * n_init,
        out_specs=[pl.BlockSpec((seq, dv), lambda b, h: (b, h))] + [state] * n_fin,
        out_shape=[jax.ShapeDtypeStruct((batch * seq, heads * dv), BF16)] + [state_shape] * n_fin,
        scratch_shapes=[
            pltpu.VMEM((seq, dv), F32),
            pltpu.VMEM((dk, dv), F32),
            pltpu.VMEM((dk, dv), F32),
        ],
        compiler_params=_params(("parallel", "arbitrary")),
        name=name,
    )(decay_f, decay_b, qkvg, qkvg, qkvg, qkvg, *(init or ()))


def _attn_kernel(sink_ref, q_ref, kp_ref, kc_ref, kn_ref, vp_ref, vc_ref, vn_ref, kx_ref, vx_ref,
                 o_ref, *, nq, tq, group):
    n = pl.program_id(1)
    dh = ATT_HEAD_DIM
    n_ctx = kx_ref.shape[0]
    win = tq + 2 * WINDOW
    rows = group * tq
    nk = win + n_ctx
    scale = dh ** -0.5
    assert tq == WINDOW
    rq = lax.broadcasted_iota(jnp.int32, (rows, WINDOW), 0) % tq
    ck = lax.broadcasted_iota(jnp.int32, (rows, WINDOW), 1)
    ok_prev = (ck >= rq) & (n > 0)
    ok_next = (ck <= rq) & (n < nq - 1)
    rcol = lax.broadcasted_iota(jnp.int32, (rows, 1), 0) // tq
    ndims = (((1,), (1,)), ((), ()))
    ones = jnp.ones((nk, dh), BF16)

    lane = lax.broadcasted_iota(jnp.int32, (tq, 2 * dh), 1)
    for kh in range(ATT_KV_HEADS):
        pair, e = divmod(kh, 2)
        mine = (lane // (dh // 2)) % 2 == e
        hs = slice(kh * dh, (kh + 1) * dh)
        ps = slice(pair * 2 * dh, (pair + 1) * 2 * dh)
        qs = jnp.concatenate(
            [jnp.where(mine, q_ref[:, (pair * group + g) * 2 * dh:(pair * group + g + 1) * 2 * dh], 0)
             for g in range(group)], axis=0)
        k_all = jnp.concatenate([kp_ref[:, ps], kc_ref[:, ps], kn_ref[:, ps], kx_ref[:, ps]], axis=0)
        v_all = jnp.concatenate([vp_ref[:, hs], vc_ref[:, hs], vn_ref[:, hs], vx_ref[:, hs]], axis=0)
        s = lax.dot_general(qs, k_all, ndims, preferred_element_type=F32) * (scale * LOG2E)
        s = jnp.concatenate([jnp.where(ok_prev, s[:, :WINDOW], NEG_INF), s[:, WINDOW:WINDOW + tq],
                             jnp.where(ok_next, s[:, WINDOW + tq:win], NEG_INF), s[:, win:]], axis=1)
        sink = jnp.zeros((rows, 1), F32)
        for g in range(group):
            sink = jnp.where(rcol == g, sink_ref[kh * group + g] * LOG2E, sink)
        m = jnp.maximum(jnp.max(s, axis=-1, keepdims=True), sink)
        p = jnp.exp2(s - m).astype(BF16)
        pv = jnp.dot(p, jnp.concatenate([v_all, ones], axis=1), preferred_element_type=F32)
        o = pv[:, :dh] / (pv[:, dh:dh + 1] + jnp.exp2(sink - m))
        for g in range(group):
            o_ref[:, (kh * group + g) * dh:(kh * group + g + 1) * dh] = (
                o[g * tq:(g + 1) * tq, :].astype(o_ref.dtype))


def _attn_call(qkv, kv_ctx, sink, *, batch, seq, n_ctx, tq, name="window_attn"):
    dh = ATT_HEAD_DIM
    kvw = ATT_KV_HEADS * dh
    qw = qkv.shape[1] - 2 * kvw
    group = qw // kvw
    nq = seq // tq
    wpb = seq // WINDOW
    step = tq // WINDOW
    k_blk = qw // kvw
    v_blk = k_blk + 1

    def prev_map(col):
        return lambda b, n: (b * wpb + jnp.maximum(n * step - 1, 0), col)

    def cur_map(col):
        return lambda b, n: (b * nq + n, col)

    def next_map(col):
        return lambda b, n: (b * wpb + jnp.minimum((n + 1) * step, wpb - 1), col)

    return pl.pallas_call(
        functools.partial(_attn_kernel, nq=nq, tq=tq, group=group),
        grid=(batch, nq),
        in_specs=[
            pl.BlockSpec(memory_space=pltpu.SMEM),
            pl.BlockSpec((tq, qw), lambda b, n: (b * nq + n, 0)),
            pl.BlockSpec((WINDOW, kvw), prev_map(k_blk)),
            pl.BlockSpec((tq, kvw), cur_map(k_blk)),
            pl.BlockSpec((WINDOW, kvw), next_map(k_blk)),
            pl.BlockSpec((WINDOW, kvw), prev_map(v_blk)),
            pl.BlockSpec((tq, kvw), cur_map(v_blk)),
            pl.BlockSpec((WINDOW, kvw), next_map(v_blk)),
            pl.BlockSpec((n_ctx, kvw), lambda b, n: (b, 0)),
            pl.BlockSpec((n_ctx, kvw), lambda b, n: (b, 1)),
        ],
        out_specs=pl.BlockSpec((tq, qw), lambda b, n: (b * nq + n, 0)),
        out_shape=jax.ShapeDtypeStruct((batch * seq, qw), BF16),
        compiler_params=_params(("parallel", "arbitrary")),
        name=name,
    )(sink, qkv, qkv, qkv, qkv, qkv, qkv, qkv, kv_ctx, kv_ctx)


def _rope_angles(pos, dim, base):
    inv_freq = base ** (-np.arange(0, dim, 2, dtype=np.float64) / dim)
    return pos.astype(np.float64)[:, None] * inv_freq[None, :]


def _table(cos_parts, sin_parts, tab_fn):
    return (jnp.asarray(np.concatenate(cos_parts), F32), jnp.asarray(np.concatenate(sin_parts), F32), tab_fn)


def _axial_tables(seq):
    rows = np.arange(seq) // GRID_W
    cols = np.arange(seq) % GRID_W
    half = ATT_HEAD_DIM // 2
    ar = _rope_angles(rows, half, ROPE_BASE)
    ac = _rope_angles(cols, half, ROPE_BASE)
    cos = np.concatenate([np.cos(ar), np.cos(ac), np.cos(ar), np.cos(ac)], axis=-1)
    sin = np.concatenate([np.sin(ar), np.sin(ac), np.sin(ar), np.sin(ac)], axis=-1)
    return cos, sin


def _pair_layout_blocks(n_q_heads, n_kv_heads):
    group = n_q_heads // n_kv_heads

    def pair(a, b):
        return [4 * a, 4 * a + 2, 4 * b, 4 * b + 2, 4 * a + 1, 4 * a + 3, 4 * b + 1, 4 * b + 3]

    blocks = []
    for p in range(n_kv_heads // 2):
        for g in range(group):
            blocks += pair(2 * p * group + g, (2 * p + 1) * group + g)
    for p in range(n_kv_heads // 2):
        blocks += pair(n_q_heads + 2 * p, n_q_heads + 2 * p + 1)
    blocks += range(4 * (n_q_heads + n_kv_heads), 4 * (n_q_heads + 2 * n_kv_heads))
    return blocks


def kernel(x, c, ctx, c_ctx, ada_w, ada_b, norm_mix_g, norm_mlp_g, mlp_w1, mlp_w2, ret_w_in, ret_w_out,
           ret_decay_fwd, ret_decay_bwd, attn_w_in, attn_w_out, attn_sink, final_norm_g):
    batch, seq, d = x.shape
    n_ctx = ctx.shape[1]
    assert ada_w.shape[0] == 2 and ret_w_in.shape[0] == 1 and attn_w_in.shape[0] == 1
    assert batch + 1 <= MOD_ROWS
    ret_dk = d // RET_HEADS
    ret_dv = 2 * ret_dk
    qk_w = RET_HEADS * ret_dk
    att_qw = attn_w_out.shape[1]
    att_kvw = ATT_KV_HEADS * ATT_HEAD_DIM
    ctx_row = batch

    cond = jnp.concatenate([c, c_ctx[None, :], jnp.zeros((MOD_ROWS - batch - 1, d), F32)], axis=0)
    mod = _ada_call(cond, ada_w, ada_b)
    mod4 = mod.reshape(mod.shape[0], MOD_ROWS, 1, mod.shape[2])

    xl = x.reshape(batch * seq, d)
    xc = ctx.reshape(batch * n_ctx, d)
    t_ctx = batch * n_ctx
    lat = dict(rows_per_mod=seq, mod_row0=0)
    cx = dict(rows_per_mod=t_ctx, mod_row0=ctx_row)

    ret_ang = _rope_angles(np.arange(seq), ret_dk, ROPE_BASE)
    ret_cos, ret_sin = np.cos(ret_ang), np.sin(ret_ang)
    att_cos, att_sin = _axial_tables(seq)

    tm, tn, tn_ctx = 1024, 1024, 512
    k_scale = ret_dk ** -0.5
    tps = seq // tm
    one = np.ones((tm, LANES))
    zero = np.zeros((tm, LANES))

    def ret_tab_lat(i, j):
        q_tiles = qk_w // tn
        return jnp.where(j < q_tiles, i % tps, jnp.where(j < 2 * q_tiles, tps + i % tps, 2 * tps))

    def ret_tab_ctx(i, j):
        q_tiles = qk_w // tn_ctx
        return jnp.where((j >= q_tiles) & (j < 2 * q_tiles), 1, 0)

    assert t_ctx == tm
    ctx_tab = _table([one, one * k_scale], [zero, zero], ret_tab_ctx)
    p_ctx, w_in = _proj_call(xc, norm_mix_g[0], mod4, 0, ret_w_in, tm=tm, tn=tn_ctx, rope="half",
                             tables=[ctx_tab], emit_w=True, name="ret_proj_ctx", **cx)
    assert ATT_KV_HEADS % 2 == 0
    pair_order = tuple(_pair_layout_blocks(att_qw // ATT_HEAD_DIM, ATT_KV_HEADS))
    casts = [(ret_w_out, 0, 32, None), (mlp_w1, 0, 16, None), (mlp_w2, 0, 64, None),
             (attn_w_in, 0, 16, pair_order), (attn_w_out, 0, 16, None)]
    lat_tab = _table([ret_cos, ret_cos * k_scale, one], [ret_sin, ret_sin * k_scale, zero], ret_tab_lat)
    p_lat, w_out, w1, w2, att_w_in, att_w_out = _proj_call(
        xl, norm_mix_g[0], mod4, 0, w_in, tm=tm, tn=tn, rope="half", tables=[lat_tab],
        ns=8, casts=casts, name="ret_proj_lat", **lat)
    ret = dict(batch=batch, dk=ret_dk, dv=ret_dv)
    y_ctx, sc_f, sc_b = _ret_call(p_ctx, ret_decay_fwd[0], ret_decay_bwd[0], seq=n_ctx, ch=min(256, n_ctx),
                                  emit_state=True, name="retention_ctx", **ret)
    y_lat, = _ret_call(p_lat, ret_decay_fwd[0], ret_decay_bwd[0], (sc_f, sc_b), seq=seq, ch=256,
                       emit_state=False, name="retention_lat", **ret)
    xl = _out_call(y_lat, w_out, xl, mod4, 0, tm=1024, tn=1024, name="ret_out_lat", **lat)
    xc = _out_call(y_ctx, w_out, xc, mod4, 0, tm=t_ctx, tn=512, name="ret_out_ctx", **cx)
    xl, w1_next, w2_next = _mlp_call(xl, norm_mlp_g[0], mod4, 0, w1, w2, tm=1024, tf=1024,
                                     casts=[(mlp_w1, 1, 16, None), (mlp_w2, 1, 64, None)], name="mlp0_lat",
                                     **lat)
    xc, = _mlp_call(xc, norm_mlp_g[0], mod4, 0, w1, w2, tm=512, tf=1024, name="mlp0_ctx", **cx)

    kv_ctx, = _proj_call(xc, norm_mix_g[1], mod4, 1, att_w_in, tm=t_ctx, tn=512, n_off=att_qw // 512,
                         n_cols=2 * att_kvw, name="attn_proj_ctx", **cx)
    att_tn = 2 * att_kvw
    assert att_qw % att_tn == 0
    q_tiles = att_qw // att_tn
    qkv, = _proj_call(xl, norm_mix_g[1], mod4, 1, att_w_in, tm=tm, tn=att_tn, rope="half",
                      tables=[_table([att_cos, one], [att_sin, zero], lambda i, j: i % tps),
                              _table([att_cos, one], [att_sin, zero],
                                     lambda i, j: jnp.where(j < q_tiles, i % tps, tps))],
                      ns=2, name="attn_proj_lat", **lat)
    o = _attn_call(qkv, kv_ctx, attn_sink[0], batch=batch, seq=seq, n_ctx=n_ctx, tq=WINDOW)
    xl = _out_call(o, att_w_out, xl, mod4, 1, tm=1024, tn=1024, name="attn_out_lat", **lat)
    xl, = _mlp_call(xl, norm_mlp_g[1], mod4, 1, w1_next, w2_next, tm=1024, tf=1024, final_g=final_norm_g,
                    name="mlp1_lat", **lat)
    return xl.reshape(batch, seq, d)
```

```python
import functools

import jax
import numpy as np
import jax.numpy as jnp
from jax import lax
from jax.experimental import pallas as pl
from jax.experimental.pallas import tpu as pltpu

F32 = jnp.float32
BF16 = jnp.bfloat16

NORM_EPS = 1e-6
NEG_INF = -1e30
LOG2E = 1.4426950408889634
GRID_W = 64
ROPE_BASE = 10000.0

RET_HEADS = 8
ATT_HEAD_DIM = 128
ATT_KV_HEADS = 4
WINDOW = 128
LANES = 128

MOD_ROWS = 8
VMEM_LIMIT = 56 * 1024 * 1024
MLP_VMEM_LIMIT = 64 * 1024 * 1024 - 256 * 1024
X_LEAD = 2


def _params(semantics, vmem=VMEM_LIMIT):
    return pltpu.CompilerParams(dimension_semantics=semantics, vmem_limit_bytes=vmem)


def _norm_mod(x, g, shift, scale):
    ms = jnp.mean(x * x, axis=-1, keepdims=True)
    return x * lax.rsqrt(ms + NORM_EPS) * (g * (1.0 + scale)) + shift


def _ada_kernel(cond_ref, w_ref, b_ref, o_ref):
    c = cond_ref[...]
    s = c * jax.nn.sigmoid(c)
    o_ref[...] = jnp.dot(s.astype(BF16), w_ref[...].astype(BF16),
                         preferred_element_type=F32) + b_ref[...]


def _ada_call(cond, ada_w, ada_b, tn=1024):
    depth, d, n = ada_w.shape
    return pl.pallas_call(
        _ada_kernel,
        grid=(depth, n // tn),
        in_specs=[
            pl.BlockSpec((MOD_ROWS, d), lambda l, j: (0, 0)),
            pl.BlockSpec((None, d, tn), lambda l, j: (l, 0, j)),
            pl.BlockSpec((None, 1, tn), lambda l, j: (l, 0, j)),
        ],
        out_specs=pl.BlockSpec((None, MOD_ROWS, tn), lambda l, j: (l, 0, j)),
        out_shape=jax.ShapeDtypeStruct((depth, MOD_ROWS, n), F32),
        compiler_params=_params(("arbitrary", "arbitrary")),
        name="ada_mod",
    )(cond, ada_w, ada_b.reshape(depth, 1, n))


def _cast_plumbing(jobs, n_steps, step_of):
    in_specs, args, out_specs, out_shapes = [], [], [], []
    for arr, layer, rows, _ in jobs:
        r, c = arr.shape[-2:]
        nblk = r // rows
        assert n_steps >= nblk and r % rows == 0
        rep = n_steps // nblk

        def blk(i, j, rep=rep, nblk=nblk):
            return jnp.minimum(step_of(i, j) // rep, nblk - 1)

        in_specs.append(pl.BlockSpec((None, rows, c), lambda i, j, blk=blk, layer=layer: (layer, blk(i, j), 0)))
        args.append(arr)
        out_specs.append(pl.BlockSpec((rows, c), lambda i, j, blk=blk: (blk(i, j), 0)))
        out_shapes.append(jax.ShapeDtypeStruct((r, c), BF16))
    return in_specs, args, out_specs, out_shapes


def _rotate_store(acc, o_ref, rope, tabs):
    tn = acc.shape[1]
    if not rope:
        o_ref[...] = acc.astype(o_ref.dtype)
        return
    groups = tn // (2 * LANES)
    for hh in range(groups):
        cos_ref, sin_ref = tabs[hh * len(tabs) // groups]
        cos = cos_ref[...]
        sin = sin_ref[...]
        lo = slice(2 * hh * LANES, (2 * hh + 1) * LANES)
        hi = slice((2 * hh + 1) * LANES, (2 * hh + 2) * LANES)
        x1 = acc[:, lo]
        x2 = acc[:, hi]
        o_ref[:, lo] = (x1 * cos - x2 * sin).astype(o_ref.dtype)
        o_ref[:, hi] = (x2 * cos + x1 * sin).astype(o_ref.dtype)


def _cast_block(v, order):
    if order is not None:
        q = v.shape[1] // len(order)
        v = jnp.concatenate([v[:, b * q:(b + 1) * q] for b in order], axis=1)
    return v.astype(BF16)


def _proj_kernel(*refs, rope, n_tab, cast_orders, emit_w, ns, nm, mod_row0, tiles_per_mod):
    refs = list(refs)
    xs_ref = refs.pop(0) if ns else None
    x_ref, g_ref, shift_ref, scale_ref, w_ref = refs[:5]
    refs = refs[5:]
    tabs = [(refs[2 * k], refs[2 * k + 1]) for k in range(n_tab)]
    refs = refs[2 * n_tab:]
    n_cast = len(cast_orders)
    cast_src, refs = refs[:n_cast], refs[n_cast:]
    o_ref = refs.pop(0)
    wout_ref = refs.pop(0) if emit_w else None
    cast_dst, a_refs = refs[:n_cast], refs[n_cast:]
    i = pl.program_id(0)
    j = pl.program_id(1)

    def prologue(x, tile):
        b = mod_row0 + tile // tiles_per_mod
        return _norm_mod(x, g_ref[...], shift_ref[b], scale_ref[b]).astype(BF16)

    def body(a_cur, a_nxt):
        w = w_ref[...]
        if emit_w:
            w = w.astype(BF16)
            wout_ref[...] = w
        acc = jnp.dot(a_cur[...], w, preferred_element_type=F32)
        _rotate_store(acc, o_ref, rope, tabs)
        for src, dst, order in zip(cast_src, cast_dst, cast_orders):
            dst[...] = _cast_block(src[...], order)
        if a_nxt is not None:
            @pl.when(j < ns)
            def _():
                rs = xs_ref.shape[0]
                a_nxt[pl.ds(pl.multiple_of(j * rs, rs), rs), :] = prologue(
                    xs_ref[...], jnp.minimum(i + 1, nm - 1))

    if ns == 0:
        @pl.when(j == 0)
        def _():
            a_refs[0][...] = prologue(x_ref[...], i)

        body(a_refs[0], None)
        return

    @pl.when((i == 0) & (j == 0))
    def _():
        a_refs[0][...] = prologue(x_ref[...], 0)

    @pl.when(i % 2 == 0)
    def _():
        body(a_refs[0], a_refs[1])

    @pl.when(i % 2 == 1)
    def _():
        body(a_refs[1], a_refs[0])


def _proj_call(x2d, g, mod4, layer, w, *, tm, tn, rows_per_mod, mod_row0, n_off=0, n_cols=None,
               rope=None, tables=(), ns=0, emit_w=False, casts=(), name="proj"):
    t, d = x2d.shape
    n_cols = w.shape[-1] if n_cols is None else n_cols
    nt = n_cols // tn
    nm = t // tm
    assert t % tm == 0 and n_cols % tn == 0 and rows_per_mod % tm == 0
    assert not emit_w or nm == 1
    assert ns == 0 or (tm % ns == 0 and ns <= nt and nm > 1)

    def mod_spec(k):
        return pl.BlockSpec((None, MOD_ROWS, 1, d), lambda i, j: (layer, 0, 0, k))

    in_specs, args = [], []
    if ns:
        in_specs.append(pl.BlockSpec(
            (tm // ns, d), lambda i, j: (jnp.minimum(i + 1, nm - 1) * ns + jnp.minimum(j, ns - 1), 0)))
        in_specs.append(pl.BlockSpec((tm, d), lambda i, j: (0, 0), pipeline_mode=pl.Buffered(1)))
        args += [x2d, x2d]
    else:
        in_specs.append(pl.BlockSpec((tm, d), lambda i, j: (i, 0)))
        args.append(x2d)
    in_specs += [pl.BlockSpec((1, d), lambda i, j: (0, 0)), mod_spec(0), mod_spec(1)]
    args += [g.reshape(1, d), mod4, mod4]
    if emit_w:
        in_specs.append(pl.BlockSpec((None, d, tn), lambda i, j: (0, 0, n_off + j)))
    else:
        in_specs.append(pl.BlockSpec((d, tn), lambda i, j: (0, n_off + j)))
    args.append(w)
    for cos, sin, tab_fn in tables:
        in_specs += [pl.BlockSpec((tm, LANES), lambda i, j, tab_fn=tab_fn: (tab_fn(i, j), 0))] * 2
        args += [cos, sin]
    c_in, c_args, c_out, c_shapes = _cast_plumbing(casts, nm * nt, lambda i, j: i * nt + j)
    out_specs = [pl.BlockSpec((tm, tn), lambda i, j: (i, j))]
    out_shapes = [jax.ShapeDtypeStruct((t, n_cols), BF16)]
    if emit_w:
        out_specs.append(pl.BlockSpec((d, tn), lambda i, j: (0, j)))
        out_shapes.append(jax.ShapeDtypeStruct((d, n_cols), BF16))
    return pl.pallas_call(
        functools.partial(_proj_kernel, rope=rope, n_tab=len(tables),
                          cast_orders=tuple(job[3] for job in casts), emit_w=emit_w, ns=ns, nm=nm,
                          mod_row0=mod_row0, tiles_per_mod=rows_per_mod // tm),
        grid=(nm, nt),
        in_specs=in_specs + c_in,
        out_specs=out_specs + c_out,
        out_shape=out_shapes + c_shapes,
        scratch_shapes=[pltpu.VMEM((tm, d), BF16)] * (2 if ns else 1),
        compiler_params=_params(("arbitrary", "arbitrary")),
        name=name,
    )(*args, *c_args)


def _out_kernel(y_ref, w_ref, x_ref, gate_ref, o_ref):
    acc = jnp.dot(y_ref[...], w_ref[...], preferred_element_type=F32)
    o_ref[...] = x_ref[...] + gate_ref[...] * acc


def _out_call(y, w, x2d, mod4, layer, *, tm, tn, rows_per_mod, mod_row0, name="out_proj"):
    t, k = y.shape
    d = w.shape[1]
    tiles_per_mod = rows_per_mod // tm
    gate_blk0 = 2 * d // tn
    return pl.pallas_call(
        _out_kernel,
        grid=(t // tm, d // tn),
        in_specs=[
            pl.BlockSpec((tm, k), lambda i, j: (i, 0)),
            pl.BlockSpec((k, tn), lambda i, j: (0, j)),
            pl.BlockSpec((tm, tn), lambda i, j: (i, j)),
            pl.BlockSpec((None, None, 1, tn),
                         lambda i, j: (layer, mod_row0 + i // tiles_per_mod, 0, gate_blk0 + j)),
        ],
        out_specs=pl.BlockSpec((tm, tn), lambda i, j: (i, j)),
        out_shape=jax.ShapeDtypeStruct((t, d), F32),
        compiler_params=_params(("parallel", "arbitrary")),
        name=name,
    )(y, w, x2d, mod4)


def _mlp_kernel(*refs, nf, final, n_cast, nm, mod_row0, tiles_per_mod):
    refs = list(refs)
    xs_ref, x_hbm, g_ref, shift_ref, scale_ref, gate_ref, w1_ref, w2_ref = refs[:8]
    refs = refs[8:]
    fg_ref = refs.pop(0) if final else None
    cast_src, refs = refs[:n_cast], refs[n_cast:]
    o_ref = refs.pop(0)
    cast_dst, refs = refs[:n_cast], refs[n_cast:]
    a_even, a_odd, x_buf, x_sem = refs
    i = pl.program_id(0)
    f = pl.program_id(1)
    tm = x_buf.shape[0]
    rs = xs_ref.shape[0]

    def x_copy(tile):
        return pltpu.make_async_copy(x_hbm.at[pl.ds(pl.multiple_of(tile * tm, tm), tm), :], x_buf, x_sem)

    def mod_row(tile):
        return mod_row0 + tile // tiles_per_mod

    def prologue(x, tile):
        b = mod_row(tile)
        return _norm_mod(x, g_ref[...], shift_ref[b], scale_ref[b]).astype(BF16)

    @pl.when((i == 0) & (f == 0))
    def _():
        first = x_copy(0)
        first.start()
        first.wait()
        a_even[...] = prologue(x_buf[...], 0)

    @pl.when(f == nf - 1 - X_LEAD)
    def _():
        x_copy(i).start()

    def body(a_cur, a_nxt):
        @pl.when(f == 0)
        def _():
            o_ref[...] = jnp.zeros_like(o_ref)

        h = jnp.dot(a_cur[...], w1_ref[...], preferred_element_type=F32)
        h = jnp.square(jnp.maximum(h, 0.0)).astype(BF16)
        o_ref[...] += jnp.dot(h, w2_ref[...], preferred_element_type=F32)
        for src, dst in zip(cast_src, cast_dst):
            dst[...] = src[...].astype(BF16)
        a_nxt[pl.ds(pl.multiple_of(f * rs, rs), rs), :] = prologue(xs_ref[...], jnp.minimum(i + 1, nm - 1))

        @pl.when(f == nf - 1)
        def _():
            x_copy(i).wait()
            r = x_buf[...] + gate_ref[mod_row(i)] * o_ref[...]
            if final:
                ms = jnp.mean(r * r, axis=-1, keepdims=True)
                r = r * lax.rsqrt(ms + NORM_EPS) * fg_ref[...]
            o_ref[...] = r

    @pl.when(i % 2 == 0)
    def _():
        body(a_even, a_odd)

    @pl.when(i % 2 == 1)
    def _():
        body(a_odd, a_even)


def _mlp_call(x2d, g, mod4, layer, w1, w2, *, tm, tf, rows_per_mod, mod_row0, final_g=None, casts=(),
              name="mlp"):
    t, d = x2d.shape
    ff = w1.shape[1]
    nf = ff // tf
    nm = t // tm
    assert t % tm == 0 and ff % tf == 0 and rows_per_mod % tm == 0
    assert tm % nf == 0 and (tm // nf) % 16 == 0 and nf > X_LEAD

    def mod_spec(k):
        return pl.BlockSpec((None, MOD_ROWS, 1, d), lambda i, f: (layer, 0, 0, k))

    in_specs = [
        pl.BlockSpec((tm // nf, d), lambda i, f: (jnp.minimum(i + 1, nm - 1) * nf + f, 0)),
        pl.BlockSpec(memory_space=pl.ANY),
        pl.BlockSpec((1, d), lambda i, f: (0, 0)),
        mod_spec(3), mod_spec(4), mod_spec(5),
        pl.BlockSpec((d, tf), lambda i, f: (0, f)),
        pl.BlockSpec((tf, d), lambda i, f: (f, 0)),
    ]
    args = [x2d, x2d, g.reshape(1, d), mod4, mod4, mod4, w1, w2]
    if final_g is not None:
        in_specs.append(pl.BlockSpec((1, d), lambda i, f: (0, 0)))
        args.append(final_g.reshape(1, d))
    c_in, c_args, c_out, c_shapes = _cast_plumbing(casts, nm * nf, lambda i, f: i * nf + f)
    return pl.pallas_call(
        functools.partial(_mlp_kernel, nf=nf, final=final_g is not None, n_cast=len(casts), nm=nm,
                          mod_row0=mod_row0, tiles_per_mod=rows_per_mod // tm),
        grid=(nm, nf),
        in_specs=in_specs + c_in,
        out_specs=[pl.BlockSpec((tm, d), lambda i, f: (i, 0))] + c_out,
        out_shape=[jax.ShapeDtypeStruct((t, d), F32)] + c_shapes,
        scratch_shapes=[pltpu.VMEM((tm, d), BF16), pltpu.VMEM((tm, d), BF16), pltpu.VMEM((tm, d), F32),
                        pltpu.SemaphoreType.DMA(())],
        compiler_params=_params(("arbitrary", "arbitrary"), vmem=MLP_VMEM_LIMIT),
        name=name,
    )(*args, *c_args)


def _log_sigmoid(x):
    return jnp.minimum(x, 0.0) - jnp.log1p(jnp.exp(-jnp.abs(x)))


def _ret_kernel(*refs, seq, ch, unroll, has_init, emit_state):
    refs = list(refs)
    df_ref, db_ref, q_ref, k_ref, v_ref, g_ref = refs[:6]
    refs = refs[6:]
    s0f_ref, s0b_ref = (refs.pop(0), refs.pop(0)) if has_init else (None, None)
    y_ref = refs.pop(0)
    sff_ref, sfb_ref = (refs.pop(0), refs.pop(0)) if emit_state else (None, None)
    oc_ref, sf_ref, sb_ref = refs
    h = pl.program_id(1)
    n = seq // ch
    lgf = _log_sigmoid(jnp.full((ch, ch), df_ref[h], F32))
    lgb = _log_sigmoid(jnp.full((ch, ch), db_ref[h], F32))
    ii = lax.broadcasted_iota(jnp.int32, (ch, ch), 0)
    jj = lax.broadcasted_iota(jnp.int32, (ch, ch), 1)
    diff = (ii - jj).astype(F32)
    decay = (jnp.where(diff >= 0, jnp.exp(lgf * jnp.maximum(diff, 0.0)), 0.0)
             + jnp.where(diff <= 0, jnp.exp(lgb * jnp.maximum(-diff, 0.0)), 0.0))
    col = lax.broadcasted_iota(jnp.int32, (ch, 1), 0).astype(F32)
    lgf_c = _log_sigmoid(jnp.full((ch, 1), df_ref[h], F32))
    lgb_c = _log_sigmoid(jnp.full((ch, 1), db_ref[h], F32))
    dq_f = jnp.exp(lgf_c * (col + 1.0))
    dk_f = jnp.exp(lgf_c * (ch - 1.0 - col))
    dc_f = jnp.exp(_log_sigmoid(jnp.full((1, 1), df_ref[h], F32)) * float(ch))
    dq_b = jnp.exp(lgb_c * (ch - col))
    dk_b = jnp.exp(lgb_c * col)
    dc_b = jnp.exp(_log_sigmoid(jnp.full((1, 1), db_ref[h], F32)) * float(ch))
    tdims = (((0,), (0,)), ((), ()))
    ndims = (((1,), (1,)), ((), ()))

    if has_init:
        sf_ref[...] = s0f_ref[...]
        sb_ref[...] = s0b_ref[...]
    else:
        sf_ref[...] = jnp.zeros_like(sf_ref)
        sb_ref[...] = jnp.zeros_like(sb_ref)

    def rows(c):
        return pl.ds(pl.multiple_of(c * ch, ch), ch)

    def cross(c, s_ref, dq, dk, dc, first):
        r = rows(c)
        q = q_ref[r, :]
        k = k_ref[r, :]
        v = v_ref[r, :]
        s = s_ref[...]
        term = jnp.dot(q, s.astype(BF16), preferred_element_type=F32) * dq
        if first:
            oc_ref[r, :] = term
        else:
            oc_ref[r, :] += term
        kd = (k.astype(F32) * dk).astype(BF16)
        s_ref[...] = s * dc + lax.dot_general(kd, v, tdims, preferred_element_type=F32)

    def sweep(first):
        def step(t, carry):
            cross(t, sf_ref, dq_f, dk_f, dc_f, first)
            cross(n - 1 - t, sb_ref, dq_b, dk_b, dc_b, first)
            return carry
        return step

    half = n // 2
    if n % 2:
        oc_ref[pl.ds(half * ch, ch), :] = jnp.zeros((ch, oc_ref.shape[1]), F32)
    if half:
        lax.fori_loop(0, half, sweep(True), 0, unroll=min(unroll, half))
    lax.fori_loop(half, n, sweep(False), 0, unroll=min(unroll, n - half))
    if emit_state:
        sff_ref[...] = sf_ref[...]
        sfb_ref[...] = sb_ref[...]

    def finish(c, carry):
        r = rows(c)
        q = q_ref[r, :]
        scores = lax.dot_general(q, k_ref[r, :], ndims, preferred_element_type=F32) * decay
        o = jnp.dot(scores.astype(BF16), v_ref[r, :], preferred_element_type=F32) + oc_ref[r, :]
        y = o * lax.rsqrt(jnp.mean(o * o, axis=-1, keepdims=True) + NORM_EPS)
        hg = 0.5 * g_ref[r, :].astype(F32)
        y_ref[r, :] = (hg * (1.0 + jnp.tanh(hg)) * y).astype(y_ref.dtype)
        return carry

    lax.fori_loop(0, n, finish, 0, unroll=min(unroll, n))


def _ret_call(qkvg, decay_f, decay_b, init=None, *, batch, seq, ch, dk, dv, emit_state, name="retention"):
    heads = RET_HEADS
    smem = pl.BlockSpec(memory_space=pltpu.SMEM)
    state = pl.BlockSpec((None, None, dk, dv), lambda b, h: (b, h, 0, 0))
    state_shape = jax.ShapeDtypeStruct((batch, heads, dk, dv), F32)
    n_init = 0 if init is None else 2
    n_fin = 2 if emit_state else 0
    return pl.pallas_call(
        functools.partial(_ret_kernel, seq=seq, ch=ch, unroll=4, has_init=init is not None,
                          emit_state=emit_state),
        grid=(batch, heads),
        in_specs=[
            smem, smem,
            pl.BlockSpec((seq, dk), lambda b, h: (b, h)),
            pl.BlockSpec((seq, dk), lambda b, h: (b, heads + h)),
            pl.BlockSpec((seq, dv), lambda b, h: (b, heads + h)),
            pl.BlockSpec((seq, dv), lambda b, h: (b, 2 * heads + h)),
        ] + [state] * n_init,
        out_specs=[pl.BlockSpec((seq, dv), lambda b, h: (b, h))] + [state] * n_fin,
        out_shape=[jax.ShapeDtypeStruct((batch * seq, heads * dv), BF16)] + [state_shape] * n_fin,
        scratch_shapes=[
            pltpu.VMEM((seq, dv), F32),
            pltpu.VMEM((dk, dv), F32),
            pltpu.VMEM((dk, dv), F32),
        ],
        compiler_params=_params(("parallel", "arbitrary")),
        name=name,
    )(decay_f, decay_b, qkvg, qkvg, qkvg, qkvg, *(init or ()))


def _attn_kernel(sink_ref, q_ref, kp_ref, kc_ref, kn_ref, vp_ref, vc_ref, vn_ref, kx_ref, vx_ref,
                 o_ref, *, nq, tq, group):
    n = pl.program_id(1)
    dh = ATT_HEAD_DIM
    n_ctx = kx_ref.shape[0]
    win = tq + 2 * WINDOW
    rows = group * tq
    nk = win + n_ctx
    scale = dh ** -0.5
    assert tq == WINDOW
    rq = lax.broadcasted_iota(jnp.int32, (rows, WINDOW), 0) % tq
    ck = lax.broadcasted_iota(jnp.int32, (rows, WINDOW), 1)
    ok_prev = (ck >= rq) & (n > 0)
    ok_next = (ck <= rq) & (n < nq - 1)
    rcol = lax.broadcasted_iota(jnp.int32, (rows, 1), 0) // tq
    ndims = (((1,), (1,)), ((), ()))
    ones = jnp.ones((nk, dh), BF16)

    lane = lax.broadcasted_iota(jnp.int32, (tq, 2 * dh), 1)
    for kh in range(ATT_KV_HEADS):
        pair, e = divmod(kh, 2)
        mine = (lane // (dh // 2)) % 2 == e
        hs = slice(kh * dh, (kh + 1) * dh)
        ps = slice(pair * 2 * dh, (pair + 1) * 2 * dh)
        qs = jnp.concatenate(
            [jnp.where(mine, q_ref[:, (pair * group + g) * 2 * dh:(pair * group + g + 1) * 2 * dh], 0)
             for g in range(group)], axis=0)
        k_all = jnp.concatenate([kp_ref[:, ps], kc_ref[:, ps], kn_ref[:, ps], kx_ref[:, ps]], axis=0)
        v_all = jnp.concatenate([vp_ref[:, hs], vc_ref[:, hs], vn_ref[:, hs], vx_ref[:, hs]], axis=0)
        s = lax.dot_general(qs, k_all, ndims, preferred_element_type=F32) * (scale * LOG2E)
        s = jnp.concatenate([jnp.where(ok_prev, s[:, :WINDOW], NEG_INF), s[:, WINDOW:WINDOW + tq],
                             jnp.where(ok_next, s[:, WINDOW + tq:win], NEG_INF), s[:, win:]], axis=1)
        sink = jnp.zeros((rows, 1), F32)
        for g in range(group):
            sink = jnp.where(rcol == g, sink_ref[kh * group + g] * LOG2E, sink)
        m = jnp.maximum(jnp.max(s, axis=-1, keepdims=True), sink)
        p = jnp.exp2(s - m).astype(BF16)
        pv = jnp.dot(p, jnp.concatenate([v_all, ones], axis=1), preferred_element_type=F32)
        o = pv[:, :dh] / (pv[:, dh:dh + 1] + jnp.exp2(sink - m))
        for g in range(group):
            o_ref[:, (kh * group + g) * dh:(kh * group + g + 1) * dh] = (
                o[g * tq:(g + 1) * tq, :].astype(o_ref.dtype))


def _attn_call(qkv, kv_ctx, sink, *, batch, seq, n_ctx, tq, name="window_attn"):
    dh = ATT_HEAD_DIM
    kvw = ATT_KV_HEADS * dh
    qw = qkv.shape[1] - 2 * kvw
    group = qw // kvw
    nq = seq // tq
    wpb = seq // WINDOW
    step = tq // WINDOW
    k_blk = qw // kvw
    v_blk = k_blk + 1

    def prev_map(col):
        return lambda b, n: (b * wpb + jnp.maximum(n * step - 1, 0), col)

    def cur_map(col):
        return lambda b, n: (b * nq + n, col)

    def next_map(col):
        return lambda b, n: (b * wpb + jnp.minimum((n + 1) * step, wpb - 1), col)

    return pl.pallas_call(
        functools.partial(_attn_kernel, nq=nq, tq=tq, group=group),
        grid=(batch, nq),
        in_specs=[
            pl.BlockSpec(memory_space=pltpu.SMEM),
            pl.BlockSpec((tq, qw), lambda b, n: (b * nq + n, 0)),
            pl.BlockSpec((WINDOW, kvw), prev_map(k_blk)),
            pl.BlockSpec((tq, kvw), cur_map(k_blk)),
            pl.BlockSpec((WINDOW, kvw), next_map(k_blk)),
            pl.BlockSpec((WINDOW, kvw), prev_map(v_blk)),
            pl.BlockSpec((tq, kvw), cur_map(v_blk)),
            pl.BlockSpec((WINDOW, kvw), next_map(v_blk)),
            pl.BlockSpec((n_ctx, kvw), lambda b, n: (b, 0)),
            pl.BlockSpec((n_ctx, kvw), lambda b, n: (b, 1)),
        ],
        out_specs=pl.BlockSpec((tq, qw), lambda b, n: (b * nq + n, 0)),
        out_shape=jax.ShapeDtypeStruct((batch * seq, qw), BF16),
        compiler_params=_params(("parallel", "arbitrary")),
        name=name,
    )(sink, qkv, qkv, qkv, qkv, qkv, qkv, qkv, kv_ctx, kv_ctx)


def _rope_angles(pos, dim, base):
    inv_freq = base ** (-np.arange(0, dim, 2, dtype=np.float64) / dim)
    return pos.astype(np.float64)[:, None] * inv_freq[None, :]


def _table(cos_parts, sin_parts, tab_fn):
    return (jnp.asarray(np.concatenate(cos_parts), F32), jnp.asarray(np.concatenate(sin_parts), F32), tab_fn)


def _axial_tables(seq):
    rows = np.arange(seq) // GRID_W
    cols = np.arange(seq) % GRID_W
    half = ATT_HEAD_DIM // 2
    ar = _rope_angles(rows, half, ROPE_BASE)
    ac = _rope_angles(cols, half, ROPE_BASE)
    cos = np.concatenate([np.cos(ar), np.cos(ac), np.cos(ar), np.cos(ac)], axis=-1)
    sin = np.concatenate([np.sin(ar), np.sin(ac), np.sin(ar), np.sin(ac)], axis=-1)
    return cos, sin


def _pair_layout_blocks(n_q_heads, n_kv_heads):
    group = n_q_heads // n_kv_heads

    def pair(a, b):
        return [4 * a, 4 * a + 2, 4 * b, 4 * b + 2, 4 * a + 1, 4 * a + 3, 4 * b + 1, 4 * b + 3]

    blocks = []
    for p in range(n_kv_heads // 2):
        for g in range(group):
            blocks += pair(2 * p * group + g, (2 * p + 1) * group + g)
    for p in range(n_kv_heads // 2):
        blocks += pair(n_q_heads + 2 * p, n_q_heads + 2 * p + 1)
    blocks += range(4 * (n_q_heads + n_kv_heads), 4 * (n_q_heads + 2 * n_kv_heads))
    return blocks


def kernel(x, c, ctx, c_ctx, ada_w, ada_b, norm_mix_g, norm_mlp_g, mlp_w1, mlp_w2, ret_w_in, ret_w_out,
           ret_decay_fwd, ret_decay_bwd, attn_w_in, attn_w_out, attn_sink, final_norm_g):
    batch, seq, d = x.shape
    n_ctx = ctx.shape[1]
    assert ada_w.shape[0] == 2 and ret_w_in.shape[0] == 1 and attn_w_in.shape[0] == 1
    assert batch + 1 <= MOD_ROWS
    ret_dk = d // RET_HEADS
    ret_dv = 2 * ret_dk
    qk_w = RET_HEADS * ret_dk
    att_qw = attn_w_out.shape[1]
    att_kvw = ATT_KV_HEADS * ATT_HEAD_DIM
    ctx_row = batch

    cond = jnp.concatenate([c, c_ctx[None, :], jnp.zeros((MOD_ROWS - batch - 1, d), F32)], axis=0)
    mod = _ada_call(cond, ada_w, ada_b)
    mod4 = mod.reshape(mod.shape[0], MOD_ROWS, 1, mod.shape[2])

    xl = x.reshape(batch * seq, d)
    xc = ctx.reshape(batch * n_ctx, d)
    t_ctx = batch * n_ctx
    lat = dict(rows_per_mod=seq, mod_row0=0)
    cx = dict(rows_per_mod=t_ctx, mod_row0=ctx_row)

    ret_ang = _rope_angles(np.arange(seq), ret_dk, ROPE_BASE)
    ret_cos, ret_sin = np.cos(ret_ang), np.sin(ret_ang)
    att_cos, att_sin = _axial_tables(seq)

    tm, tn, tn_ctx = 1024, 1024, 512
    k_scale = ret_dk ** -0.5
    tps = seq // tm
    one = np.ones((tm, LANES))
    zero = np.zeros((tm, LANES))

    def ret_tab_lat(i, j):
        q_tiles = qk_w // tn
        return jnp.where(j < q_tiles, i % tps, jnp.where(j < 2 * q_tiles, tps + i % tps, 2 * tps))

    def ret_tab_ctx(i, j):
        q_tiles = qk_w // tn_ctx
        return jnp.where((j >= q_tiles) & (j < 2 * q_tiles), 1, 0)

    assert t_ctx == tm
    ctx_tab = _table([one, one * k_scale], [zero, zero], ret_tab_ctx)
    p_ctx, w_in = _proj_call(xc, norm_mix_g[0], mod4, 0, ret_w_in, tm=tm, tn=tn_ctx, rope="half",
                             tables=[ctx_tab], emit_w=True, name="ret_proj_ctx", **cx)
    assert ATT_KV_HEADS % 2 == 0
    pair_order = tuple(_pair_layout_blocks(att_qw // ATT_HEAD_DIM, ATT_KV_HEADS))
    casts = [(ret_w_out, 0, 32, None), (mlp_w1, 0, 16, None), (mlp_w2, 0, 64, None),
             (attn_w_in, 0, 16, pair_order), (attn_w_out, 0, 16, None)]
    lat_tab = _table([ret_cos, ret_cos * k_scale, one], [ret_sin, ret_sin * k_scale, zero], ret_tab_lat)
    p_lat, w_out, w1, w2, att_w_in, att_w_out = _proj_call(
        xl, norm_mix_g[0], mod4, 0, w_in, tm=tm, tn=tn, rope="half", tables=[lat_tab],
        ns=8, casts=casts, name="ret_proj_lat", **lat)
    ret = dict(batch=batch, dk=ret_dk, dv=ret_dv)
    y_ctx, sc_f, sc_b = _ret_call(p_ctx, ret_decay_fwd[0], ret_decay_bwd[0], seq=n_ctx, ch=min(256, n_ctx),
                                  emit_state=True, name="retention_ctx", **ret)
    y_lat, = _ret_call(p_lat, ret_decay_fwd[0], ret_decay_bwd[0], (sc_f, sc_b), seq=seq, ch=256,
                       emit_state=False, name="retention_lat", **ret)
    xl = _out_call(y_lat, w_out, xl, mod4, 0, tm=1024, tn=1024, name="ret_out_lat", **lat)
    xc = _out_call(y_ctx, w_out, xc, mod4, 0, tm=t_ctx, tn=512, name="ret_out_ctx", **cx)
    xl, w1_next, w2_next = _mlp_call(xl, norm_mlp_g[0], mod4, 0, w1, w2, tm=1024, tf=1024,
                                     casts=[(mlp_w1, 1, 16, None), (mlp_w2, 1, 64, None)], name="mlp0_lat",
                                     **lat)
    xc, = _mlp_call(xc, norm_mlp_g[0], mod4, 0, w1, w2, tm=t_ctx, tf=1024, name="mlp0_ctx", **cx)

    kv_ctx, = _proj_call(xc, norm_mix_g[1], mod4, 1, att_w_in, tm=t_ctx, tn=512, n_off=att_qw // 512,
                         n_cols=2 * att_kvw, name="attn_proj_ctx", **cx)
    att_tn = 2 * att_kvw
    assert att_qw % att_tn == 0
    q_tiles = att_qw // att_tn
    qkv, = _proj_call(xl, norm_mix_g[1], mod4, 1, att_w_in, tm=tm, tn=att_tn, rope="half",
                      tables=[_table([att_cos, one], [att_sin, zero], lambda i, j: i % tps),
                              _table([att_cos, one], [att_sin, zero],
                                     lambda i, j: jnp.where(j < q_tiles, i % tps, tps))],
                      ns=2, name="attn_proj_lat", **lat)
    o = _attn_call(qkv, kv_ctx, attn_sink[0], batch=batch, seq=seq, n_ctx=n_ctx, tq=WINDOW)
    xl = _out_call(o, att_w_out, xl, mod4, 1, tm=1024, tn=1024, name="attn_out_lat", **lat)
    xl, = _mlp_call(xl, norm_mlp_g[1], mod4, 1, w1_next, w2_next, tm=1024, tf=1024, final_g=final_norm_g,
                    name="mlp1_lat", **lat)
    return xl.reshape(batch, seq, d)
```

```python
import functools

import jax
import numpy as np
import jax.numpy as jnp
from jax import lax
from jax.experimental import pallas as pl
from jax.experimental.pallas import tpu as pltpu

F32 = jnp.float32
BF16 = jnp.bfloat16

NORM_EPS = 1e-6
NEG_INF = -1e30
LOG2E = 1.4426950408889634
GRID_W = 64
ROPE_BASE = 10000.0

RET_HEADS = 8
ATT_HEAD_DIM = 128
ATT_KV_HEADS = 4
WINDOW = 128
LANES = 128

MOD_ROWS = 8
VMEM_LIMIT = 56 * 1024 * 1024
MLP_VMEM_LIMIT = 64 * 1024 * 1024 - 256 * 1024
X_LEAD = 2


def _params(semantics, vmem=VMEM_LIMIT):
    return pltpu.CompilerParams(dimension_semantics=semantics, vmem_limit_bytes=vmem)


def _norm_mod(x, g, shift, scale):
    ms = jnp.mean(x * x, axis=-1, keepdims=True)
    return x * lax.rsqrt(ms + NORM_EPS) * (g * (1.0 + scale)) + shift


def _ada_kernel(cond_ref, w_ref, b_ref, o_ref):
    c = cond_ref[...]
    s = c * jax.nn.sigmoid(c)
    o_ref[...] = jnp.dot(s.astype(BF16), w_ref[...].astype(BF16),
                         preferred_element_type=F32) + b_ref[...]


def _ada_call(cond, ada_w, ada_b, tn=1024):
    depth, d, n = ada_w.shape
    return pl.pallas_call(
        _ada_kernel,
        grid=(depth, n // tn),
        in_specs=[
            pl.BlockSpec((MOD_ROWS, d), lambda l, j: (0, 0)),
            pl.BlockSpec((None, d, tn), lambda l, j: (l, 0, j)),
            pl.BlockSpec((None, 1, tn), lambda l, j: (l, 0, j)),
        ],
        out_specs=pl.BlockSpec((None, MOD_ROWS, tn), lambda l, j: (l, 0, j)),
        out_shape=jax.ShapeDtypeStruct((depth, MOD_ROWS, n), F32),
        compiler_params=_params(("arbitrary", "arbitrary")),
        name="ada_mod",
    )(cond, ada_w, ada_b.reshape(depth, 1, n))


def _cast_plumbing(jobs, n_steps, step_of):
    in_specs, args, out_specs, out_shapes = [], [], [], []
    for arr, layer, rows, _ in jobs:
        r, c = arr.shape[-2:]
        nblk = r // rows
        assert n_steps >= nblk and r % rows == 0
        rep = n_steps // nblk

        def blk(i, j, rep=rep, nblk=nblk):
            return jnp.minimum(step_of(i, j) // rep, nblk - 1)

        in_specs.append(pl.BlockSpec((None, rows, c), lambda i, j, blk=blk, layer=layer: (layer, blk(i, j), 0)))
        args.append(arr)
        out_specs.append(pl.BlockSpec((rows, c), lambda i, j, blk=blk: (blk(i, j), 0)))
        out_shapes.append(jax.ShapeDtypeStruct((r, c), BF16))
    return in_specs, args, out_specs, out_shapes


def _rotate_store(acc, o_ref, rope, tabs):
    tn = acc.shape[1]
    if not rope:
        o_ref[...] = acc.astype(o_ref.dtype)
        return
    groups = tn // (2 * LANES)
    for hh in range(groups):
        cos_ref, sin_ref = tabs[hh * len(tabs) // groups]
        cos = cos_ref[...]
        sin = sin_ref[...]
        lo = slice(2 * hh * LANES, (2 * hh + 1) * LANES)
        hi = slice((2 * hh + 1) * LANES, (2 * hh + 2) * LANES)
        x1 = acc[:, lo]
        x2 = acc[:, hi]
        o_ref[:, lo] = (x1 * cos - x2 * sin).astype(o_ref.dtype)
        o_ref[:, hi] = (x2 * cos + x1 * sin).astype(o_ref.dtype)


def _cast_block(v, order):
    if order is not None:
        q = v.shape[1] // len(order)
        v = jnp.concatenate([v[:, b * q:(b + 1) * q] for b in order], axis=1)
    return v.astype(BF16)


def _proj_kernel(*refs, rope, n_tab, cast_orders, emit_w, ns, nm, mod_row0, tiles_per_mod):
    refs = list(refs)
    xs_ref = refs.pop(0) if ns else None
    x_ref, g_ref, shift_ref, scale_ref, w_ref = refs[:5]
    refs = refs[5:]
    tabs = [(refs[2 * k], refs[2 * k + 1]) for k in range(n_tab)]
    refs = refs[2 * n_tab:]
    n_cast = len(cast_orders)
    cast_src, refs = refs[:n_cast], refs[n_cast:]
    o_ref = refs.pop(0)
    wout_ref = refs.pop(0) if emit_w else None
    cast_dst, a_refs = refs[:n_cast], refs[n_cast:]
    i = pl.program_id(0)
    j = pl.program_id(1)

    def prologue(x, tile):
        b = mod_row0 + tile // tiles_per_mod
        return _norm_mod(x, g_ref[...], shift_ref[b], scale_ref[b]).astype(BF16)

    def body(a_cur, a_nxt):
        w = w_ref[...]
        if emit_w:
            w = w.astype(BF16)
            wout_ref[...] = w
        acc = jnp.dot(a_cur[...], w, preferred_element_type=F32)
        _rotate_store(acc, o_ref, rope, tabs)
        for src, dst, order in zip(cast_src, cast_dst, cast_orders):
            dst[...] = _cast_block(src[...], order)
        if a_nxt is not None:
            @pl.when(j < ns)
            def _():
                rs = xs_ref.shape[0]
                a_nxt[pl.ds(pl.multiple_of(j * rs, rs), rs), :] = prologue(
                    xs_ref[...], jnp.minimum(i + 1, nm - 1))

    if ns == 0:
        @pl.when(j == 0)
        def _():
            a_refs[0][...] = prologue(x_ref[...], i)

        body(a_refs[0], None)
        return

    @pl.when((i == 0) & (j == 0))
    def _():
        a_refs[0][...] = prologue(x_ref[...], 0)

    @pl.when(i % 2 == 0)
    def _():
        body(a_refs[0], a_refs[1])

    @pl.when(i % 2 == 1)
    def _():
        body(a_refs[1], a_refs[0])


def _proj_call(x2d, g, mod4, layer, w, *, tm, tn, rows_per_mod, mod_row0, n_off=0, n_cols=None,
               rope=None, tables=(), ns=0, emit_w=False, casts=(), vmem=VMEM_LIMIT, name="proj"):
    t, d = x2d.shape
    n_cols = w.shape[-1] if n_cols is None else n_cols
    nt = n_cols // tn
    nm = t // tm
    assert t % tm == 0 and n_cols % tn == 0 and rows_per_mod % tm == 0
    assert not emit_w or nm == 1
    assert ns == 0 or (tm % ns == 0 and ns <= nt and nm > 1)

    def mod_spec(k):
        return pl.BlockSpec((None, MOD_ROWS, 1, d), lambda i, j: (layer, 0, 0, k))

    in_specs, args = [], []
    if ns:
        in_specs.append(pl.BlockSpec(
            (tm // ns, d), lambda i, j: (jnp.minimum(i + 1, nm - 1) * ns + jnp.minimum(j, ns - 1), 0)))
        in_specs.append(pl.BlockSpec((tm, d), lambda i, j: (0, 0), pipeline_mode=pl.Buffered(1)))
        args += [x2d, x2d]
    else:
        in_specs.append(pl.BlockSpec((tm, d), lambda i, j: (i, 0)))
        args.append(x2d)
    in_specs += [pl.BlockSpec((1, d), lambda i, j: (0, 0)), mod_spec(0), mod_spec(1)]
    args += [g.reshape(1, d), mod4, mod4]
    if emit_w:
        in_specs.append(pl.BlockSpec((None, d, tn), lambda i, j: (0, 0, n_off + j)))
    else:
        in_specs.append(pl.BlockSpec((d, tn), lambda i, j: (0, n_off + j)))
    args.append(w)
    for cos, sin, tab_fn in tables:
        in_specs += [pl.BlockSpec((tm, LANES), lambda i, j, tab_fn=tab_fn: (tab_fn(i, j), 0))] * 2
        args += [cos, sin]
    c_in, c_args, c_out, c_shapes = _cast_plumbing(casts, nm * nt, lambda i, j: i * nt + j)
    out_specs = [pl.BlockSpec((tm, tn), lambda i, j: (i, j))]
    out_shapes = [jax.ShapeDtypeStruct((t, n_cols), BF16)]
    if emit_w:
        out_specs.append(pl.BlockSpec((d, tn), lambda i, j: (0, j)))
        out_shapes.append(jax.ShapeDtypeStruct((d, n_cols), BF16))
    return pl.pallas_call(
        functools.partial(_proj_kernel, rope=rope, n_tab=len(tables),
                          cast_orders=tuple(job[3] for job in casts), emit_w=emit_w, ns=ns, nm=nm,
                          mod_row0=mod_row0, tiles_per_mod=rows_per_mod // tm),
        grid=(nm, nt),
        in_specs=in_specs + c_in,
        out_specs=out_specs + c_out,
        out_shape=out_shapes + c_shapes,
        scratch_shapes=[pltpu.VMEM((tm, d), BF16)] * (2 if ns else 1),
        compiler_params=_params(("arbitrary", "arbitrary"), vmem=vmem),
        name=name,
    )(*args, *c_args)


def _out_kernel(y_ref, w_ref, x_ref, gate_ref, o_ref):
    acc = jnp.dot(y_ref[...], w_ref[...], preferred_element_type=F32)
    o_ref[...] = x_ref[...] + gate_ref[...] * acc


def _out_call(y, w, x2d, mod4, layer, *, tm, tn, rows_per_mod, mod_row0, name="out_proj"):
    t, k = y.shape
    d = w.shape[1]
    tiles_per_mod = rows_per_mod // tm
    gate_blk0 = 2 * d // tn
    return pl.pallas_call(
        _out_kernel,
        grid=(t // tm, d // tn),
        in_specs=[
            pl.BlockSpec((tm, k), lambda i, j: (i, 0)),
            pl.BlockSpec((k, tn), lambda i, j: (0, j)),
            pl.BlockSpec((tm, tn), lambda i, j: (i, j)),
            pl.BlockSpec((None, None, 1, tn),
                         lambda i, j: (layer, mod_row0 + i // tiles_per_mod, 0, gate_blk0 + j)),
        ],
        out_specs=pl.BlockSpec((tm, tn), lambda i, j: (i, j)),
        out_shape=jax.ShapeDtypeStruct((t, d), F32),
        compiler_params=_params(("parallel", "arbitrary")),
        name=name,
    )(y, w, x2d, mod4)


def _mlp_kernel(*refs, nf, final, n_cast, nm, mod_row0, tiles_per_mod):
    refs = list(refs)
    xs_ref, x_hbm, g_ref, shift_ref, scale_ref, gate_ref, w1_ref, w2_ref = refs[:8]
    refs = refs[8:]
    fg_ref = refs.pop(0) if final else None
    cast_src, refs = refs[:n_cast], refs[n_cast:]
    o_ref = refs.pop(0)
    cast_dst, refs = refs[:n_cast], refs[n_cast:]
    a_even, a_odd, x_buf, x_sem = refs
    i = pl.program_id(0)
    f = pl.program_id(1)
    tm = x_buf.shape[0]
    rs = xs_ref.shape[0]

    def x_copy(tile):
        return pltpu.make_async_copy(x_hbm.at[pl.ds(pl.multiple_of(tile * tm, tm), tm), :], x_buf, x_sem)

    def mod_row(tile):
        return mod_row0 + tile // tiles_per_mod

    def prologue(x, tile):
        b = mod_row(tile)
        return _norm_mod(x, g_ref[...], shift_ref[b], scale_ref[b]).astype(BF16)

    @pl.when((i == 0) & (f == 0))
    def _():
        first = x_copy(0)
        first.start()
        first.wait()
        a_even[...] = prologue(x_buf[...], 0)

    @pl.when(f == nf - 1 - X_LEAD)
    def _():
        x_copy(i).start()

    def body(a_cur, a_nxt):
        @pl.when(f == 0)
        def _():
            o_ref[...] = jnp.zeros_like(o_ref)

        h = jnp.dot(a_cur[...], w1_ref[...], preferred_element_type=F32)
        h = jnp.square(jnp.maximum(h, 0.0)).astype(BF16)
        o_ref[...] += jnp.dot(h, w2_ref[...], preferred_element_type=F32)
        for src, dst in zip(cast_src, cast_dst):
            dst[...] = src[...].astype(BF16)
        a_nxt[pl.ds(pl.multiple_of(f * rs, rs), rs), :] = prologue(xs_ref[...], jnp.minimum(i + 1, nm - 1))

        @pl.when(f == nf - 1)
        def _():
            x_copy(i).wait()
            r = x_buf[...] + gate_ref[mod_row(i)] * o_ref[...]
            if final:
                ms = jnp.mean(r * r, axis=-1, keepdims=True)
                r = r * lax.rsqrt(ms + NORM_EPS) * fg_ref[...]
            o_ref[...] = r

    @pl.when(i % 2 == 0)
    def _():
        body(a_even, a_odd)

    @pl.when(i % 2 == 1)
    def _():
        body(a_odd, a_even)


def _mlp_call(x2d, g, mod4, layer, w1, w2, *, tm, tf, rows_per_mod, mod_row0, final_g=None, casts=(),
              name="mlp"):
    t, d = x2d.shape
    ff = w1.shape[1]
    nf = ff // tf
    nm = t // tm
    assert t % tm == 0 and ff % tf == 0 and rows_per_mod % tm == 0
    assert tm % nf == 0 and (tm // nf) % 16 == 0 and nf > X_LEAD

    def mod_spec(k):
        return pl.BlockSpec((None, MOD_ROWS, 1, d), lambda i, f: (layer, 0, 0, k))

    in_specs = [
        pl.BlockSpec((tm // nf, d), lambda i, f: (jnp.minimum(i + 1, nm - 1) * nf + f, 0)),
        pl.BlockSpec(memory_space=pl.ANY),
        pl.BlockSpec((1, d), lambda i, f: (0, 0)),
        mod_spec(3), mod_spec(4), mod_spec(5),
        pl.BlockSpec((d, tf), lambda i, f: (0, f)),
        pl.BlockSpec((tf, d), lambda i, f: (f, 0)),
    ]
    args = [x2d, x2d, g.reshape(1, d), mod4, mod4, mod4, w1, w2]
    if final_g is not None:
        in_specs.append(pl.BlockSpec((1, d), lambda i, f: (0, 0)))
        args.append(final_g.reshape(1, d))
    c_in, c_args, c_out, c_shapes = _cast_plumbing(casts, nm * nf, lambda i, f: i * nf + f)
    return pl.pallas_call(
        functools.partial(_mlp_kernel, nf=nf, final=final_g is not None, n_cast=len(casts), nm=nm,
                          mod_row0=mod_row0, tiles_per_mod=rows_per_mod // tm),
        grid=(nm, nf),
        in_specs=in_specs + c_in,
        out_specs=[pl.BlockSpec((tm, d), lambda i, f: (i, 0))] + c_out,
        out_shape=[jax.ShapeDtypeStruct((t, d), F32)] + c_shapes,
        scratch_shapes=[pltpu.VMEM((tm, d), BF16), pltpu.VMEM((tm, d), BF16), pltpu.VMEM((tm, d), F32),
                        pltpu.SemaphoreType.DMA(())],
        compiler_params=_params(("arbitrary", "arbitrary"), vmem=MLP_VMEM_LIMIT),
        name=name,
    )(*args, *c_args)


def _log_sigmoid(x):
    return jnp.minimum(x, 0.0) - jnp.log1p(jnp.exp(-jnp.abs(x)))


def _ret_kernel(*refs, seq, ch, unroll, has_init, emit_state):
    refs = list(refs)
    df_ref, db_ref, q_ref, k_ref, v_ref, g_ref = refs[:6]
    refs = refs[6:]
    s0f_ref, s0b_ref = (refs.pop(0), refs.pop(0)) if has_init else (None, None)
    y_ref = refs.pop(0)
    sff_ref, sfb_ref = (refs.pop(0), refs.pop(0)) if emit_state else (None, None)
    oc_ref, sf_ref, sb_ref = refs
    h = pl.program_id(1)
    n = seq // ch
    lgf = _log_sigmoid(jnp.full((ch, ch), df_ref[h], F32))
    lgb = _log_sigmoid(jnp.full((ch, ch), db_ref[h], F32))
    ii = lax.broadcasted_iota(jnp.int32, (ch, ch), 0)
    jj = lax.broadcasted_iota(jnp.int32, (ch, ch), 1)
    diff = (ii - jj).astype(F32)
    decay = (jnp.where(diff >= 0, jnp.exp(lgf * jnp.maximum(diff, 0.0)), 0.0)
             + jnp.where(diff <= 0, jnp.exp(lgb * jnp.maximum(-diff, 0.0)), 0.0))
    col = lax.broadcasted_iota(jnp.int32, (ch, 1), 0).astype(F32)
    lgf_c = _log_sigmoid(jnp.full((ch, 1), df_ref[h], F32))
    lgb_c = _log_sigmoid(jnp.full((ch, 1), db_ref[h], F32))
    dq_f = jnp.exp(lgf_c * (col + 1.0))
    dk_f = jnp.exp(lgf_c * (ch - 1.0 - col))
    dc_f = jnp.exp(_log_sigmoid(jnp.full((1, 1), df_ref[h], F32)) * float(ch))
    dq_b = jnp.exp(lgb_c * (ch - col))
    dk_b = jnp.exp(lgb_c * col)
    dc_b = jnp.exp(_log_sigmoid(jnp.full((1, 1), db_ref[h], F32)) * float(ch))
    tdims = (((0,), (0,)), ((), ()))
    ndims = (((1,), (1,)), ((), ()))

    if has_init:
        sf_ref[...] = s0f_ref[...]
        sb_ref[...] = s0b_ref[...]
    else:
        sf_ref[...] = jnp.zeros_like(sf_ref)
        sb_ref[...] = jnp.zeros_like(sb_ref)

    def rows(c):
        return pl.ds(pl.multiple_of(c * ch, ch), ch)

    def cross(c, s_ref, dq, dk, dc, first):
        r = rows(c)
        q = q_ref[r, :]
        k = k_ref[r, :]
        v = v_ref[r, :]
        s = s_ref[...]
        term = jnp.dot(q, s.astype(BF16), preferred_element_type=F32) * dq
        if first:
            oc_ref[r, :] = term
        else:
            oc_ref[r, :] += term
        kd = (k.astype(F32) * dk).astype(BF16)
        s_ref[...] = s * dc + lax.dot_general(kd, v, tdims, preferred_element_type=F32)

    def sweep(first):
        def step(t, carry):
            cross(t, sf_ref, dq_f, dk_f, dc_f, first)
            cross(n - 1 - t, sb_ref, dq_b, dk_b, dc_b, first)
            return carry
        return step

    half = n // 2
    if n % 2:
        oc_ref[pl.ds(half * ch, ch), :] = jnp.zeros((ch, oc_ref.shape[1]), F32)
    if half:
        lax.fori_loop(0, half, sweep(True), 0, unroll=min(unroll, half))
    lax.fori_loop(half, n, sweep(False), 0, unroll=min(unroll, n - half))
    if emit_state:
        sff_ref[...] = sf_ref[...]
        sfb_ref[...] = sb_ref[...]

    def finish(c, carry):
        r = rows(c)
        q = q_ref[r, :]
        scores = lax.dot_general(q, k_ref[r, :], ndims, preferred_element_type=F32) * decay
        o = jnp.dot(scores.astype(BF16), v_ref[r, :], preferred_element_type=F32) + oc_ref[r, :]
        y = o * lax.rsqrt(jnp.mean(o * o, axis=-1, keepdims=True) + NORM_EPS)
        hg = 0.5 * g_ref[r, :].astype(F32)
        y_ref[r, :] = (hg * (1.0 + jnp.tanh(hg)) * y).astype(y_ref.dtype)
        return carry

    lax.fori_loop(0, n, finish, 0, unroll=min(unroll, n))


def _ret_call(qkvg, decay_f, decay_b, init=None, *, batch, seq, ch, dk, dv, emit_state, name="retention"):
    heads = RET_HEADS
    smem = pl.BlockSpec(memory_space=pltpu.SMEM)
    state = pl.BlockSpec((None, None, dk, dv), lambda b, h: (b, h, 0, 0))
    state_shape = jax.ShapeDtypeStruct((batch, heads, dk, dv), F32)
    n_init = 0 if init is None else 2
    n_fin = 2 if emit_state else 0
    return pl.pallas_call(
        functools.partial(_ret_kernel, seq=seq, ch=ch, unroll=4, has_init=init is not None,
                          emit_state=emit_state),
        grid=(batch, heads),
        in_specs=[
            smem, smem,
            pl.BlockSpec((seq, dk), lambda b, h: (b, h)),
            pl.BlockSpec((seq, dk), lambda b, h: (b, heads + h)),
            pl.BlockSpec((seq, dv), lambda b, h: (b, heads + h)),
            pl.BlockSpec((seq, dv), lambda b, h: (b, 2 * heads + h)),
        ] + [state] * n_init,
        out_specs=[pl.BlockSpec((seq, dv), lambda b, h: (b, h))] + [state] * n_fin,
        out_shape=[jax.ShapeDtypeStruct((batch * seq, heads * dv), BF16)] + [state_shape] * n_fin,
        scratch_shapes=[
            pltpu.VMEM((seq, dv), F32),
            pltpu.VMEM((dk, dv), F32),
            pltpu.VMEM((dk, dv), F32),
        ],
        compiler_params=_params(("parallel", "arbitrary")),
        name=name,
    )(decay_f, decay_b, qkvg, qkvg, qkvg, qkvg, *(init or ()))


def _attn_kernel(sink_ref, q_ref, kp_ref, kc_ref, kn_ref, vp_ref, vc_ref, vn_ref, kx_ref, vx_ref,
                 o_ref, *, nq, tq, group):
    n = pl.program_id(1)
    dh = ATT_HEAD_DIM
    n_ctx = kx_ref.shape[0]
    win = tq + 2 * WINDOW
    rows = group * tq
    nk = win + n_ctx
    scale = dh ** -0.5
    assert tq == WINDOW
    rq = lax.broadcasted_iota(jnp.int32, (rows, WINDOW), 0) % tq
    ck = lax.broadcasted_iota(jnp.int32, (rows, WINDOW), 1)
    ok_prev = (ck >= rq) & (n > 0)
    ok_next = (ck <= rq) & (n < nq - 1)
    rcol = lax.broadcasted_iota(jnp.int32, (rows, 1), 0) // tq
    ndims = (((1,), (1,)), ((), ()))
    ones = jnp.ones((nk, dh), BF16)

    lane = lax.broadcasted_iota(jnp.int32, (tq, 2 * dh), 1)
    for kh in range(ATT_KV_HEADS):
        pair, e = divmod(kh, 2)
        mine = (lane // (dh // 2)) % 2 == e
        hs = slice(kh * dh, (kh + 1) * dh)
        ps = slice(pair * 2 * dh, (pair + 1) * 2 * dh)
        qs = jnp.concatenate(
            [jnp.where(mine, q_ref[:, (pair * group + g) * 2 * dh:(pair * group + g + 1) * 2 * dh], 0)
             for g in range(group)], axis=0)
        k_all = jnp.concatenate([kp_ref[:, ps], kc_ref[:, ps], kn_ref[:, ps], kx_ref[:, ps]], axis=0)
        v_all = jnp.concatenate([vp_ref[:, hs], vc_ref[:, hs], vn_ref[:, hs], vx_ref[:, hs]], axis=0)
        s = lax.dot_general(qs, k_all, ndims, preferred_element_type=F32) * (scale * LOG2E)
        s = jnp.concatenate([jnp.where(ok_prev, s[:, :WINDOW], NEG_INF), s[:, WINDOW:WINDOW + tq],
                             jnp.where(ok_next, s[:, WINDOW + tq:win], NEG_INF), s[:, win:]], axis=1)
        sink = jnp.zeros((rows, 1), F32)
        for g in range(group):
            sink = jnp.where(rcol == g, sink_ref[kh * group + g] * LOG2E, sink)
        m = jnp.maximum(jnp.max(s, axis=-1, keepdims=True), sink)
        p = jnp.exp2(s - m).astype(BF16)
        pv = jnp.dot(p, jnp.concatenate([v_all, ones], axis=1), preferred_element_type=F32)
        o = pv[:, :dh] / (pv[:, dh:dh + 1] + jnp.exp2(sink - m))
        for g in range(group):
            o_ref[:, (kh * group + g) * dh:(kh * group + g + 1) * dh] = (
                o[g * tq:(g + 1) * tq, :].astype(o_ref.dtype))


def _attn_call(qkv, kv_ctx, sink, *, batch, seq, n_ctx, tq, name="window_attn"):
    dh = ATT_HEAD_DIM
    kvw = ATT_KV_HEADS * dh
    qw = qkv.shape[1] - 2 * kvw
    group = qw // kvw
    nq = seq // tq
    wpb = seq // WINDOW
    step = tq // WINDOW
    k_blk = qw // kvw
    v_blk = k_blk + 1

    def prev_map(col):
        return lambda b, n: (b * wpb + jnp.maximum(n * step - 1, 0), col)

    def cur_map(col):
        return lambda b, n: (b * nq + n, col)

    def next_map(col):
        return lambda b, n: (b * wpb + jnp.minimum((n + 1) * step, wpb - 1), col)

    return pl.pallas_call(
        functools.partial(_attn_kernel, nq=nq, tq=tq, group=group),
        grid=(batch, nq),
        in_specs=[
            pl.BlockSpec(memory_space=pltpu.SMEM),
            pl.BlockSpec((tq, qw), lambda b, n: (b * nq + n, 0)),
            pl.BlockSpec((WINDOW, kvw), prev_map(k_blk)),
            pl.BlockSpec((tq, kvw), cur_map(k_blk)),
            pl.BlockSpec((WINDOW, kvw), next_map(k_blk)),
            pl.BlockSpec((WINDOW, kvw), prev_map(v_blk)),
            pl.BlockSpec((tq, kvw), cur_map(v_blk)),
            pl.BlockSpec((WINDOW, kvw), next_map(v_blk)),
            pl.BlockSpec((n_ctx, kvw), lambda b, n: (b, 0)),
            pl.BlockSpec((n_ctx, kvw), lambda b, n: (b, 1)),
        ],
        out_specs=pl.BlockSpec((tq, qw), lambda b, n: (b * nq + n, 0)),
        out_shape=jax.ShapeDtypeStruct((batch * seq, qw), BF16),
        compiler_params=_params(("parallel", "arbitrary")),
        name=name,
    )(sink, qkv, qkv, qkv, qkv, qkv, qkv, qkv, kv_ctx, kv_ctx)


def _rope_angles(pos, dim, base):
    inv_freq = base ** (-np.arange(0, dim, 2, dtype=np.float64) / dim)
    return pos.astype(np.float64)[:, None] * inv_freq[None, :]


def _table(cos_parts, sin_parts, tab_fn):
    return (jnp.asarray(np.concatenate(cos_parts), F32), jnp.asarray(np.concatenate(sin_parts), F32), tab_fn)


def _axial_tables(seq):
    rows = np.arange(seq) // GRID_W
    cols = np.arange(seq) % GRID_W
    half = ATT_HEAD_DIM // 2
    ar = _rope_angles(rows, half, ROPE_BASE)
    ac = _rope_angles(cols, half, ROPE_BASE)
    cos = np.concatenate([np.cos(ar), np.cos(ac), np.cos(ar), np.cos(ac)], axis=-1)
    sin = np.concatenate([np.sin(ar), np.sin(ac), np.sin(ar), np.sin(ac)], axis=-1)
    return cos, sin


def _pair_layout_blocks(n_q_heads, n_kv_heads):
    group = n_q_heads // n_kv_heads

    def pair(a, b):
        return [4 * a, 4 * a + 2, 4 * b, 4 * b + 2, 4 * a + 1, 4 * a + 3, 4 * b + 1, 4 * b + 3]

    blocks = []
    for p in range(n_kv_heads // 2):
        for g in range(group):
            blocks += pair(2 * p * group + g, (2 * p + 1) * group + g)
    for p in range(n_kv_heads // 2):
        blocks += pair(n_q_heads + 2 * p, n_q_heads + 2 * p + 1)
    blocks += range(4 * (n_q_heads + n_kv_heads), 4 * (n_q_heads + 2 * n_kv_heads))
    return blocks


def kernel(x, c, ctx, c_ctx, ada_w, ada_b, norm_mix_g, norm_mlp_g, mlp_w1, mlp_w2, ret_w_in, ret_w_out,
           ret_decay_fwd, ret_decay_bwd, attn_w_in, attn_w_out, attn_sink, final_norm_g):
    batch, seq, d = x.shape
    n_ctx = ctx.shape[1]
    assert ada_w.shape[0] == 2 and ret_w_in.shape[0] == 1 and attn_w_in.shape[0] == 1
    assert batch + 1 <= MOD_ROWS
    ret_dk = d // RET_HEADS
    ret_dv = 2 * ret_dk
    qk_w = RET_HEADS * ret_dk
    att_qw = attn_w_out.shape[1]
    att_kvw = ATT_KV_HEADS * ATT_HEAD_DIM
    ctx_row = batch

    cond = jnp.concatenate([c, c_ctx[None, :], jnp.zeros((MOD_ROWS - batch - 1, d), F32)], axis=0)
    mod = _ada_call(cond, ada_w, ada_b)
    mod4 = mod.reshape(mod.shape[0], MOD_ROWS, 1, mod.shape[2])

    xl = x.reshape(batch * seq, d)
    xc = ctx.reshape(batch * n_ctx, d)
    t_ctx = batch * n_ctx
    lat = dict(rows_per_mod=seq, mod_row0=0)
    cx = dict(rows_per_mod=t_ctx, mod_row0=ctx_row)

    ret_ang = _rope_angles(np.arange(seq), ret_dk, ROPE_BASE)
    ret_cos, ret_sin = np.cos(ret_ang), np.sin(ret_ang)
    att_cos, att_sin = _axial_tables(seq)

    tm, tn, tn_ctx = 1024, 1024, 512
    k_scale = ret_dk ** -0.5
    tps = seq // tm
    one = np.ones((tm, LANES))
    zero = np.zeros((tm, LANES))

    def ret_tab_ctx(i, j):
        q_tiles = qk_w // tn_ctx
        return jnp.where((j >= q_tiles) & (j < 2 * q_tiles), 1, 0)

    assert t_ctx == tm
    ctx_tab = _table([one, one * k_scale], [zero, zero], ret_tab_ctx)
    p_ctx, w_in = _proj_call(xc, norm_mix_g[0], mod4, 0, ret_w_in, tm=tm, tn=tn_ctx, rope="half",
                             tables=[ctx_tab], emit_w=True, name="ret_proj_ctx", **cx)
    assert ATT_KV_HEADS % 2 == 0
    pair_order = tuple(_pair_layout_blocks(att_qw // ATT_HEAD_DIM, ATT_KV_HEADS))
    casts = [(ret_w_out, 0, 32, None), (mlp_w1, 0, 16, None), (mlp_w2, 0, 64, None),
             (attn_w_in, 0, 16, pair_order), (attn_w_out, 0, 16, None)]
    tn_lat, third = 1536, 512
    assert qk_w % third == 0 and tn_lat == 3 * third

    def lat_section(part):
        def tab_fn(i, j):
            col = j * tn_lat + part * third
            return jnp.where(col < qk_w, i % tps, jnp.where(col < 2 * qk_w, tps + i % tps, 2 * tps))
        return tab_fn

    lat_tabs = [_table([ret_cos, ret_cos * k_scale, one], [ret_sin, ret_sin * k_scale, zero], lat_section(p))
                for p in range(3)]
    p_lat, w_out, w1, w2, att_w_in, att_w_out = _proj_call(
        xl, norm_mix_g[0], mod4, 0, w_in, tm=tm, tn=tn_lat, rope="half", tables=lat_tabs,
        ns=8, casts=casts, name="ret_proj_lat", **lat)
    ret = dict(batch=batch, dk=ret_dk, dv=ret_dv)
    y_ctx, sc_f, sc_b = _ret_call(p_ctx, ret_decay_fwd[0], ret_decay_bwd[0], seq=n_ctx, ch=min(256, n_ctx),
                                  emit_state=True, name="retention_ctx", **ret)
    y_lat, = _ret_call(p_lat, ret_decay_fwd[0], ret_decay_bwd[0], (sc_f, sc_b), seq=seq, ch=256,
                       emit_state=False, name="retention_lat", **ret)
    xl = _out_call(y_lat, w_out, xl, mod4, 0, tm=1024, tn=1024, name="ret_out_lat", **lat)
    xc = _out_call(y_ctx, w_out, xc, mod4, 0, tm=t_ctx, tn=512, name="ret_out_ctx", **cx)
    xl, w1_next, w2_next = _mlp_call(xl, norm_mlp_g[0], mod4, 0, w1, w2, tm=1024, tf=1024,
                                     casts=[(mlp_w1, 1, 16, None), (mlp_w2, 1, 64, None)], name="mlp0_lat",
                                     **lat)
    xc, = _mlp_call(xc, norm_mlp_g[0], mod4, 0, w1, w2, tm=t_ctx, tf=1024, name="mlp0_ctx", **cx)

    kv_ctx, = _proj_call(xc, norm_mix_g[1], mod4, 1, att_w_in, tm=t_ctx, tn=512, n_off=att_qw // 512,
                         n_cols=2 * att_kvw, name="attn_proj_ctx", **cx)
    att_tn = 2 * att_kvw
    assert att_qw % att_tn == 0
    q_tiles = att_qw // att_tn
    qkv, = _proj_call(xl, norm_mix_g[1], mod4, 1, att_w_in, tm=tm, tn=att_tn, rope="half",
                      tables=[_table([att_cos, one], [att_sin, zero], lambda i, j: i % tps),
                              _table([att_cos, one], [att_sin, zero],
                                     lambda i, j: jnp.where(j < q_tiles, i % tps, tps))],
                      ns=2, name="attn_proj_lat", **lat)
    o = _attn_call(qkv, kv_ctx, attn_sink[0], batch=batch, seq=seq, n_ctx=n_ctx, tq=WINDOW)
    xl = _out_call(o, att_w_out, xl, mod4, 1, tm=1024, tn=1024, name="attn_out_lat", **lat)
    xl, = _mlp_call(xl, norm_mlp_g[1], mod4, 1, w1_next, w2_next, tm=1024, tf=1024, final_g=final_norm_g,
                    name="mlp1_lat", **lat)
    return xl.reshape(batch, seq, d)
```
